```python
import jax
import jax.numpy as jnp
from jax import lax

D_MODEL = 1024
BATCH = 8
SEQ = 2048
DEPTH = 1

HEAD_DIM = 64
NSA_HEADS = 8
NSA_KV_HEADS = 2
NSA_GROUP = NSA_HEADS // NSA_KV_HEADS
NSA_WIDTH = NSA_HEADS * HEAD_DIM
NSA_KV_WIDTH = NSA_KV_HEADS * HEAD_DIM
CMP_LEN = 32
CMP_STRIDE = 16
CMP_HIDDEN = 128
SEL_BLOCK = 64
TOP_N = 8
WINDOW = 512
Q_BLOCK = 128
FORCE_BONUS = 1000.0
RWKV_HEADS = 8
RWKV_HEAD_DIM = 64
RWKV_WIDTH = RWKV_HEADS * RWKV_HEAD_DIM
DECAY_RANK = 64
AAA_RANK = 64
GATE_RANK = 128
GN_EPS = 64e-5
MEM_TOKENS = 256
MEM_HEADS = 4
MEM_HEAD_DIM = 128
MEM_WIDTH = MEM_HEADS * MEM_HEAD_DIM
N_BRANCHES = 3
BRANCH_WIDTH = 512
N_EXPERTS = 32
TOP_K = 4
EXPERT_FF = 1024
SWIGLU_ALPHA = 1.702
SWIGLU_LIMIT = 7.0

ROPE_THETA = 10000.0
RMS_EPS = 1e-6

SHIFT_WIDTH = 3 * RWKV_WIDTH + DECAY_RANK + AAA_RANK + GATE_RANK
IN_SPLITS = (NSA_WIDTH, 6 * NSA_KV_WIDTH, 3 * NSA_HEADS, SHIFT_WIDTH, MEM_WIDTH, N_BRANCHES * D_MODEL)
IN_WIDTH = NSA_WIDTH + 6 * NSA_KV_WIDTH + 3 * NSA_HEADS + SHIFT_WIDTH + MEM_WIDTH + N_BRANCHES * D_MODEL
RWKV_SPLITS = (RWKV_WIDTH, RWKV_WIDTH, RWKV_WIDTH, DECAY_RANK, AAA_RANK, GATE_RANK)

kernel_name = "hybrid_nsa_rwkv7_memory_moe"


def _split(t, widths):
    out, start = [], 0
    for w in widths:
        out.append(t[..., start:start + w])
        start += w
    return out


def rmsnorm(x, g):
    xf = x.astype(jnp.float32)
    y = xf * lax.rsqrt(jnp.mean(xf * xf, axis=-1, keepdims=True) + RMS_EPS)
    return (y * g.astype(jnp.float32)).astype(x.dtype)


def rope_tables(pos, dim):
    inv = ROPE_THETA ** (-jnp.arange(0, dim, 2, dtype=jnp.float32) / dim)
    ang = pos.astype(jnp.float32)[:, None] * inv[None, :]
    ang = jnp.concatenate([ang, ang], axis=-1)
    return jnp.cos(ang), jnp.sin(ang)


def apply_rope(x, cos, sin):
    x1, x2 = jnp.split(x, 2, axis=-1)
    rot = jnp.concatenate([-x2, x1], axis=-1)
    return (x * cos + rot * sin).astype(x.dtype)


def masked_softmax(s, mask):
    s = jnp.where(mask, s.astype(jnp.float32), -jnp.inf)
    m = jnp.max(s, axis=-1, keepdims=True)
    m = jnp.where(jnp.isfinite(m), m, 0.0)
    e = jnp.where(mask, jnp.exp(s - m), 0.0)
    return e / jnp.maximum(jnp.sum(e, axis=-1, keepdims=True), 1e-30)


def nsa_mixer(q, kv, gate_logits, q_norm, k_norm, cmp_pe, cmp_w1, cmp_b1, cmp_w2, cos, sin):
    B, S, _ = q.shape
    G, HG, Dh = NSA_KV_HEADS, NSA_GROUP, HEAD_DIM
    scale = Dh ** -0.5
    pos = jnp.arange(S)
    q = apply_rope(rmsnorm(q.reshape(B, S, G, HG, Dh), q_norm), cos[:, None, None], sin[:, None, None])
    k_cmp, v_cmp, k_slc, v_slc, k_win, v_win = [t.reshape(B, S, G, Dh) for t in jnp.split(kv, 6, axis=-1)]

    n_cmp = (S - CMP_LEN) // CMP_STRIDE + 1
    cmp_start = jnp.arange(n_cmp) * CMP_STRIDE
    cmp_end = cmp_start + CMP_LEN - 1
    gather_idx = cmp_start[:, None] + jnp.arange(CMP_LEN)[None, :]

    def compress(t, j):
        blocks = t[:, gather_idx] + cmp_pe[j][None, None, :, None, :]
        blocks = jnp.moveaxis(blocks, 3, 2).reshape(B, n_cmp, G, CMP_LEN * Dh)
        return jax.nn.gelu(blocks @ cmp_w1[j] + cmp_b1[j]) @ cmp_w2[j]

    cc, sc = rope_tables(cmp_end, Dh)
    kc = apply_rope(rmsnorm(compress(k_cmp, 0), k_norm[0]), cc[:, None], sc[:, None])
    vc = compress(v_cmp, 1)
    s_cmp = jnp.einsum('bsghd,bcgd->bsghc', q, kc) * scale
    p_cmp = masked_softmax(s_cmp, (cmp_end[None, :] <= pos[:, None])[None, :, None, None, :])
    o_cmp = jnp.einsum('bsghc,bcgd->bsghd', p_cmp.astype(vc.dtype), vc)

    n_sel = S // SEL_BLOCK
    n_top = min(TOP_N, n_sel)
    blk = jnp.arange(n_sel)
    blk_start = blk * SEL_BLOCK
    cover = ((cmp_start[:, None] < blk_start[None, :] + SEL_BLOCK)
             & (cmp_start[:, None] + CMP_LEN > blk_start[None, :])).astype(jnp.float32)
    imp = jnp.einsum('bsghc,cn->bsgn', p_cmp, cover)
    cur = pos // SEL_BLOCK
    forced = (blk[None, :] == 0) | (blk[None, :] == cur[:, None]) | (blk[None, :] == cur[:, None] - 1)
    imp = jnp.where(forced[None, :, None, :], imp + FORCE_BONUS, imp)
    imp = jnp.where((blk[None, :] <= cur[:, None])[None, :, None, :], imp, -jnp.inf)
    _, sel_idx = lax.top_k(imp, n_top)

    ks = apply_rope(rmsnorm(k_slc, k_norm[1]), cos[:, None], sin[:, None])
    kw = apply_rope(rmsnorm(k_win, k_norm[2]), cos[:, None], sin[:, None])
    k_blocks = jnp.transpose(ks.reshape(B, n_sel, SEL_BLOCK, G, Dh), (0, 3, 1, 2, 4))
    v_blocks = jnp.transpose(v_slc.reshape(B, n_sel, SEL_BLOCK, G, Dh), (0, 3, 1, 2, 4))
    kw_pad = jnp.concatenate([jnp.zeros((B, WINDOW, G, Dh), kw.dtype), kw], axis=1)
    vw_pad = jnp.concatenate([jnp.zeros((B, WINDOW, G, Dh), v_win.dtype), v_win], axis=1)
    n_chunks = S // Q_BLOCK
    b_ix = jnp.arange(B)[:, None, None, None]
    g_ix = jnp.arange(G)[None, None, :, None]

    def to_chunks(t):
        return jnp.moveaxis(t.reshape((B, n_chunks, Q_BLOCK) + t.shape[2:]), 1, 0)

    def from_chunks(t):
        return jnp.moveaxis(t, 0, 1).reshape((B, S) + t.shape[3:])

    def chunk_fn(args):
        c, qc, idx = args
        tq = c * Q_BLOCK + jnp.arange(Q_BLOCK)
        k_sel = k_blocks[b_ix, g_ix, idx]
        v_sel = v_blocks[b_ix, g_ix, idx]
        s = jnp.einsum('bqghd,bqgnld->bqghnl', qc, k_sel) * scale
        kpos = idx[..., None] * SEL_BLOCK + jnp.arange(SEL_BLOCK)
        m = (kpos <= tq[None, :, None, None, None]).reshape(B, Q_BLOCK, G, 1, n_top * SEL_BLOCK)
        p = masked_softmax(s.reshape(B, Q_BLOCK, G, HG, n_top * SEL_BLOCK), m)
        p = p.reshape(B, Q_BLOCK, G, HG, n_top, SEL_BLOCK)
        o_s = jnp.einsum('bqghnl,bqgnld->bqghd', p.astype(v_sel.dtype), v_sel)
        kb = lax.dynamic_slice_in_dim(kw_pad, c * Q_BLOCK, WINDOW + Q_BLOCK, axis=1)
        vb = lax.dynamic_slice_in_dim(vw_pad, c * Q_BLOCK, WINDOW + Q_BLOCK, axis=1)
        kp = c * Q_BLOCK - WINDOW + jnp.arange(WINDOW + Q_BLOCK)
        mw = (kp[None, :] >= 0) & (kp[None, :] <= tq[:, None]) & (kp[None, :] > tq[:, None] - WINDOW)
        sw = jnp.einsum('bqghd,bkgd->bqghk', qc, kb) * scale
        pw = masked_softmax(sw, mw[None, :, None, None, :])
        o_w = jnp.einsum('bqghk,bkgd->bqghd', pw.astype(vb.dtype), vb)
        return o_s, o_w

    o_slc, o_win = lax.map(chunk_fn, (jnp.arange(n_chunks), to_chunks(q), to_chunks(sel_idx)))
    o_slc, o_win = from_chunks(o_slc), from_chunks(o_win)

    gates = jax.nn.sigmoid(gate_logits.astype(jnp.float32)).reshape(B, S, G, HG, 3).astype(q.dtype)
    o = gates[..., 0:1] * o_cmp + gates[..., 1:2] * o_slc + gates[..., 2:3] * o_win
    return o.reshape(B, S, NSA_WIDTH)


def rwkv7_mixer(p, shift_mix, w0, w_up, a0, a_up, g_up, k_k, k_a, r_k, ln_w, ln_b):
    B, S, _ = p.shape
    H, N = RWKV_HEADS, RWKV_HEAD_DIM
    prev = jnp.concatenate([jnp.zeros_like(p[:, :1]), p[:, :-1]], axis=1)
    p = p + (prev - p) * shift_mix
    r, k, v, xw, xa, xg = _split(p, RWKV_SPLITS)
    w = -jax.nn.softplus(-(w0 + jnp.tanh(xw) @ w_up)) - 0.5
    a = jax.nn.sigmoid(a0 + xa @ a_up)
    g = jax.nn.sigmoid(xg) @ g_up

    def heads(t):
        return t.reshape(B, S, H, N).astype(jnp.float32)

    kk = heads(k * k_k)
    kk = kk / jnp.maximum(jnp.linalg.norm(kk, axis=-1, keepdims=True), 1e-12)
    k = k * (1.0 + (a - 1.0) * k_a)
    r4, k4, v4, a4 = heads(r), heads(k), heads(v), heads(a)
    decay = jnp.exp(-jnp.exp(heads(w)))

    def step(state, inp):
        r_t, w_t, k_t, v_t, a_t, b_t = inp
        sa = jnp.einsum('bhvk,bhk->bhv', state, a_t)
        state = (state * w_t[:, :, None, :] + sa[..., None] * b_t[:, :, None, :]
                 + v_t[..., None] * k_t[:, :, None, :])
        return state, jnp.einsum('bhvk,bhk->bhv', state, r_t)

    seqs = tuple(jnp.moveaxis(t, 1, 0) for t in (r4, decay, k4, v4, -kk, kk * a4))
    _, y = lax.scan(step, jnp.zeros((B, H, N, N), jnp.float32), seqs)
    y = jnp.moveaxis(y, 0, 1)
    mu = jnp.mean(y, axis=-1, keepdims=True)
    var = jnp.mean(jnp.square(y - mu), axis=-1, keepdims=True)
    y = ((y - mu) * lax.rsqrt(var + GN_EPS)).reshape(B, S, H * N) * ln_w + ln_b
    bonus = jnp.sum(r4 * k4 * r_k, axis=-1, keepdims=True) * v4
    y = (y + bonus.reshape(B, S, H * N)) * g
    return y.astype(p.dtype)


def memory_mixer(q, mem_n, w_kv, q_norm, k_norm):
    B, S, _ = q.shape
    M = mem_n.shape[1]
    q = rmsnorm(q.reshape(B, S, MEM_HEADS, MEM_HEAD_DIM), q_norm)
    k, v = jnp.split(mem_n @ w_kv, 2, axis=-1)
    k = rmsnorm(k.reshape(B, M, MEM_HEADS, MEM_HEAD_DIM), k_norm)
    v = v.reshape(B, M, MEM_HEADS, MEM_HEAD_DIM)
    s = jnp.einsum('bshd,bmhd->bhsm', q, k) * (MEM_HEAD_DIM ** -0.5)
    pr = jax.nn.softmax(s.astype(jnp.float32), axis=-1)
    o = jnp.einsum('bhsm,bmhd->bshd', pr.astype(v.dtype), v)
    return o.reshape(B, S, MEM_WIDTH)


def moe_ffn(h, router_w, router_b, w1, b1, w2, b2):
    B, S, D = h.shape
    t = h.reshape(B * S, D)
    logits = (t @ router_w + router_b).astype(jnp.float32)
    top_v, top_i = lax.top_k(logits, TOP_K)
    top_w = jax.nn.softmax(top_v, axis=-1)
    gate = jnp.sum(jax.nn.one_hot(top_i, N_EXPERTS, dtype=jnp.float32) * top_w[..., None], axis=1)
    gate = gate.astype(t.dtype)
    out = jnp.zeros_like(t)
    for e in range(N_EXPERTS):
        hid = t @ w1[e] + b1[e]
        x_glu = jnp.minimum(hid[:, ::2], SWIGLU_LIMIT)
        x_lin = jnp.clip(hid[:, 1::2], -SWIGLU_LIMIT, SWIGLU_LIMIT)
        act = x_glu * jax.nn.sigmoid(SWIGLU_ALPHA * x_glu) * (x_lin + 1.0)
        out = out + gate[:, e:e + 1] * (act @ w2[e] + b2[e])
    return out.reshape(B, S, D)


def setup_inputs(seed: int = 0) -> dict:
    key = jax.random.key(seed)
    ks = iter(jax.random.split(key, 48))
    L = DEPTH

    def nrm(shape, scale):
        return scale * jax.random.normal(next(ks), shape, jnp.float32)

    def gain(shape):
        return 1.0 + 0.05 * jax.random.normal(next(ks), shape, jnp.float32)

    return {
        "x": nrm((BATCH, SEQ, D_MODEL), 1.0),
        "mem": nrm((BATCH, MEM_TOKENS, D_MODEL), 1.0),
        "g_mix": gain((L, D_MODEL)),
        "g_mem": gain((L, D_MODEL)),
        "w_in": nrm((L, D_MODEL, IN_WIDTH), D_MODEL ** -0.5),
        "b_merge": nrm((L, N_BRANCHES * D_MODEL), 0.01),
        "nsa_q_norm": gain((L, HEAD_DIM)),
        "nsa_k_norm": gain((L, 3, HEAD_DIM)),
        "cmp_pe": nrm((L, 2, CMP_LEN, HEAD_DIM), 0.1),
        "cmp_w1": nrm((L, 2, CMP_LEN * HEAD_DIM, CMP_HIDDEN), (CMP_LEN * HEAD_DIM) ** -0.5),
        "cmp_b1": nrm((L, 2, CMP_HIDDEN), 0.01),
        "cmp_w2": nrm((L, 2, CMP_HIDDEN, HEAD_DIM), CMP_HIDDEN ** -0.5),
        "rwkv_shift_mix": jax.random.uniform(next(ks), (L, SHIFT_WIDTH), jnp.float32),
        "rwkv_w0": -1.0 + nrm((L, RWKV_WIDTH), 0.5),
        "rwkv_w_up": nrm((L, DECAY_RANK, RWKV_WIDTH), 0.1),
        "rwkv_a0": nrm((L, RWKV_WIDTH), 0.1),
        "rwkv_a_up": nrm((L, AAA_RANK, RWKV_WIDTH), 0.1),
        "rwkv_g_up": nrm((L, GATE_RANK, RWKV_WIDTH), GATE_RANK ** -0.5),
        "rwkv_k_k": 0.85 + nrm((L, RWKV_WIDTH), 0.05),
        "rwkv_k_a": 1.0 + nrm((L, RWKV_WIDTH), 0.05),
        "rwkv_r_k": nrm((L, RWKV_HEADS, RWKV_HEAD_DIM), 0.1),
        "rwkv_ln_w": gain((L, RWKV_WIDTH)),
        "rwkv_ln_b": nrm((L, RWKV_WIDTH), 0.01),
        "mem_w_kv": nrm((L, D_MODEL, 2 * MEM_WIDTH), D_MODEL ** -0.5),
        "mem_q_norm": gain((L, MEM_HEAD_DIM)),
        "mem_k_norm": gain((L, MEM_HEAD_DIM)),
        "w_branch": nrm((L, N_BRANCHES, BRANCH_WIDTH, D_MODEL), BRANCH_WIDTH ** -0.5),
        "w_out": nrm((L, D_MODEL, D_MODEL), D_MODEL ** -0.5),
        "g_ffn": gain((L, D_MODEL)),
        "router_w": nrm((L, D_MODEL, N_EXPERTS), D_MODEL ** -0.5),
        "router_b": nrm((L, N_EXPERTS), 0.01),
        "exp_w1": nrm((L, N_EXPERTS, D_MODEL, 2 * EXPERT_FF), D_MODEL ** -0.5),
        "exp_b1": nrm((L, N_EXPERTS, 2 * EXPERT_FF), 0.01),
        "exp_w2": nrm((L, N_EXPERTS, EXPERT_FF, D_MODEL), EXPERT_FF ** -0.5),
        "exp_b2": nrm((L, N_EXPERTS, D_MODEL), 0.01),
    }


def reference(x, mem, g_mix, g_mem, w_in, b_merge, nsa_q_norm, nsa_k_norm, cmp_pe, cmp_w1, cmp_b1, cmp_w2,
              rwkv_shift_mix, rwkv_w0, rwkv_w_up, rwkv_a0, rwkv_a_up, rwkv_g_up, rwkv_k_k, rwkv_k_a, rwkv_r_k,
              rwkv_ln_w, rwkv_ln_b, mem_w_kv, mem_q_norm, mem_k_norm, w_branch, w_out, g_ffn,
              router_w, router_b, exp_w1, exp_b1, exp_w2, exp_b2):
    B, S, D = x.shape
    cos, sin = rope_tables(jnp.arange(S), HEAD_DIM)
    for l in range(DEPTH):
        h = rmsnorm(x, g_mix[l])
        q_nsa, kv_nsa, gate_nsa, p_rwkv, q_mem, merge_logits = _split(h @ w_in[l], IN_SPLITS)
        o_a = nsa_mixer(q_nsa, kv_nsa, gate_nsa, nsa_q_norm[l], nsa_k_norm[l], cmp_pe[l], cmp_w1[l],
                        cmp_b1[l], cmp_w2[l], cos, sin)
        o_b = rwkv7_mixer(p_rwkv, rwkv_shift_mix[l], rwkv_w0[l], rwkv_w_up[l], rwkv_a0[l], rwkv_a_up[l],
                          rwkv_g_up[l], rwkv_k_k[l], rwkv_k_a[l], rwkv_r_k[l], rwkv_ln_w[l], rwkv_ln_b[l])
        o_m = memory_mixer(q_mem, rmsnorm(mem, g_mem[l]), mem_w_kv[l], mem_q_norm[l], mem_k_norm[l])
        branches = jnp.einsum('nbsc,ncd->bsnd', jnp.stack([o_a, o_b, o_m], axis=0), w_branch[l])
        gates = jax.nn.sigmoid((merge_logits + b_merge[l]).astype(jnp.float32)).reshape(B, S, N_BRANCHES, D)
        mixed = jnp.sum(gates.astype(branches.dtype) * branches, axis=2)
        x = x + mixed @ w_out[l]
        x = x + moe_ffn(rmsnorm(x, g_ffn[l]), router_w[l], router_b[l], exp_w1[l], exp_b1[l],
                        exp_w2[l], exp_b2[l])
    return x
```

```python
import functools

import jax
import jax.numpy as jnp
from jax import lax
from jax.experimental import pallas as pl
from jax.experimental.pallas import tpu as pltpu

F32 = jnp.float32
BF16 = jnp.bfloat16
HI = lax.Precision.HIGHEST

D_MODEL = 1024
HEAD_DIM = 64
NSA_HEADS = 8
NSA_KV_HEADS = 2
NSA_GROUP = NSA_HEADS // NSA_KV_HEADS
NSA_WIDTH = NSA_HEADS * HEAD_DIM
NSA_KV_WIDTH = NSA_KV_HEADS * HEAD_DIM
CMP_LEN = 32
CMP_STRIDE = 16
CMP_HIDDEN = 128
SEL_BLOCK = 64
TOP_N = 8
WINDOW = 512
Q_BLOCK = 128
FORCE_BONUS = 1000.0
RWKV_HEADS = 8
RWKV_HEAD_DIM = 64
RWKV_WIDTH = RWKV_HEADS * RWKV_HEAD_DIM
DECAY_RANK = 64
AAA_RANK = 64
GATE_RANK = 128
GN_EPS = 64e-5
MEM_HEADS = 4
MEM_HEAD_DIM = 128
MEM_WIDTH = MEM_HEADS * MEM_HEAD_DIM
N_BRANCHES = 3
BRANCH_WIDTH = 512
N_EXPERTS = 32
TOP_K = 4
EXPERT_FF = 1024
SWIGLU_ALPHA = 1.702
SWIGLU_LIMIT = 7.0
ROPE_THETA = 10000.0
RMS_EPS = 1e-6

LANE = 128
NEG_BIG = -1e30

RWKV_PAD = 2048
COL_RWKV = 0
COL_MERGE = COL_RWKV + RWKV_PAD
COL_QNSA = COL_MERGE + N_BRANCHES * D_MODEL
COL_QMEM = COL_QNSA + NSA_WIDTH
COL_KV = COL_QMEM + MEM_WIDTH
COL_GATE = COL_KV + 6 * NSA_KV_WIDTH
IN_PAD = 7168
RW_R, RW_K, RW_V, RW_XW, RW_XA, RW_XG = 0, 512, 1024, 1536, 1664, 1792

RWKV_CHUNK = 64
VMEM_LIMIT = 56 * 1024 * 1024


def _dot(a, b, prec=None):
    return jnp.dot(a, b, preferred_element_type=F32, precision=prec)


def _dot_nt(a, b, prec=None):
    return lax.dot_general(a, b, (((1,), (1,)), ((), ())), preferred_element_type=F32, precision=prec)


def _dot_tn(a, b, prec=None):
    return lax.dot_general(a, b, (((0,), (0,)), ((), ())), preferred_element_type=F32, precision=prec)


def _seg_matrix(width, seg):
    r = lax.broadcasted_iota(jnp.int32, (width, width), 0) // seg
    c = lax.broadcasted_iota(jnp.int32, (width, width), 1) // seg
    return (r == c).astype(F32)


def _params(sem):
    return pltpu.CompilerParams(dimension_semantics=sem, vmem_limit_bytes=VMEM_LIMIT)


def _inproj_kernel(x_ref, g_ref, w_ref, o_ref, hn_ref):
    @pl.when(pl.program_id(1) == 0)
    def _():
        x = x_ref[...]
        ms = jnp.mean(x * x, axis=-1, keepdims=True)
        hn_ref[...] = (x * lax.rsqrt(ms + RMS_EPS) * g_ref[...]).astype(BF16)

    o_ref[...] = _dot(hn_ref[...], w_ref[...])


def _inproj(x2, g, w, tm=1024, tn=512):
    t, d = x2.shape
    n = w.shape[1]
    return pl.pallas_call(
        _inproj_kernel,
        grid=(t // tm, n // tn),
        in_specs=[
            pl.BlockSpec((tm, d), lambda i, j: (i, 0)),
            pl.BlockSpec((1, d), lambda i, j: (0, 0)),
            pl.BlockSpec((d, tn), lambda i, j: (0, j)),
        ],
        out_specs=pl.BlockSpec((tm, tn), lambda i, j: (i, j)),
        out_shape=jax.ShapeDtypeStruct((t, n), F32),
        scratch_shapes=[pltpu.VMEM((tm, d), BF16)],
        compiler_params=_params(("parallel", "arbitrary")),
        name="inproj",
    )(x2, g, w)


def _rwkv_kernel(p_ref, mix_ref, w0_ref, wup_ref, a0_ref, aup_ref, gup_ref, kk_ref, ka_ref,
                 rk_ref, lnw_ref, lnb_ref, o_ref, state_ref, prev_ref, y_ref):
    c = RWKV_CHUNK
    n = RWKV_HEAD_DIM

    @pl.when(pl.program_id(1) == 0)
    def _():
        state_ref[...] = jnp.zeros_like(state_ref)
        prev_ref[...] = jnp.zeros_like(prev_ref)

    p = p_ref[0]
    row = lax.broadcasted_iota(jnp.int32, p.shape, 0)
    prev = jnp.where(row == 0, prev_ref[...], pltpu.roll(p, 1, axis=0))
    prev_ref[...] = p[c - 1:c, :]
    ps = p + (prev - p) * mix_ref[...]
    r = ps[:, RW_R:RW_R + RWKV_WIDTH]
    k = ps[:, RW_K:RW_K + RWKV_WIDTH]
    v = ps[:, RW_V:RW_V + RWKV_WIDTH]
    xw = ps[:, RW_XW:RW_XW + LANE]
    xa = ps[:, RW_XA:RW_XA + LANE]
    xg = ps[:, RW_XG:RW_XG + LANE]

    z = -(w0_ref[...] + _dot(jnp.tanh(xw), wup_ref[...], HI))
    softplus = jnp.maximum(z, 0.0) + jnp.log1p(jnp.exp(-jnp.abs(z)))
    w = -softplus - 0.5
    logw = -jnp.exp(w)
    a = jax.nn.sigmoid(a0_ref[...] + _dot(xa, aup_ref[...], HI))
    g = _dot(jax.nn.sigmoid(xg), gup_ref[...], HI)

    seg = _seg_matrix(RWKV_WIDTH, n)
    kk = k * kk_ref[...]
    kk_norm = jnp.sqrt(_dot(kk * kk, seg, HI))
    kk = kk / jnp.maximum(kk_norm, 1e-12)
    k = k * (1.0 + (a - 1.0) * ka_ref[...])
    kka = kk * a

    ti = lax.broadcasted_iota(jnp.int32, (c, c), 0)
    tj = lax.broadcasted_iota(jnp.int32, (c, c), 1)
    lower_incl = tj <= ti
    lower_strict = tj < ti
    ltri = lower_incl.astype(F32)
    eye = (ti == tj).astype(F32)

    for h in range(RWKV_HEADS):
        sl = slice(h * n, (h + 1) * n)
        r_h, k_h, v_h, kk_h, kka_h, lw_h = r[:, sl], k[:, sl], v[:, sl], kk[:, sl], kka[:, sl], logw[:, sl]
        cum = _dot(ltri, lw_h, HI)
        einv = jnp.exp(-cum)
        a_t = -kk_h * jnp.exp(cum - lw_h)
        b_t = kka_h * einv
        k_t = k_h * einv
        r_t = r_h * jnp.exp(cum)
        tot = cum[c - 1:c, :]
        dec_end = jnp.exp(tot - cum)
        b_e = kka_h * dec_end
        k_e = k_h * dec_end

        a_ab = jnp.where(lower_strict, _dot_nt(a_t, b_t, HI), 0.0)
        a_ak = jnp.where(lower_strict, _dot_nt(a_t, k_t, HI), 0.0)
        a_rb = jnp.where(lower_incl, _dot_nt(r_t, b_t, HI), 0.0)
        a_rk = jnp.where(lower_incl, _dot_nt(r_t, k_t, HI), 0.0)

        tinv = eye + a_ab
        pw = a_ab
        for _ in range(5):
            pw = _dot(pw, pw, HI)
            tinv = tinv + _dot(tinv, pw, HI)

        s = state_ref[h]
        u = _dot(tinv, _dot_nt(a_t, s, HI) + _dot(a_ak, v_h, HI), HI)
        y = _dot_nt(r_t, s, HI) + _dot(a_rb, u, HI) + _dot(a_rk, v_h, HI)
        state_ref[h] = s * jnp.exp(tot) + _dot_tn(u, b_e, HI) + _dot_tn(v_h, k_e, HI)
        y_ref[:, sl] = y

    y = y_ref[...]
    mu = _dot(y, seg, HI) * (1.0 / n)
    d = y - mu
    var = _dot(d * d, seg, HI) * (1.0 / n)
    yn = d * lax.rsqrt(var + GN_EPS) * lnw_ref[...] + lnb_ref[...]
    bonus = _dot(r * k * rk_ref[...], seg, HI) * v
    o_ref[0] = (yn + bonus) * g


def _rwkv(proj3, mix, w0, wup, a0, aup, gup, k_k, k_a, r_k, ln_w, ln_b):
    b, s, _ = proj3.shape
    c = RWKV_CHUNK
    vec = lambda width: pl.BlockSpec((1, width), lambda i, t: (0, 0))
    mat = lambda rows: pl.BlockSpec((rows, RWKV_WIDTH), lambda i, t: (0, 0))
    return pl.pallas_call(
        _rwkv_kernel,
        grid=(b, s // c),
        in_specs=[
            pl.BlockSpec((1, c, RWKV_PAD), lambda i, t: (i, t, COL_RWKV // RWKV_PAD)),
            vec(RWKV_PAD), vec(RWKV_WIDTH), mat(LANE), vec(RWKV_WIDTH), mat(LANE), mat(LANE),
            vec(RWKV_WIDTH), vec(RWKV_WIDTH), vec(RWKV_WIDTH), vec(RWKV_WIDTH), vec(RWKV_WIDTH),
        ],
        out_specs=pl.BlockSpec((1, c, RWKV_WIDTH), lambda i, t: (i, t, 0)),
        out_shape=jax.ShapeDtypeStruct((b, s, RWKV_WIDTH), F32),
        scratch_shapes=[
            pltpu.VMEM((RWKV_HEADS, RWKV_HEAD_DIM, RWKV_HEAD_DIM), F32),
            pltpu.VMEM((1, RWKV_PAD), F32),
            pltpu.VMEM((c, RWKV_WIDTH), F32),
        ],
        compiler_params=_params(("parallel", "arbitrary")),
        name="rwkv7",
    )(proj3, mix, w0, wup, a0, aup, gup, k_k, k_a, r_k, ln_w, ln_b)


def _rope(x, cos, sin_signed):
    w = x.shape[-1]
    first_half = (lax.broadcasted_iota(jnp.int32, x.shape, 1) % HEAD_DIM) < (HEAD_DIM // 2)
    rot = jnp.where(first_half, pltpu.roll(x, w - HEAD_DIM // 2, axis=1), pltpu.roll(x, HEAD_DIM // 2, axis=1))
    return x * cos + rot * sin_signed


def _head_rmsnorm(x, gain, seg):
    ms = _dot(x * x, seg, HI) * (1.0 / HEAD_DIM)
    return x * lax.rsqrt(ms + RMS_EPS) * gain


def _split_groups(x):
    return [x[:, g * HEAD_DIM:(g + 1) * HEAD_DIM] for g in range(NSA_KV_HEADS)]


def _nsa_prep_kernel(q_ref, ksl_ref, vsl_ref, kwn_ref, vwn_ref, cos_ref, sin_ref, qn_ref, kn_ref,
                     qo_ref, ks_ref, vs_ref, kw_ref, vw_ref):
    seg_q = _seg_matrix(NSA_WIDTH, HEAD_DIM)
    seg_k = _seg_matrix(NSA_KV_WIDTH, HEAD_DIM)
    cos_k = cos_ref[:, :NSA_KV_WIDTH]
    sin_k = sin_ref[:, :NSA_KV_WIDTH]

    q = _rope(_head_rmsnorm(q_ref[0], qn_ref[...], seg_q), cos_ref[...], sin_ref[...])
    qo_ref[0] = (q * (HEAD_DIM ** -0.5)).astype(BF16)

    ks = _rope(_head_rmsnorm(ksl_ref[0], kn_ref[1:2, :], seg_k), cos_k, sin_k)
    kw = _rope(_head_rmsnorm(kwn_ref[0], kn_ref[2:3, :], seg_k), cos_k, sin_k)
    for g, (a, b_, c_, d_) in enumerate(zip(_split_groups(ks), _split_groups(vsl_ref[0]),
                                            _split_groups(kw), _split_groups(vwn_ref[0]))):
        ks_ref[0, g] = a.astype(BF16)
        vs_ref[0, g] = b_.astype(BF16)
        kw_ref[0, g] = c_.astype(BF16)
        vw_ref[0, g] = d_.astype(BF16)


def _nsa_prep(proj3, cos, sin_signed, q_norm, k_norm, tq=512):
    b, s, _ = proj3.shape
    kvw = NSA_KV_WIDTH
    kv_spec = lambda j: pl.BlockSpec((1, tq, kvw), lambda i, t: (i, t, COL_KV // kvw + j))
    out_kv = pl.BlockSpec((1, NSA_KV_HEADS, tq, HEAD_DIM), lambda i, t: (i, 0, t, 0))
    kv_shape = jax.ShapeDtypeStruct((b, NSA_KV_HEADS, s, HEAD_DIM), BF16)
    return pl.pallas_call(
        _nsa_prep_kernel,
        grid=(b, s // tq),
        in_specs=[
            pl.BlockSpec((1, tq, NSA_WIDTH), lambda i, t: (i, t, COL_QNSA // NSA_WIDTH)),
            kv_spec(2), kv_spec(3), kv_spec(4), kv_spec(5),
            pl.BlockSpec((tq, NSA_WIDTH), lambda i, t: (t, 0)),
            pl.BlockSpec((tq, NSA_WIDTH), lambda i, t: (t, 0)),
            pl.BlockSpec((1, NSA_WIDTH), lambda i, t: (0, 0)),
            pl.BlockSpec((3, kvw), lambda i, t: (0, 0)),
        ],
        out_specs=[
            pl.BlockSpec((1, tq, NSA_WIDTH), lambda i, t: (i, t, 0)),
            out_kv, out_kv, out_kv, out_kv,
        ],
        out_shape=[jax.ShapeDtypeStruct((b, s, NSA_WIDTH), BF16), kv_shape, kv_shape, kv_shape, kv_shape],
        compiler_params=_params(("parallel", "parallel")),
        name="nsa_prep",
    )(proj3, proj3, proj3, proj3, proj3, cos, sin_signed, q_norm, k_norm)


def _gelu_tanh(x):
    return 0.5 * x * (1.0 + jnp.tanh(0.7978845608028654 * (x + 0.044715 * x * x * x)))


def _nsa_cmp_kernel(kc_in_ref, vc_in_ref, cos_ref, sin_ref, kn_ref, pe_ref, w1_ref, b1_ref, w2_ref,
                    kc_ref, vc_ref):
    n_cmp = (kc_in_ref.shape[1] - CMP_LEN) // CMP_STRIDE + 1
    n_pad = n_cmp + 1
    zero_row = jnp.zeros((1, NSA_KV_WIDTH), F32)
    outs = []
    for j, src in enumerate((kc_in_ref, vc_in_ref)):
        acc = jnp.zeros((NSA_KV_HEADS * n_pad, CMP_HIDDEN), F32)
        for l in range(CMP_LEN):
            x = src[0, pl.ds(l, n_cmp, stride=CMP_STRIDE), :]
            x = jnp.concatenate([x, zero_row], axis=0)
            xg = jnp.concatenate(_split_groups(x), axis=0) + pe_ref[j, l:l + 1, :]
            acc = acc + _dot(xg.astype(BF16), w1_ref[j, l * HEAD_DIM:(l + 1) * HEAD_DIM, :])
        hid = _gelu_tanh(acc + b1_ref[j])
        out = _dot(hid.astype(BF16), w2_ref[j])
        outs.append(jnp.concatenate([out[g * n_pad:(g + 1) * n_pad] for g in range(NSA_KV_HEADS)], axis=1))
    kc, vc = outs
    seg_k = _seg_matrix(NSA_KV_WIDTH, HEAD_DIM)
    kc = _rope(_head_rmsnorm(kc, kn_ref[0:1, :], seg_k), cos_ref[...], sin_ref[...])
    for g, (a, b_) in enumerate(zip(_split_groups(kc), _split_groups(vc))):
        kc_ref[0, g] = a.astype(BF16)
        vc_ref[0, g] = b_.astype(BF16)


def _nsa_cmp(proj3, cos_c, sin_c, k_norm, pe, w1, b1, w2):
    b, s, _ = proj3.shape
    kvw = NSA_KV_WIDTH
    n_pad = (s - CMP_LEN) // CMP_STRIDE + 2
    full = lambda shape: pl.BlockSpec(shape, lambda i: (0,) * len(shape))
    out_spec = pl.BlockSpec((1, NSA_KV_HEADS, n_pad, HEAD_DIM), lambda i: (i, 0, 0, 0))
    out_shape = jax.ShapeDtypeStruct((b, NSA_KV_HEADS, n_pad, HEAD_DIM), BF16)
    return pl.pallas_call(
        _nsa_cmp_kernel,
        grid=(b,),
        in_specs=[
            pl.BlockSpec((1, s, kvw), lambda i: (i, 0, COL_KV // kvw)),
            pl.BlockSpec((1, s, kvw), lambda i: (i, 0, COL_KV // kvw + 1)),
            full((n_pad, kvw)), full((n_pad, kvw)), full((3, kvw)),
            full((2, CMP_LEN, HEAD_DIM)), full((2, CMP_LEN * HEAD_DIM, CMP_HIDDEN)),
            full((2, 1, CMP_HIDDEN)), full((2, CMP_HIDDEN, HEAD_DIM)),
        ],
        out_specs=[out_spec, out_spec],
        out_shape=[out_shape, out_shape],
        compiler_params=_params(("parallel",)),
        name="nsa_compress",
    )(proj3, proj3, cos_c, sin_c, k_norm, pe, w1, b1, w2)


def _online_softmax_step(carry, s, mask, v):
    m_i, l_i, acc = carry
    s = jnp.where(mask, s, NEG_BIG)
    m_new = jnp.maximum(m_i, jnp.max(s, axis=-1, keepdims=True))
    alpha = jnp.exp(m_i - m_new)
    p = jnp.where(mask, jnp.exp(s - m_new), 0.0)
    l_new = alpha * l_i + jnp.sum(p, axis=-1, keepdims=True)
    acc = alpha * acc + _dot(p.astype(BF16), v)
    return m_new, l_new, acc


def _nsa_attn_kernel(q_ref, gate_ref, kc_ref, vc_ref, ks_ref, vs_ref, kw_ref, vw_ref, o_ref):
    qb = Q_BLOCK
    hg = NSA_GROUP
    g = pl.program_id(1)
    c = pl.program_id(2)
    q = q_ref[0]
    qs = jnp.concatenate([q[:, h * HEAD_DIM:(h + 1) * HEAD_DIM] for h in range(hg)], axis=0)
    rep = lambda cond: jnp.concatenate([jnp.where(cond, 1, 0)] * hg, axis=0) > 0

    row = lax.broadcasted_iota(jnp.int32, (qb, LANE), 0)
    lane = lax.broadcasted_iota(jnp.int32, (qb, LANE), 1)
    lane_f = lane.astype(F32)
    pos = c * qb + row

    n_cp = kc_ref.shape[2]
    s_c = _dot_nt(qs, kc_ref[0, 0])
    cend = lax.broadcasted_iota(jnp.int32, (qb, n_cp), 1) * CMP_STRIDE + CMP_LEN - 1
    cmask = rep(cend <= c * qb + lax.broadcasted_iota(jnp.int32, (qb, n_cp), 0))
    s_m = jnp.where(cmask, s_c, -jnp.inf)
    m = jnp.max(s_m, axis=-1, keepdims=True)
    m = jnp.where(m == -jnp.inf, 0.0, m)
    e = jnp.where(cmask, jnp.exp(s_m - m), 0.0)
    p_c = e / jnp.maximum(jnp.sum(e, axis=-1, keepdims=True), 1e-30)
    o_cmp = _dot(p_c.astype(BF16), vc_ref[0, 0])

    p_sum = p_c[0:qb]
    for h in range(1, hg):
        p_sum = p_sum + p_c[h * qb:(h + 1) * qb]
    ci = lax.broadcasted_iota(jnp.int32, (n_cp, LANE), 0) * CMP_STRIDE
    ni = lax.broadcasted_iota(jnp.int32, (n_cp, LANE), 1) * SEL_BLOCK
    cover = ((ci < ni + SEL_BLOCK) & (ci + CMP_LEN > ni)).astype(F32)
    imp = _dot(p_sum, cover, HI)
    cur = pos // SEL_BLOCK
    forced = (lane == 0) | (lane == cur) | (lane == cur - 1)
    imp = jnp.where(forced, imp + FORCE_BONUS, imp)
    imp = jnp.where(lane <= cur, imp, -jnp.inf)
    sel = jnp.zeros((qb, LANE), F32)
    for _ in range(TOP_N):
        mx = jnp.max(imp, axis=-1, keepdims=True)
        idx = jnp.min(jnp.where(imp == mx, lane_f, 2.0 * LANE), axis=-1, keepdims=True)
        hit = lane_f == idx
        sel = jnp.where(hit, 1.0, sel)
        imp = jnp.where(hit, -jnp.inf, imp)
    sel_b = sel.astype(BF16)

    init = (jnp.full((hg * qb, 1), NEG_BIG, F32), jnp.zeros((hg * qb, 1), F32), jnp.zeros((hg * qb, HEAD_DIM), F32))
    er = lax.broadcasted_iota(jnp.int32, (LANE, LANE), 0)
    ec = lax.broadcasted_iota(jnp.int32, (LANE, LANE), 1) // SEL_BLOCK

    def sel_body(kt, carry):
        off = pl.multiple_of(kt * LANE, LANE)
        k = ks_ref[0, 0, pl.ds(off, LANE), :]
        v = vs_ref[0, 0, pl.ds(off, LANE), :]
        expand = jnp.where(er == ec + kt * (LANE // SEL_BLOCK), 1.0, 0.0).astype(BF16)
        in_sel = _dot(sel_b, expand) > 0.5
        mask = rep(in_sel & (lane + kt * LANE <= pos))
        return _online_softmax_step(carry, _dot_nt(qs, k), mask, v)

    _, l_s, acc_s = lax.fori_loop(0, c + 1, sel_body, init)
    o_slc = acc_s / jnp.maximum(l_s, 1e-30)

    def win_body(kt, carry):
        off = pl.multiple_of(kt * LANE, LANE)
        k = kw_ref[0, 0, pl.ds(off, LANE), :]
        v = vw_ref[0, 0, pl.ds(off, LANE), :]
        kpos = lane + kt * LANE
        mask = rep((kpos <= pos) & (kpos > pos - WINDOW))
        return _online_softmax_step(carry, _dot_nt(qs, k), mask, v)

    _, l_w, acc_w = lax.fori_loop(jnp.maximum(c - WINDOW // LANE, 0), c + 1, win_body, init)
    o_win = acc_w / jnp.maximum(l_w, 1e-30)

    gates = jax.nn.sigmoid(gate_ref[0])
    outs = []
    for h in range(hg):
        rows = slice(h * qb, (h + 1) * qb)
        col = (g * hg + h) * 3
        pick = lambda j: jnp.sum(jnp.where(lane == col + j, gates, 0.0), axis=-1, keepdims=True)
        outs.append(pick(0) * o_cmp[rows] + pick(1) * o_slc[rows] + pick(2) * o_win[rows])
    o_ref[0] = jnp.concatenate(outs, axis=1)


def _nsa_attn(q, proj3, kc, vc, ks, vs, kw, vw):
    b, s, _ = q.shape
    gw = NSA_GROUP * HEAD_DIM
    n_pad = kc.shape[2]
    cmp_spec = pl.BlockSpec((1, 1, n_pad, HEAD_DIM), lambda i, g, c: (i, g, 0, 0))
    kv_spec = pl.BlockSpec((1, 1, s, HEAD_DIM), lambda i, g, c: (i, g, 0, 0))
    return pl.pallas_call(
        _nsa_attn_kernel,
        grid=(b, NSA_KV_HEADS, s // Q_BLOCK),
        in_specs=[
            pl.BlockSpec((1, Q_BLOCK, gw), lambda i, g, c: (i, c, g)),
            pl.BlockSpec((1, Q_BLOCK, LANE), lambda i, g, c: (i, c, COL_GATE // LANE)),
            cmp_spec, cmp_spec, kv_spec, kv_spec, kv_spec, kv_spec,
        ],
        out_specs=pl.BlockSpec((1, Q_BLOCK, gw), lambda i, g, c: (i, c, g)),
        out_shape=jax.ShapeDtypeStruct((b, s, NSA_WIDTH), F32),
        compiler_params=_params(("parallel", "parallel", "arbitrary")),
        name="nsa_attn",
    )(q, proj3, kc, vc, ks, vs, kw, vw)


def _mem_kv_kernel(mem_ref, g_ref, w_ref, kn_ref, k_ref, v_ref):
    x = mem_ref[0]
    ms = jnp.mean(x * x, axis=-1, keepdims=True)
    xn = (x * lax.rsqrt(ms + RMS_EPS) * g_ref[...]).astype(BF16)
    kv = _dot(xn, w_ref[...])
    for h in range(MEM_HEADS):
        sl = slice(h * MEM_HEAD_DIM, (h + 1) * MEM_HEAD_DIM)
        kh = kv[:, sl]
        kms = jnp.mean(kh * kh, axis=-1, keepdims=True)
        k_ref[0, :, sl] = (kh * lax.rsqrt(kms + RMS_EPS) * kn_ref[...]).astype(BF16)
    v_ref[0] = kv[:, MEM_WIDTH:].astype(BF16)


def _mem_kv(mem, g_mem, w_kv, k_norm):
    b, m, d = mem.shape
    spec = pl.BlockSpec((1, m, MEM_WIDTH), lambda i: (i, 0, 0))
    shape = jax.ShapeDtypeStruct((b, m, MEM_WIDTH), BF16)
    return pl.pallas_call(
        _mem_kv_kernel,
        grid=(b,),
        in_specs=[
            pl.BlockSpec((1, m, d), lambda i: (i, 0, 0)),
            pl.BlockSpec((1, d), lambda i: (0, 0)),
            pl.BlockSpec((d, 2 * MEM_WIDTH), lambda i: (0, 0)),
            pl.BlockSpec((1, MEM_HEAD_DIM), lambda i: (0, 0)),
        ],
        out_specs=[spec, spec],
        out_shape=[shape, shape],
        compiler_params=_params(("parallel",)),
        name="mem_kv",
    )(mem, g_mem, w_kv, k_norm)


def _mem_attn_kernel(q_ref, qn_ref, k_ref, v_ref, o_ref):
    q = q_ref[0]
    for h in range(MEM_HEADS):
        sl = slice(h * MEM_HEAD_DIM, (h + 1) * MEM_HEAD_DIM)
        qh = q[:, sl]
        ms = jnp.mean(qh * qh, axis=-1, keepdims=True)
        qh = (qh * lax.rsqrt(ms + RMS_EPS) * qn_ref[...]).astype(BF16)
        s = _dot_nt(qh, k_ref[0, :, sl]) * (MEM_HEAD_DIM ** -0.5)
        m = jnp.max(s, axis=-1, keepdims=True)
        e = jnp.exp(s - m)
        p = e / jnp.sum(e, axis=-1, keepdims=True)
        o_ref[0, :, sl] = _dot(p.astype(BF16), v_ref[0, :, sl])


def _mem_attn(proj3, q_norm, k, v, tq=512):
    b, s, _ = proj3.shape
    m = k.shape[1]
    kv_spec = pl.BlockSpec((1, m, MEM_WIDTH), lambda i, t: (i, 0, 0))
    return pl.pallas_call(
        _mem_attn_kernel,
        grid=(b, s // tq),
        in_specs=[
            pl.BlockSpec((1, tq, MEM_WIDTH), lambda i, t: (i, t, COL_QMEM // MEM_WIDTH)),
            pl.BlockSpec((1, MEM_HEAD_DIM), lambda i, t: (0, 0)),
            kv_spec, kv_spec,
        ],
        out_specs=pl.BlockSpec((1, tq, MEM_WIDTH), lambda i, t: (i, t, 0)),
        out_shape=jax.ShapeDtypeStruct((b, s, MEM_WIDTH), F32),
        compiler_params=_params(("parallel", "parallel")),
        name="mem_attn",
    )(proj3, q_norm, k, v)


def _merge_kernel(oa_ref, ob_ref, om_ref, l0_ref, l1_ref, l2_ref, x_ref, bm_ref, wb_ref, wo_ref,
                  gf_ref, rw_ref, rb_ref, x1_ref, h_ref, gate_ref):
    mixed = None
    for n, (o_ref, l_ref) in enumerate(((oa_ref, l0_ref), (ob_ref, l1_ref), (om_ref, l2_ref))):
        gate = jax.nn.sigmoid(l_ref[...] + bm_ref[:, n * D_MODEL:(n + 1) * D_MODEL])
        term = gate * _dot(o_ref[...].astype(BF16), wb_ref[n])
        mixed = term if mixed is None else mixed + term
    x1 = x_ref[...] + _dot(mixed.astype(BF16), wo_ref[...])
    x1_ref[...] = x1
    ms = jnp.mean(x1 * x1, axis=-1, keepdims=True)
    h = x1 * lax.rsqrt(ms + RMS_EPS) * gf_ref[...]
    h_ref[...] = h.astype(BF16)

    logits = _dot(h, rw_ref[...], HI) + rb_ref[...]
    lane_f = lax.broadcasted_iota(jnp.int32, logits.shape, 1).astype(F32)
    work = logits
    sel = jnp.zeros(logits.shape, F32)
    top = None
    for _ in range(TOP_K):
        mx = jnp.max(work, axis=-1, keepdims=True)
        top = mx if top is None else top
        idx = jnp.min(jnp.where(work == mx, lane_f, 2.0 * LANE), axis=-1, keepdims=True)
        hit = lane_f == idx
        sel = jnp.where(hit, 1.0, sel)
        work = jnp.where(hit, -jnp.inf, work)
    e = jnp.where(sel > 0.5, jnp.exp(logits - top), 0.0)
    gate_ref[...] = e / jnp.sum(e, axis=-1, keepdims=True)


def _merge(o_a, o_b, o_m, proj2, x2, b_merge, w_branch, w_out, g_ffn, router_w, router_b, tm=512):
    t, d = x2.shape
    row = lambda width: pl.BlockSpec((tm, width), lambda i: (i, 0))
    logit = lambda n: pl.BlockSpec((tm, d), lambda i: (i, COL_MERGE // d + n))
    full = lambda shape: pl.BlockSpec(shape, lambda i: (0,) * len(shape))
    return pl.pallas_call(
        _merge_kernel,
        grid=(t // tm,),
        in_specs=[
            row(BRANCH_WIDTH), row(BRANCH_WIDTH), row(BRANCH_WIDTH), logit(0), logit(1), logit(2), row(d),
            full((1, N_BRANCHES * d)), full((N_BRANCHES, BRANCH_WIDTH, d)), full((d, d)), full((1, d)),
            full((d, LANE)), full((1, LANE)),
        ],
        out_specs=[row(d), row(d), row(LANE)],
        out_shape=[jax.ShapeDtypeStruct((t, d), F32), jax.ShapeDtypeStruct((t, d), BF16),
                   jax.ShapeDtypeStruct((t, LANE), F32)],
        compiler_params=_params(("parallel",)),
        name="merge_router",
    )(o_a, o_b, o_m, proj2, proj2, proj2, x2, b_merge, w_branch, w_out, g_ffn, router_w, router_b)


def _moe_kernel(h_ref, gate_ref, x1_ref, w1g_ref, w1l_ref, b1g_ref, b1l_ref, w2_ref, b2_ref, o_ref, acc_ref):
    e = pl.program_id(1)
    f = pl.program_id(2)

    @pl.when((e == 0) & (f == 0))
    def _():
        acc_ref[...] = x1_ref[...]

    gate = gate_ref[...]
    lane = lax.broadcasted_iota(jnp.int32, gate.shape, 1)
    gcol = jnp.sum(jnp.where(lane == e, gate, 0.0), axis=-1, keepdims=True)

    h = h_ref[...]
    x_glu = jnp.minimum(_dot(h, w1g_ref[0]) + b1g_ref[0], SWIGLU_LIMIT)
    x_lin = jnp.clip(_dot(h, w1l_ref[0]) + b1l_ref[0], -SWIGLU_LIMIT, SWIGLU_LIMIT)
    act = x_glu * jax.nn.sigmoid(SWIGLU_ALPHA * x_glu) * (x_lin + 1.0)
    contrib = _dot(act.astype(BF16), w2_ref[0])
    bias = jnp.where(f == 0, 1.0, 0.0) * b2_ref[0]
    acc_ref[...] += gcol * (contrib + bias)

    @pl.when((e == pl.num_programs(1) - 1) & (f == pl.num_programs(2) - 1))
    def _():
        o_ref[...] = acc_ref[...]


def _moe(h, gate, x1, w1g, w1l, b1g, b1l, w2, b2, tm=1024, tf=512):
    t, d = h.shape
    n_e, _, ff = w1g.shape
    return pl.pallas_call(
        _moe_kernel,
        grid=(t // tm, n_e, ff // tf),
        in_specs=[
            pl.BlockSpec((tm, d), lambda i, e, f: (i, 0)),
            pl.BlockSpec((tm, LANE), lambda i, e, f: (i, 0)),
            pl.BlockSpec((tm, d), lambda i, e, f: (i, 0)),
            pl.BlockSpec((1, d, tf), lambda i, e, f: (e, 0, f)),
            pl.BlockSpec((1, d, tf), lambda i, e, f: (e, 0, f)),
            pl.BlockSpec((1, 1, tf), lambda i, e, f: (e, 0, f)),
            pl.BlockSpec((1, 1, tf), lambda i, e, f: (e, 0, f)),
            pl.BlockSpec((1, tf, d), lambda i, e, f: (e, f, 0)),
            pl.BlockSpec((1, 1, d), lambda i, e, f: (e, 0, 0)),
        ],
        out_specs=pl.BlockSpec((tm, d), lambda i, e, f: (i, 0)),
        out_shape=jax.ShapeDtypeStruct((t, d), F32),
        scratch_shapes=[pltpu.VMEM((tm, d), F32)],
        compiler_params=_params(("parallel", "arbitrary", "arbitrary")),
        name="moe_ffn",
    )(h, gate, x1, w1g, w1l, b1g, b1l, w2, b2)


def _pad_cols(w, width):
    return jnp.pad(w, ((0, 0), (0, width - w.shape[1])))


def _rope_tables(pos, reps):
    inv = ROPE_THETA ** (-jnp.arange(0, HEAD_DIM, 2, dtype=F32) / HEAD_DIM)
    ang = pos.astype(F32)[:, None] * inv[None, :]
    ang = jnp.concatenate([ang, ang], axis=-1)
    sign = jnp.concatenate([-jnp.ones((HEAD_DIM // 2,), F32), jnp.ones((HEAD_DIM // 2,), F32)])
    return jnp.tile(jnp.cos(ang), (1, reps)), jnp.tile(jnp.sin(ang) * sign, (1, reps))


def _layer(x, mem, g_mix, g_mem, w_in, b_merge, nsa_q_norm, nsa_k_norm, cmp_pe, cmp_w1, cmp_b1, cmp_w2,
           rwkv_shift_mix, rwkv_w0, rwkv_w_up, rwkv_a0, rwkv_a_up, rwkv_g_up, rwkv_k_k, rwkv_k_a, rwkv_r_k,
           rwkv_ln_w, rwkv_ln_b, mem_w_kv, mem_q_norm, mem_k_norm, w_branch, w_out, g_ffn,
           router_w, router_b, exp_w1, exp_b1, exp_w2, exp_b2, cos, sin, cos_c, sin_c):
    b, s, d = x.shape
    t = b * s
    x2 = x.reshape(t, d)

    o = 0
    parts = []
    for width in (NSA_WIDTH, 6 * NSA_KV_WIDTH, 3 * NSA_HEADS,
                  3 * RWKV_WIDTH + DECAY_RANK + AAA_RANK + GATE_RANK, MEM_WIDTH, N_BRANCHES * D_MODEL):
        parts.append(w_in[:, o:o + width])
        o += width
    w_q, w_kv, w_gate, w_rwkv, w_qm, w_merge = parts

    def rwkv_layout(m):
        r3 = m[:, :3 * RWKV_WIDTH]
        xw = m[:, 3 * RWKV_WIDTH:3 * RWKV_WIDTH + DECAY_RANK]
        xa = m[:, 3 * RWKV_WIDTH + DECAY_RANK:3 * RWKV_WIDTH + DECAY_RANK + AAA_RANK]
        xg = m[:, 3 * RWKV_WIDTH + DECAY_RANK + AAA_RANK:]
        return _pad_cols(jnp.concatenate([r3, _pad_cols(xw, LANE), _pad_cols(xa, LANE), xg], axis=1), RWKV_PAD)

    w_all = jnp.concatenate([rwkv_layout(w_rwkv), w_merge, w_q, w_qm, w_kv, w_gate], axis=1)
    w_all = _pad_cols(w_all, IN_PAD).astype(BF16)
    proj2 = _inproj(x2, g_mix.reshape(1, d), w_all)
    proj3 = proj2.reshape(b, s, IN_PAD)

    rowv = lambda a: a.reshape(1, -1)
    pad_rows = lambda m: jnp.pad(m, ((0, LANE - m.shape[0]), (0, 0)))
    o_b = _rwkv(proj3, rwkv_layout(rowv(rwkv_shift_mix)), rowv(rwkv_w0), pad_rows(rwkv_w_up), rowv(rwkv_a0),
                pad_rows(rwkv_a_up), rwkv_g_up, rowv(rwkv_k_k), rowv(rwkv_k_a), rowv(rwkv_r_k),
                rowv(rwkv_ln_w), rowv(rwkv_ln_b))

    q_gain = jnp.tile(nsa_q_norm.reshape(1, HEAD_DIM), (1, NSA_HEADS))
    k_gain = jnp.tile(nsa_k_norm, (1, NSA_KV_HEADS))
    qn, ks, vs, kw, vw = _nsa_prep(proj3, cos, sin, q_gain, k_gain)
    kc, vc = _nsa_cmp(proj3, cos_c, sin_c, k_gain, cmp_pe, cmp_w1.astype(BF16),
                      cmp_b1.reshape(2, 1, CMP_HIDDEN), cmp_w2.astype(BF16))
    o_a = _nsa_attn(qn, proj3, kc, vc, ks, vs, kw, vw)

    mk, mv = _mem_kv(mem, g_mem.reshape(1, d), mem_w_kv.astype(BF16), mem_k_norm.reshape(1, MEM_HEAD_DIM))
    o_m = _mem_attn(proj3, mem_q_norm.reshape(1, MEM_HEAD_DIM), mk, mv)

    rw = _pad_cols(router_w, LANE)
    rb = jnp.concatenate([router_b, jnp.full((LANE - N_EXPERTS,), NEG_BIG, F32)]).reshape(1, LANE)
    x1, h2, gate = _merge(o_a.reshape(t, NSA_WIDTH), o_b.reshape(t, RWKV_WIDTH), o_m.reshape(t, MEM_WIDTH),
                          proj2, x2, b_merge.reshape(1, -1), w_branch.astype(BF16), w_out.astype(BF16),
                          g_ffn.reshape(1, d), rw, rb)

    w1g = exp_w1[:, :, 0::2].astype(BF16)
    w1l = exp_w1[:, :, 1::2].astype(BF16)
    b1g = exp_b1[:, None, 0::2]
    b1l = exp_b1[:, None, 1::2]
    out = _moe(h2, gate, x1, w1g, w1l, b1g, b1l, exp_w2.astype(BF16), exp_b2[:, None, :])
    return out.reshape(b, s, d)


def kernel(x, mem, g_mix, g_mem, w_in, b_merge, nsa_q_norm, nsa_k_norm, cmp_pe, cmp_w1, cmp_b1, cmp_w2,
           rwkv_shift_mix, rwkv_w0, rwkv_w_up, rwkv_a0, rwkv_a_up, rwkv_g_up, rwkv_k_k, rwkv_k_a, rwkv_r_k,
           rwkv_ln_w, rwkv_ln_b, mem_w_kv, mem_q_norm, mem_k_norm, w_branch, w_out, g_ffn,
           router_w, router_b, exp_w1, exp_b1, exp_w2, exp_b2):
    s = x.shape[1]
    cos, sin = _rope_tables(jnp.arange(s), NSA_HEADS)
    n_cmp = (s - CMP_LEN) // CMP_STRIDE + 1
    cos_c, sin_c = _rope_tables(jnp.arange(n_cmp + 1) * CMP_STRIDE + CMP_LEN - 1, NSA_KV_HEADS)
    depth = g_mix.shape[0]
    for l in range(depth):
        x = _layer(x, mem, g_mix[l], g_mem[l], w_in[l], b_merge[l], nsa_q_norm[l], nsa_k_norm[l], cmp_pe[l],
                   cmp_w1[l], cmp_b1[l], cmp_w2[l], rwkv_shift_mix[l], rwkv_w0[l], rwkv_w_up[l], rwkv_a0[l],
                   rwkv_a_up[l], rwkv_g_up[l], rwkv_k_k[l], rwkv_k_a[l], rwkv_r_k[l], rwkv_ln_w[l], rwkv_ln_b[l],
                   mem_w_kv[l], mem_q_norm[l], mem_k_norm[l], w_branch[l], w_out[l], g_ffn[l], router_w[l],
                   router_b[l], exp_w1[l], exp_b1[l], exp_w2[l], exp_b2[l], cos, sin, cos_c, sin_c)
    return x
```

```python
import functools

import jax
import jax.numpy as jnp
from jax import lax
from jax.experimental import pallas as pl
from jax.experimental.pallas import tpu as pltpu

F32 = jnp.float32
BF16 = jnp.bfloat16
HI = lax.Precision.HIGHEST

D_MODEL = 1024
HEAD_DIM = 64
NSA_HEADS = 8
NSA_KV_HEADS = 2
NSA_GROUP = NSA_HEADS // NSA_KV_HEADS
NSA_WIDTH = NSA_HEADS * HEAD_DIM
NSA_KV_WIDTH = NSA_KV_HEADS * HEAD_DIM
CMP_LEN = 32
CMP_STRIDE = 16
CMP_HIDDEN = 128
SEL_BLOCK = 64
TOP_N = 8
WINDOW = 512
Q_BLOCK = 128
FORCE_BONUS = 1000.0
RWKV_HEADS = 8
RWKV_HEAD_DIM = 64
RWKV_WIDTH = RWKV_HEADS * RWKV_HEAD_DIM
DECAY_RANK = 64
AAA_RANK = 64
GATE_RANK = 128
GN_EPS = 64e-5
MEM_HEADS = 4
MEM_HEAD_DIM = 128
MEM_WIDTH = MEM_HEADS * MEM_HEAD_DIM
N_BRANCHES = 3
BRANCH_WIDTH = 512
N_EXPERTS = 32
TOP_K = 4
EXPERT_FF = 1024
SWIGLU_ALPHA = 1.702
SWIGLU_LIMIT = 7.0
ROPE_THETA = 10000.0
RMS_EPS = 1e-6

LANE = 128
NEG_BIG = -1e30

RWKV_PAD = 2048
COL_RWKV = 0
COL_MERGE = COL_RWKV + RWKV_PAD
COL_QNSA = COL_MERGE + N_BRANCHES * D_MODEL
COL_QMEM = COL_QNSA + NSA_WIDTH
COL_KV = COL_QMEM + MEM_WIDTH
COL_GATE = COL_KV + 6 * NSA_KV_WIDTH
IN_PAD = 7168
RW_R, RW_K, RW_V, RW_XW, RW_XA, RW_XG = 0, 512, 1024, 1536, 1664, 1792

RWKV_CHUNK = 64
VMEM_LIMIT = 56 * 1024 * 1024


def _dot(a, b, prec=None):
    return jnp.dot(a, b, preferred_element_type=F32, precision=prec)


def _dot_nt(a, b, prec=None):
    return lax.dot_general(a, b, (((1,), (1,)), ((), ())), preferred_element_type=F32, precision=prec)


def _dot_tn(a, b, prec=None):
    return lax.dot_general(a, b, (((0,), (0,)), ((), ())), preferred_element_type=F32, precision=prec)


def _split3(x):
    hi = x.astype(BF16)
    r1 = x - hi.astype(F32)
    mid = r1.astype(BF16)
    lo = (r1 - mid.astype(F32)).astype(BF16)
    return hi, mid, lo


def _dot01_left(m01, x):
    n = x.shape[1]
    out = _dot(m01, jnp.concatenate(_split3(x), axis=1))
    return out[:, :n] + out[:, n:2 * n] + out[:, 2 * n:]


def _dot01_right(x, m01):
    m = x.shape[0]
    out = _dot(jnp.concatenate(_split3(x), axis=0), m01)
    return out[:m] + out[m:2 * m] + out[2 * m:]


def _seg_matrix(width, seg):
    r = lax.broadcasted_iota(jnp.int32, (width, width), 0) // seg
    c = lax.broadcasted_iota(jnp.int32, (width, width), 1) // seg
    return (r == c).astype(F32)


def _params(sem):
    return pltpu.CompilerParams(dimension_semantics=sem, vmem_limit_bytes=VMEM_LIMIT)


def _inproj_kernel(x_ref, g_ref, w_ref, o_ref, hn_ref):
    @pl.when(pl.program_id(1) == 0)
    def _():
        x = x_ref[...]
        ms = jnp.mean(x * x, axis=-1, keepdims=True)
        hn_ref[...] = (x * lax.rsqrt(ms + RMS_EPS) * g_ref[...]).astype(BF16)

    o_ref[...] = _dot(hn_ref[...], w_ref[...])


def _inproj(x2, g, w, tm=1024, tn=512):
    t, d = x2.shape
    n = w.shape[1]
    return pl.pallas_call(
        _inproj_kernel,
        grid=(t // tm, n // tn),
        in_specs=[
            pl.BlockSpec((tm, d), lambda i, j: (i, 0)),
            pl.BlockSpec((1, d), lambda i, j: (0, 0)),
            pl.BlockSpec((d, tn), lambda i, j: (0, j)),
        ],
        out_specs=pl.BlockSpec((tm, tn), lambda i, j: (i, j)),
        out_shape=jax.ShapeDtypeStruct((t, n), F32),
        scratch_shapes=[pltpu.VMEM((tm, d), BF16)],
        compiler_params=_params(("parallel", "arbitrary")),
        name="inproj",
    )(x2, g, w)


def _rwkv_chunk_kernel(p_ref, pprev_ref, mix_ref, w0_ref, wup_ref, a0_ref, aup_ref, gup_ref, kk_ref, ka_ref,
                       rk_ref, rm_ref, y0_ref, bonus_ref, g_ref, gam_ref, m_ref, d0_ref):
    c = RWKV_CHUNK
    n = RWKV_HEAD_DIM
    rows = p_ref.shape[1]

    p = p_ref[0]
    row = lax.broadcasted_iota(jnp.int32, p.shape, 0)
    last_prev = jnp.where(pl.program_id(1) == 0, 0.0, 1.0) * pprev_ref[0, 7:8, :]
    prev = jnp.where(row == 0, last_prev, pltpu.roll(p, 1, axis=0))
    ps = p + (prev - p) * mix_ref[...]
    r = ps[:, RW_R:RW_R + RWKV_WIDTH]
    k = ps[:, RW_K:RW_K + RWKV_WIDTH]
    v = ps[:, RW_V:RW_V + RWKV_WIDTH]
    xw = ps[:, RW_XW:RW_XW + LANE]
    xa = ps[:, RW_XA:RW_XA + LANE]
    xg = ps[:, RW_XG:RW_XG + LANE]

    z = -(w0_ref[...] + _dot(jnp.tanh(xw), wup_ref[...], HI))
    softplus = jnp.maximum(z, 0.0) + jnp.log1p(jnp.exp(-jnp.abs(z)))
    w = -softplus - 0.5
    logw = -jnp.exp(w)
    a = jax.nn.sigmoid(a0_ref[...] + _dot(xa, aup_ref[...], HI))
    g_ref[0] = _dot(jax.nn.sigmoid(xg), gup_ref[...], HI)

    seg = _seg_matrix(RWKV_WIDTH, n).astype(BF16)
    kk = k * kk_ref[...]
    k = k * (1.0 + (a - 1.0) * ka_ref[...])
    sums = _dot01_right(jnp.concatenate([kk * kk, r * k * rk_ref[...]], axis=0), seg)
    kk = kk / jnp.maximum(jnp.sqrt(sums[:rows]), 1e-12)
    bonus_ref[0] = sums[rows:] * v
    kka = kk * a

    ti = lax.broadcasted_iota(jnp.int32, (2 * c, 2 * c), 0)
    tj = lax.broadcasted_iota(jnp.int32, (2 * c, 2 * c), 1)
    keep = (tj % c) < jnp.where(ti < c, ti, ti - c + 1)
    ci = lax.broadcasted_iota(jnp.int32, (c, c), 0)
    cj = lax.broadcasted_iota(jnp.int32, (c, c), 1)
    eye = (ci == cj).astype(F32)
    ltri = (cj <= ci).astype(BF16)
    zeros_cn = jnp.zeros((c, n), BF16)

    chains = []
    for j in range(rows // c):
        rs = slice(j * c, (j + 1) * c)
        lw = logw[rs]
        cum = _dot01_left(ltri, lw)
        tot = cum[c - 1:c, :]
        einv = jnp.exp(-cum)
        dec_end = jnp.exp(tot - cum)
        r_f = r[rs] * jnp.exp(cum)
        a_t = (-kk[rs] * jnp.exp(cum - lw)).astype(BF16)
        b_t = (kka[rs] * einv).astype(BF16)
        k_t = (k[rs] * einv).astype(BF16)
        r_t = r_f.astype(BF16)
        b_e = (kka[rs] * dec_end).astype(BF16)
        k_e = (k[rs] * dec_end).astype(BF16)
        v_b = v[rs].astype(BF16)
        gam_ref[0, j] = jnp.exp(tot)
        for h in range(RWKV_HEADS):
            sl = slice(h * n, (h + 1) * n)
            chains.append(dict(j=j, h=h, rs=rs, sl=sl, a=a_t[:, sl], r=r_t[:, sl], rf=r_f[:, sl], v=v_b[:, sl],
                               rhs=jnp.concatenate([b_t[:, sl], k_t[:, sl]], axis=0),
                               bke=jnp.concatenate([b_e[:, sl], k_e[:, sl]], axis=0)))

    for ch in chains:
        lhs = jnp.concatenate([ch["a"], ch["r"]], axis=0)
        ch["amat"] = jnp.where(keep, _dot_nt(lhs, ch["rhs"]), 0.0)
        ch["pw"] = ch["amat"][:c, :c]
        ch["tinv"] = eye + ch["pw"]
    for _ in range(5):
        for ch in chains:
            pw_b = ch["pw"].astype(BF16)
            ch["pw"] = _dot(pw_b, pw_b)
        for ch in chains:
            ch["tinv"] = ch["tinv"] + _dot(ch["tinv"].astype(BF16), ch["pw"].astype(BF16))
    for ch in chains:
        ch["akv"] = _dot(ch["amat"][:c, c:].astype(BF16), ch["v"])
    for ch in chains:
        wu = _dot(ch["tinv"].astype(BF16), jnp.concatenate([ch["a"], ch["akv"].astype(BF16)], axis=1)).astype(BF16)
        ch["x"] = jnp.concatenate([wu, jnp.concatenate([zeros_cn, ch["v"]], axis=1)], axis=0)
    for ch in chains:
        ry = _dot(ch["amat"][c:, :].astype(BF16), ch["x"])
        rm_ref[0, ch["rs"], ch["sl"]] = (ch["rf"] + ry[:, :n]).astype(BF16)
        y0_ref[0, ch["rs"], ch["sl"]] = ry[:, n:]
    for ch in chains:
        md = _dot_tn(ch["x"], ch["bke"])
        m_ref[0, ch["j"], ch["h"]] = md[:n].astype(BF16)
        d0_ref[0, ch["j"], ch["h"]] = md[n:]


def _rwkv_chunks(proj3, mix, w0, wup, a0, aup, gup, k_k, k_a, r_k, rows=128):
    b, s, _ = proj3.shape
    c = RWKV_CHUNK
    nc = s // c
    cps = rows // c
    vec = lambda width: pl.BlockSpec((1, width), lambda i, t: (0, 0))
    mat = lambda nrows: pl.BlockSpec((nrows, RWKV_WIDTH), lambda i, t: (0, 0))
    tok = pl.BlockSpec((1, rows, RWKV_WIDTH), lambda i, t: (i, t, 0))
    sq = pl.BlockSpec((1, cps, RWKV_HEADS, RWKV_HEAD_DIM, RWKV_HEAD_DIM), lambda i, t: (i, t, 0, 0, 0))
    tok_shape = lambda dt: jax.ShapeDtypeStruct((b, s, RWKV_WIDTH), dt)
    sq_shape = lambda dt: jax.ShapeDtypeStruct((b, nc, RWKV_HEADS, RWKV_HEAD_DIM, RWKV_HEAD_DIM), dt)
    return pl.pallas_call(
        _rwkv_chunk_kernel,
        grid=(b, s // rows),
        in_specs=[
            pl.BlockSpec((1, rows, RWKV_PAD), lambda i, t: (i, t, COL_RWKV // RWKV_PAD)),
            pl.BlockSpec((1, 8, RWKV_PAD), lambda i, t: (i, jnp.maximum(t * (rows // 8) - 1, 0), COL_RWKV // RWKV_PAD)),
            vec(RWKV_PAD), vec(RWKV_WIDTH), mat(LANE), vec(RWKV_WIDTH), mat(LANE), mat(LANE),
            vec(RWKV_WIDTH), vec(RWKV_WIDTH), vec(RWKV_WIDTH),
        ],
        out_specs=[tok, tok, tok, tok,
                   pl.BlockSpec((1, cps, 1, RWKV_WIDTH), lambda i, t: (i, t, 0, 0)), sq, sq],
        out_shape=[tok_shape(BF16), tok_shape(F32), tok_shape(F32), tok_shape(F32),
                   jax.ShapeDtypeStruct((b, nc, 1, RWKV_WIDTH), F32), sq_shape(BF16), sq_shape(F32)],
        compiler_params=_params(("parallel", "parallel")),
        name="rwkv7_chunks",
    )(proj3, proj3, mix, w0, wup, a0, aup, gup, k_k, k_a, r_k)


def _rwkv_scan_kernel(rm_ref, y0_ref, bonus_ref, g_ref, gam_ref, m_ref, d0_ref, lnw_ref, lnb_ref,
                      o_ref, state_ref, y_ref):
    c = RWKV_CHUNK
    n = RWKV_HEAD_DIM

    @pl.when(pl.program_id(1) == 0)
    def _():
        state_ref[...] = jnp.zeros_like(state_ref)

    for j in range(gam_ref.shape[1]):
        rs = slice(j * c, (j + 1) * c)
        gam = gam_ref[0, j]
        for h in range(RWKV_HEADS):
            sl = slice(h * n, (h + 1) * n)
            s = state_ref[h]
            s_b = s.astype(BF16)
            y_ref[rs, sl] = _dot_nt(rm_ref[0, rs, sl], s_b) + y0_ref[0, rs, sl]
            state_ref[h] = s * gam[:, sl] + _dot(s_b, m_ref[0, j, h]) + d0_ref[0, j, h]

    seg = _seg_matrix(RWKV_WIDTH, n).astype(BF16)
    y = y_ref[...]
    mu = _dot01_right(y, seg) * (1.0 / n)
    d = y - mu
    var = _dot01_right(d * d, seg) * (1.0 / n)
    yn = d * lax.rsqrt(var + GN_EPS) * lnw_ref[...] + lnb_ref[...]
    o_ref[0] = (yn + bonus_ref[0]) * g_ref[0]


def _rwkv_scan(rm, y0, bonus, g, gam, m, d0, ln_w, ln_b, rows=256):
    b, s, _ = rm.shape
    cps = rows // RWKV_CHUNK
    vec = pl.BlockSpec((1, RWKV_WIDTH), lambda i, t: (0, 0))
    tok = pl.BlockSpec((1, rows, RWKV_WIDTH), lambda i, t: (i, t, 0))
    sq = pl.BlockSpec((1, cps, RWKV_HEADS, RWKV_HEAD_DIM, RWKV_HEAD_DIM), lambda i, t: (i, t, 0, 0, 0))
    return pl.pallas_call(
        _rwkv_scan_kernel,
        grid=(b, s // rows),
        in_specs=[tok, tok, tok, tok, pl.BlockSpec((1, cps, 1, RWKV_WIDTH), lambda i, t: (i, t, 0, 0)), sq, sq,
                  vec, vec],
        out_specs=tok,
        out_shape=jax.ShapeDtypeStruct((b, s, RWKV_WIDTH), F32),
        scratch_shapes=[
            pltpu.VMEM((RWKV_HEADS, RWKV_HEAD_DIM, RWKV_HEAD_DIM), F32),
            pltpu.VMEM((rows, RWKV_WIDTH), F32),
        ],
        compiler_params=_params(("parallel", "arbitrary")),
        name="rwkv7_scan",
    )(rm, y0, bonus, g, gam, m, d0, ln_w, ln_b)


def _rwkv(proj3, mix, w0, wup, a0, aup, gup, k_k, k_a, r_k, ln_w, ln_b):
    rm, y0, bonus, g, gam, m, d0 = _rwkv_chunks(proj3, mix, w0, wup, a0, aup, gup, k_k, k_a, r_k)
    return _rwkv_scan(rm, y0, bonus, g, gam, m, d0, ln_w, ln_b)


def _rope(x, cos, sin_signed):
    w = x.shape[-1]
    first_half = (lax.broadcasted_iota(jnp.int32, x.shape, 1) % HEAD_DIM) < (HEAD_DIM // 2)
    rot = jnp.where(first_half, pltpu.roll(x, w - HEAD_DIM // 2, axis=1), pltpu.roll(x, HEAD_DIM // 2, axis=1))
    return x * cos + rot * sin_signed


def _head_rmsnorm(x, gain, seg):
    ms = _dot(x * x, seg, HI) * (1.0 / HEAD_DIM)
    return x * lax.rsqrt(ms + RMS_EPS) * gain


def _split_groups(x):
    return [x[:, g * HEAD_DIM:(g + 1) * HEAD_DIM] for g in range(NSA_KV_HEADS)]


def _nsa_prep_kernel(q_ref, ksl_ref, vsl_ref, kwn_ref, vwn_ref, cos_ref, sin_ref, qn_ref, kn_ref,
                     qo_ref, ks_ref, vs_ref, kw_ref, vw_ref):
    seg_q = _seg_matrix(NSA_WIDTH, HEAD_DIM)
    seg_k = _seg_matrix(NSA_KV_WIDTH, HEAD_DIM)
    cos_k = cos_ref[:, :NSA_KV_WIDTH]
    sin_k = sin_ref[:, :NSA_KV_WIDTH]

    q = _rope(_head_rmsnorm(q_ref[0], qn_ref[...], seg_q), cos_ref[...], sin_ref[...])
    qo_ref[0] = (q * (HEAD_DIM ** -0.5)).astype(BF16)

    ks = _rope(_head_rmsnorm(ksl_ref[0], kn_ref[1:2, :], seg_k), cos_k, sin_k)
    kw = _rope(_head_rmsnorm(kwn_ref[0], kn_ref[2:3, :], seg_k), cos_k, sin_k)
    for g, (a, b_, c_, d_) in enumerate(zip(_split_groups(ks), _split_groups(vsl_ref[0]),
                                            _split_groups(kw), _split_groups(vwn_ref[0]))):
        ks_ref[0, g] = a.astype(BF16)
        vs_ref[0, g] = b_.astype(BF16)
        kw_ref[0, g] = c_.astype(BF16)
        vw_ref[0, g] = d_.astype(BF16)


def _nsa_prep(proj3, cos, sin_signed, q_norm, k_norm, tq=512):
    b, s, _ = proj3.shape
    kvw = NSA_KV_WIDTH
    kv_spec = lambda j: pl.BlockSpec((1, tq, kvw), lambda i, t: (i, t, COL_KV // kvw + j))
    out_kv = pl.BlockSpec((1, NSA_KV_HEADS, tq, HEAD_DIM), lambda i, t: (i, 0, t, 0))
    kv_shape = jax.ShapeDtypeStruct((b, NSA_KV_HEADS, s, HEAD_DIM), BF16)
    return pl.pallas_call(
        _nsa_prep_kernel,
        grid=(b, s // tq),
        in_specs=[
            pl.BlockSpec((1, tq, NSA_WIDTH), lambda i, t: (i, t, COL_QNSA // NSA_WIDTH)),
            kv_spec(2), kv_spec(3), kv_spec(4), kv_spec(5),
            pl.BlockSpec((tq, NSA_WIDTH), lambda i, t: (t, 0)),
            pl.BlockSpec((tq, NSA_WIDTH), lambda i, t: (t, 0)),
            pl.BlockSpec((1, NSA_WIDTH), lambda i, t: (0, 0)),
            pl.BlockSpec((3, kvw), lambda i, t: (0, 0)),
        ],
        out_specs=[
            pl.BlockSpec((1, tq, NSA_WIDTH), lambda i, t: (i, t, 0)),
            out_kv, out_kv, out_kv, out_kv,
        ],
        out_shape=[jax.ShapeDtypeStruct((b, s, NSA_WIDTH), BF16), kv_shape, kv_shape, kv_shape, kv_shape],
        compiler_params=_params(("parallel", "parallel")),
        name="nsa_prep",
    )(proj3, proj3, proj3, proj3, proj3, cos, sin_signed, q_norm, k_norm)


def _gelu_tanh(x):
    return 0.5 * x * (1.0 + jnp.tanh(0.7978845608028654 * (x + 0.044715 * x * x * x)))


def _nsa_cmp_kernel(kc_in_ref, vc_in_ref, cos_ref, sin_ref, kn_ref, pe_ref, w1_ref, b1_ref, w2_ref,
                    kc_ref, vc_ref):
    n_cmp = (kc_in_ref.shape[1] - CMP_LEN) // CMP_STRIDE + 1
    n_pad = n_cmp + 1
    zero_row = jnp.zeros((1, NSA_KV_WIDTH), F32)
    outs = []
    for j, src in enumerate((kc_in_ref, vc_in_ref)):
        acc = jnp.zeros((NSA_KV_HEADS * n_pad, CMP_HIDDEN), F32)
        for l in range(CMP_LEN):
            x = src[0, pl.ds(l, n_cmp, stride=CMP_STRIDE), :]
            x = jnp.concatenate([x, zero_row], axis=0)
            xg = jnp.concatenate(_split_groups(x), axis=0) + pe_ref[j, l:l + 1, :]
            acc = acc + _dot(xg.astype(BF16), w1_ref[j, l * HEAD_DIM:(l + 1) * HEAD_DIM, :])
        hid = _gelu_tanh(acc + b1_ref[j])
        out = _dot(hid.astype(BF16), w2_ref[j])
        outs.append(jnp.concatenate([out[g * n_pad:(g + 1) * n_pad] for g in range(NSA_KV_HEADS)], axis=1))
    kc, vc = outs
    seg_k = _seg_matrix(NSA_KV_WIDTH, HEAD_DIM)
    kc = _rope(_head_rmsnorm(kc, kn_ref[0:1, :], seg_k), cos_ref[...], sin_ref[...])
    for g, (a, b_) in enumerate(zip(_split_groups(kc), _split_groups(vc))):
        kc_ref[0, g] = a.astype(BF16)
        vc_ref[0, g] = b_.astype(BF16)


def _nsa_cmp(proj3, cos_c, sin_c, k_norm, pe, w1, b1, w2):
    b, s, _ = proj3.shape
    kvw = NSA_KV_WIDTH
    n_pad = (s - CMP_LEN) // CMP_STRIDE + 2
    full = lambda shape: pl.BlockSpec(shape, lambda i: (0,) * len(shape))
    out_spec = pl.BlockSpec((1, NSA_KV_HEADS, n_pad, HEAD_DIM), lambda i: (i, 0, 0, 0))
    out_shape = jax.ShapeDtypeStruct((b, NSA_KV_HEADS, n_pad, HEAD_DIM), BF16)
    return pl.pallas_call(
        _nsa_cmp_kernel,
        grid=(b,),
        in_specs=[
            pl.BlockSpec((1, s, kvw), lambda i: (i, 0, COL_KV // kvw)),
            pl.BlockSpec((1, s, kvw), lambda i: (i, 0, COL_KV // kvw + 1)),
            full((n_pad, kvw)), full((n_pad, kvw)), full((3, kvw)),
            full((2, CMP_LEN, HEAD_DIM)), full((2, CMP_LEN * HEAD_DIM, CMP_HIDDEN)),
            full((2, 1, CMP_HIDDEN)), full((2, CMP_HIDDEN, HEAD_DIM)),
        ],
        out_specs=[out_spec, out_spec],
        out_shape=[out_shape, out_shape],
        compiler_params=_params(("parallel",)),
        name="nsa_compress",
    )(proj3, proj3, cos_c, sin_c, k_norm, pe, w1, b1, w2)


def _online_softmax_step(carry, s, mask, v):
    m_i, l_i, acc = carry
    s = jnp.where(mask, s, NEG_BIG)
    m_new = jnp.maximum(m_i, jnp.max(s, axis=-1, keepdims=True))
    alpha = jnp.exp(m_i - m_new)
    p = jnp.where(mask, jnp.exp(s - m_new), 0.0)
    l_new = alpha * l_i + jnp.sum(p, axis=-1, keepdims=True)
    acc = alpha * acc + _dot(p.astype(BF16), v)
    return m_new, l_new, acc


def _nsa_attn_kernel(q_ref, gate_ref, kc_ref, vc_ref, ks_ref, vs_ref, kw_ref, vw_ref, o_ref):
    qb = Q_BLOCK
    hg = NSA_GROUP
    g = pl.program_id(1)
    c = pl.program_id(2)
    q = q_ref[0]
    qs = jnp.concatenate([q[:, h * HEAD_DIM:(h + 1) * HEAD_DIM] for h in range(hg)], axis=0)
    rep = lambda cond: jnp.concatenate([jnp.where(cond, 1, 0)] * hg, axis=0) > 0

    row = lax.broadcasted_iota(jnp.int32, (qb, LANE), 0)
    lane = lax.broadcasted_iota(jnp.int32, (qb, LANE), 1)
    lane_f = lane.astype(F32)
    pos = c * qb + row

    n_cp = kc_ref.shape[2]
    s_c = _dot_nt(qs, kc_ref[0, 0])
    cend = lax.broadcasted_iota(jnp.int32, (qb, n_cp), 1) * CMP_STRIDE + CMP_LEN - 1
    cmask = rep(cend <= c * qb + lax.broadcasted_iota(jnp.int32, (qb, n_cp), 0))
    s_m = jnp.where(cmask, s_c, -jnp.inf)
    m = jnp.max(s_m, axis=-1, keepdims=True)
    m = jnp.where(m == -jnp.inf, 0.0, m)
    e = jnp.where(cmask, jnp.exp(s_m - m), 0.0)
    p_c = e / jnp.maximum(jnp.sum(e, axis=-1, keepdims=True), 1e-30)
    o_cmp = _dot(p_c.astype(BF16), vc_ref[0, 0])

    p_sum = p_c[0:qb]
    for h in range(1, hg):
        p_sum = p_sum + p_c[h * qb:(h + 1) * qb]
    ci = lax.broadcasted_iota(jnp.int32, (n_cp, LANE), 0) * CMP_STRIDE
    ni = lax.broadcasted_iota(jnp.int32, (n_cp, LANE), 1) * SEL_BLOCK
    cover = ((ci < ni + SEL_BLOCK) & (ci + CMP_LEN > ni)).astype(F32)
    imp = _dot(p_sum, cover, HI)
    cur = pos // SEL_BLOCK
    forced = (lane == 0) | (lane == cur) | (lane == cur - 1)
    imp = jnp.where(forced, imp + FORCE_BONUS, imp)
    imp = jnp.where(lane <= cur, imp, -jnp.inf)
    sel = jnp.zeros((qb, LANE), F32)
    for _ in range(TOP_N):
        mx = jnp.max(imp, axis=-1, keepdims=True)
        idx = jnp.min(jnp.where(imp == mx, lane_f, 2.0 * LANE), axis=-1, keepdims=True)
        hit = lane_f == idx
        sel = jnp.where(hit, 1.0, sel)
        imp = jnp.where(hit, -jnp.inf, imp)
    sel_b = sel.astype(BF16)

    init = (jnp.full((hg * qb, 1), NEG_BIG, F32), jnp.zeros((hg * qb, 1), F32), jnp.zeros((hg * qb, HEAD_DIM), F32))
    er = lax.broadcasted_iota(jnp.int32, (LANE, LANE), 0)
    ec = lax.broadcasted_iota(jnp.int32, (LANE, LANE), 1) // SEL_BLOCK

    def sel_body(kt, carry):
        off = pl.multiple_of(kt * LANE, LANE)
        k = ks_ref[0, 0, pl.ds(off, LANE), :]
        v = vs_ref[0, 0, pl.ds(off, LANE), :]
        expand = jnp.where(er == ec + kt * (LANE // SEL_BLOCK), 1.0, 0.0).astype(BF16)
        in_sel = _dot(sel_b, expand) > 0.5
        mask = rep(in_sel & (lane + kt * LANE <= pos))
        return _online_softmax_step(carry, _dot_nt(qs, k), mask, v)

    _, l_s, acc_s = lax.fori_loop(0, c + 1, sel_body, init)
    o_slc = acc_s / jnp.maximum(l_s, 1e-30)

    def win_body(kt, carry):
        off = pl.multiple_of(kt * LANE, LANE)
        k = kw_ref[0, 0, pl.ds(off, LANE), :]
        v = vw_ref[0, 0, pl.ds(off, LANE), :]
        kpos = lane + kt * LANE
        mask = rep((kpos <= pos) & (kpos > pos - WINDOW))
        return _online_softmax_step(carry, _dot_nt(qs, k), mask, v)

    _, l_w, acc_w = lax.fori_loop(jnp.maximum(c - WINDOW // LANE, 0), c + 1, win_body, init)
    o_win = acc_w / jnp.maximum(l_w, 1e-30)

    gates = jax.nn.sigmoid(gate_ref[0])
    outs = []
    for h in range(hg):
        rows = slice(h * qb, (h + 1) * qb)
        col = (g * hg + h) * 3
        pick = lambda j: jnp.sum(jnp.where(lane == col + j, gates, 0.0), axis=-1, keepdims=True)
        outs.append(pick(0) * o_cmp[rows] + pick(1) * o_slc[rows] + pick(2) * o_win[rows])
    o_ref[0] = jnp.concatenate(outs, axis=1)


def _nsa_attn(q, proj3, kc, vc, ks, vs, kw, vw):
    b, s, _ = q.shape
    gw = NSA_GROUP * HEAD_DIM
    n_pad = kc.shape[2]
    cmp_spec = pl.BlockSpec((1, 1, n_pad, HEAD_DIM), lambda i, g, c: (i, g, 0, 0))
    kv_spec = pl.BlockSpec((1, 1, s, HEAD_DIM), lambda i, g, c: (i, g, 0, 0))
    return pl.pallas_call(
        _nsa_attn_kernel,
        grid=(b, NSA_KV_HEADS, s // Q_BLOCK),
        in_specs=[
            pl.BlockSpec((1, Q_BLOCK, gw), lambda i, g, c: (i, c, g)),
            pl.BlockSpec((1, Q_BLOCK, LANE), lambda i, g, c: (i, c, COL_GATE // LANE)),
            cmp_spec, cmp_spec, kv_spec, kv_spec, kv_spec, kv_spec,
        ],
        out_specs=pl.BlockSpec((1, Q_BLOCK, gw), lambda i, g, c: (i, c, g)),
        out_shape=jax.ShapeDtypeStruct((b, s, NSA_WIDTH), F32),
        compiler_params=_params(("parallel", "parallel", "arbitrary")),
        name="nsa_attn",
    )(q, proj3, kc, vc, ks, vs, kw, vw)


def _mem_kv_kernel(mem_ref, g_ref, w_ref, kn_ref, k_ref, v_ref):
    x = mem_ref[0]
    ms = jnp.mean(x * x, axis=-1, keepdims=True)
    xn = (x * lax.rsqrt(ms + RMS_EPS) * g_ref[...]).astype(BF16)
    kv = _dot(xn, w_ref[...])
    for h in range(MEM_HEADS):
        sl = slice(h * MEM_HEAD_DIM, (h + 1) * MEM_HEAD_DIM)
        kh = kv[:, sl]
        kms = jnp.mean(kh * kh, axis=-1, keepdims=True)
        k_ref[0, :, sl] = (kh * lax.rsqrt(kms + RMS_EPS) * kn_ref[...]).astype(BF16)
    v_ref[0] = kv[:, MEM_WIDTH:].astype(BF16)


def _mem_kv(mem, g_mem, w_kv, k_norm):
    b, m, d = mem.shape
    spec = pl.BlockSpec((1, m, MEM_WIDTH), lambda i: (i, 0, 0))
    shape = jax.ShapeDtypeStruct((b, m, MEM_WIDTH), BF16)
    return pl.pallas_call(
        _mem_kv_kernel,
        grid=(b,),
        in_specs=[
            pl.BlockSpec((1, m, d), lambda i: (i, 0, 0)),
            pl.BlockSpec((1, d), lambda i: (0, 0)),
            pl.BlockSpec((d, 2 * MEM_WIDTH), lambda i: (0, 0)),
            pl.BlockSpec((1, MEM_HEAD_DIM), lambda i: (0, 0)),
        ],
        out_specs=[spec, spec],
        out_shape=[shape, shape],
        compiler_params=_params(("parallel",)),
        name="mem_kv",
    )(mem, g_mem, w_kv, k_norm)


def _mem_attn_kernel(q_ref, qn_ref, k_ref, v_ref, o_ref):
    q = q_ref[0]
    for h in range(MEM_HEADS):
        sl = slice(h * MEM_HEAD_DIM, (h + 1) * MEM_HEAD_DIM)
        qh = q[:, sl]
        ms = jnp.mean(qh * qh, axis=-1, keepdims=True)
        qh = (qh * lax.rsqrt(ms + RMS_EPS) * qn_ref[...]).astype(BF16)
        s = _dot_nt(qh, k_ref[0, :, sl]) * (MEM_HEAD_DIM ** -0.5)
        m = jnp.max(s, axis=-1, keepdims=True)
        e = jnp.exp(s - m)
        p = e / jnp.sum(e, axis=-1, keepdims=True)
        o_ref[0, :, sl] = _dot(p.astype(BF16), v_ref[0, :, sl])


def _mem_attn(proj3, q_norm, k, v, tq=512):
    b, s, _ = proj3.shape
    m = k.shape[1]
    kv_spec = pl.BlockSpec((1, m, MEM_WIDTH), lambda i, t: (i, 0, 0))
    return pl.pallas_call(
        _mem_attn_kernel,
        grid=(b, s // tq),
        in_specs=[
            pl.BlockSpec((1, tq, MEM_WIDTH), lambda i, t: (i, t, COL_QMEM // MEM_WIDTH)),
            pl.BlockSpec((1, MEM_HEAD_DIM), lambda i, t: (0, 0)),
            kv_spec, kv_spec,
        ],
        out_specs=pl.BlockSpec((1, tq, MEM_WIDTH), lambda i, t: (i, t, 0)),
        out_shape=jax.ShapeDtypeStruct((b, s, MEM_WIDTH), F32),
        compiler_params=_params(("parallel", "parallel")),
        name="mem_attn",
    )(proj3, q_norm, k, v)


def _merge_kernel(oa_ref, ob_ref, om_ref, l0_ref, l1_ref, l2_ref, x_ref, bm_ref, wb_ref, wo_ref,
                  gf_ref, rw_ref, rb_ref, x1_ref, h_ref, gate_ref):
    mixed = None
    for n, (o_ref, l_ref) in enumerate(((oa_ref, l0_ref), (ob_ref, l1_ref), (om_ref, l2_ref))):
        gate = jax.nn.sigmoid(l_ref[...] + bm_ref[:, n * D_MODEL:(n + 1) * D_MODEL])
        term = gate * _dot(o_ref[...].astype(BF16), wb_ref[n])
        mixed = term if mixed is None else mixed + term
    x1 = x_ref[...] + _dot(mixed.astype(BF16), wo_ref[...])
    x1_ref[...] = x1
    ms = jnp.mean(x1 * x1, axis=-1, keepdims=True)
    h = x1 * lax.rsqrt(ms + RMS_EPS) * gf_ref[...]
    h_ref[...] = h.astype(BF16)

    logits = _dot(h, rw_ref[...], HI) + rb_ref[...]
    lane_f = lax.broadcasted_iota(jnp.int32, logits.shape, 1).astype(F32)
    work = logits
    sel = jnp.zeros(logits.shape, F32)
    top = None
    for _ in range(TOP_K):
        mx = jnp.max(work, axis=-1, keepdims=True)
        top = mx if top is None else top
        idx = jnp.min(jnp.where(work == mx, lane_f, 2.0 * LANE), axis=-1, keepdims=True)
        hit = lane_f == idx
        sel = jnp.where(hit, 1.0, sel)
        work = jnp.where(hit, -jnp.inf, work)
    e = jnp.where(sel > 0.5, jnp.exp(logits - top), 0.0)
    gate_ref[...] = e / jnp.sum(e, axis=-1, keepdims=True)


def _merge(o_a, o_b, o_m, proj2, x2, b_merge, w_branch, w_out, g_ffn, router_w, router_b, tm=512):
    t, d = x2.shape
    row = lambda width: pl.BlockSpec((tm, width), lambda i: (i, 0))
    logit = lambda n: pl.BlockSpec((tm, d), lambda i: (i, COL_MERGE // d + n))
    full = lambda shape: pl.BlockSpec(shape, lambda i: (0,) * len(shape))
    return pl.pallas_call(
        _merge_kernel,
        grid=(t // tm,),
        in_specs=[
            row(BRANCH_WIDTH), row(BRANCH_WIDTH), row(BRANCH_WIDTH), logit(0), logit(1), logit(2), row(d),
            full((1, N_BRANCHES * d)), full((N_BRANCHES, BRANCH_WIDTH, d)), full((d, d)), full((1, d)),
            full((d, LANE)), full((1, LANE)),
        ],
        out_specs=[row(d), row(d), row(LANE)],
        out_shape=[jax.ShapeDtypeStruct((t, d), F32), jax.ShapeDtypeStruct((t, d), BF16),
                   jax.ShapeDtypeStruct((t, LANE), F32)],
        compiler_params=_params(("parallel",)),
        name="merge_router",
    )(o_a, o_b, o_m, proj2, proj2, proj2, x2, b_merge, w_branch, w_out, g_ffn, router_w, router_b)


def _deinterleave(x):
    rows, width = x.shape
    lane = lax.broadcasted_iota(jnp.int32, (rows, LANE), 1)
    half = LANE // 2
    low = lane < half
    idx = jnp.where(low, 2 * lane, 2 * (lane - half) + 1)
    evens, odds = [], []
    for j in range(0, width, 2 * LANE):
        a = jnp.take_along_axis(x[:, j:j + LANE], idx, axis=1)
        b = jnp.take_along_axis(x[:, j + LANE:j + 2 * LANE], idx, axis=1)
        evens.append(jnp.where(low, a, pltpu.roll(b, half, axis=1)))
        odds.append(jnp.where(low, pltpu.roll(a, half, axis=1), b))
    return jnp.concatenate(evens, axis=1), jnp.concatenate(odds, axis=1)


def _moe_kernel(h_ref, gate_ref, x1_ref, w1_ref, b1g_ref, b1l_ref, w2_ref, b2_ref, o_ref, acc_ref):
    e = pl.program_id(1)
    f = pl.program_id(2)

    @pl.when((e == 0) & (f == 0))
    def _():
        acc_ref[...] = x1_ref[...]

    gate = gate_ref[...]
    lane = lax.broadcasted_iota(jnp.int32, gate.shape, 1)
    gcol = jnp.sum(jnp.where(lane == e, gate, 0.0), axis=-1, keepdims=True)

    hid_g, hid_l = _deinterleave(_dot(h_ref[...], w1_ref[0]))
    x_glu = jnp.minimum(hid_g + b1g_ref[0], SWIGLU_LIMIT)
    x_lin = jnp.clip(hid_l + b1l_ref[0], -SWIGLU_LIMIT, SWIGLU_LIMIT)
    act = x_glu * jax.nn.sigmoid(SWIGLU_ALPHA * x_glu) * (x_lin + 1.0)
    contrib = _dot(act.astype(BF16), w2_ref[0])
    bias = jnp.where(f == 0, 1.0, 0.0) * b2_ref[0]
    acc_ref[...] += gcol * (contrib + bias)

    @pl.when((e == pl.num_programs(1) - 1) & (f == pl.num_programs(2) - 1))
    def _():
        o_ref[...] = acc_ref[...]


def _moe(h, gate, x1, w1, b1g, b1l, w2, b2, tm=1024, tf=512):
    t, d = h.shape
    n_e, ff, _ = w2.shape
    return pl.pallas_call(
        _moe_kernel,
        grid=(t // tm, n_e, ff // tf),
        in_specs=[
            pl.BlockSpec((tm, d), lambda i, e, f: (i, 0)),
            pl.BlockSpec((tm, LANE), lambda i, e, f: (i, 0)),
            pl.BlockSpec((tm, d), lambda i, e, f: (i, 0)),
            pl.BlockSpec((1, d, 2 * tf), lambda i, e, f: (e, 0, f)),
            pl.BlockSpec((1, 1, tf), lambda i, e, f: (e, 0, f)),
            pl.BlockSpec((1, 1, tf), lambda i, e, f: (e, 0, f)),
            pl.BlockSpec((1, tf, d), lambda i, e, f: (e, f, 0)),
            pl.BlockSpec((1, 1, d), lambda i, e, f: (e, 0, 0)),
        ],
        out_specs=pl.BlockSpec((tm, d), lambda i, e, f: (i, 0)),
        out_shape=jax.ShapeDtypeStruct((t, d), F32),
        scratch_shapes=[pltpu.VMEM((tm, d), F32)],
        compiler_params=_params(("parallel", "arbitrary", "arbitrary")),
        name="moe_ffn",
    )(h, gate, x1, w1, b1g, b1l, w2, b2)


def _pad_cols(w, width):
    return jnp.pad(w, ((0, 0), (0, width - w.shape[1])))


def _rope_tables(pos, reps):
    inv = ROPE_THETA ** (-jnp.arange(0, HEAD_DIM, 2, dtype=F32) / HEAD_DIM)
    ang = pos.astype(F32)[:, None] * inv[None, :]
    ang = jnp.concatenate([ang, ang], axis=-1)
    sign = jnp.concatenate([-jnp.ones((HEAD_DIM // 2,), F32), jnp.ones((HEAD_DIM // 2,), F32)])
    return jnp.tile(jnp.cos(ang), (1, reps)), jnp.tile(jnp.sin(ang) * sign, (1, reps))


def _layer(x, mem, g_mix, g_mem, w_in, b_merge, nsa_q_norm, nsa_k_norm, cmp_pe, cmp_w1, cmp_b1, cmp_w2,
           rwkv_shift_mix, rwkv_w0, rwkv_w_up, rwkv_a0, rwkv_a_up, rwkv_g_up, rwkv_k_k, rwkv_k_a, rwkv_r_k,
           rwkv_ln_w, rwkv_ln_b, mem_w_kv, mem_q_norm, mem_k_norm, w_branch, w_out, g_ffn,
           router_w, router_b, exp_w1, exp_b1, exp_w2, exp_b2, cos, sin, cos_c, sin_c):
    b, s, d = x.shape
    t = b * s
    x2 = x.reshape(t, d)

    o = 0
    parts = []
    for width in (NSA_WIDTH, 6 * NSA_KV_WIDTH, 3 * NSA_HEADS,
                  3 * RWKV_WIDTH + DECAY_RANK + AAA_RANK + GATE_RANK, MEM_WIDTH, N_BRANCHES * D_MODEL):
        parts.append(w_in[:, o:o + width])
        o += width
    w_q, w_kv, w_gate, w_rwkv, w_qm, w_merge = parts

    def rwkv_layout(m):
        r3 = m[:, :3 * RWKV_WIDTH]
        xw = m[:, 3 * RWKV_WIDTH:3 * RWKV_WIDTH + DECAY_RANK]
        xa = m[:, 3 * RWKV_WIDTH + DECAY_RANK:3 * RWKV_WIDTH + DECAY_RANK + AAA_RANK]
        xg = m[:, 3 * RWKV_WIDTH + DECAY_RANK + AAA_RANK:]
        return _pad_cols(jnp.concatenate([r3, _pad_cols(xw, LANE), _pad_cols(xa, LANE), xg], axis=1), RWKV_PAD)

    w_all = jnp.concatenate([rwkv_layout(w_rwkv), w_merge, w_q, w_qm, w_kv, w_gate], axis=1)
    w_all = _pad_cols(w_all, IN_PAD).astype(BF16)
    proj2 = _inproj(x2, g_mix.reshape(1, d), w_all)
    proj3 = proj2.reshape(b, s, IN_PAD)

    rowv = lambda a: a.reshape(1, -1)
    pad_rows = lambda m: jnp.pad(m, ((0, LANE - m.shape[0]), (0, 0)))
    o_b = _rwkv(proj3, rwkv_layout(rowv(rwkv_shift_mix)), rowv(rwkv_w0), pad_rows(rwkv_w_up), rowv(rwkv_a0),
                pad_rows(rwkv_a_up), rwkv_g_up, rowv(rwkv_k_k), rowv(rwkv_k_a), rowv(rwkv_r_k),
                rowv(rwkv_ln_w), rowv(rwkv_ln_b))

    q_gain = jnp.tile(nsa_q_norm.reshape(1, HEAD_DIM), (1, NSA_HEADS))
    k_gain = jnp.tile(nsa_k_norm, (1, NSA_KV_HEADS))
    qn, ks, vs, kw, vw = _nsa_prep(proj3, cos, sin, q_gain, k_gain)
    kc, vc = _nsa_cmp(proj3, cos_c, sin_c, k_gain, cmp_pe, cmp_w1.astype(BF16),
                      cmp_b1.reshape(2, 1, CMP_HIDDEN), cmp_w2.astype(BF16))
    o_a = _nsa_attn(qn, proj3, kc, vc, ks, vs, kw, vw)

    mk, mv = _mem_kv(mem, g_mem.reshape(1, d), mem_w_kv.astype(BF16), mem_k_norm.reshape(1, MEM_HEAD_DIM))
    o_m = _mem_attn(proj3, mem_q_norm.reshape(1, MEM_HEAD_DIM), mk, mv)

    rw = _pad_cols(router_w, LANE)
    rb = jnp.concatenate([router_b, jnp.full((LANE - N_EXPERTS,), NEG_BIG, F32)]).reshape(1, LANE)
    x1, h2, gate = _merge(o_a.reshape(t, NSA_WIDTH), o_b.reshape(t, RWKV_WIDTH), o_m.reshape(t, MEM_WIDTH),
                          proj2, x2, b_merge.reshape(1, -1), w_branch.astype(BF16), w_out.astype(BF16),
                          g_ffn.reshape(1, d), rw, rb)

    b1g = exp_b1[:, None, 0::2]
    b1l = exp_b1[:, None, 1::2]
    out = _moe(h2, gate, x1, exp_w1.astype(BF16), b1g, b1l, exp_w2.astype(BF16), exp_b2[:, None, :])
    return out.reshape(b, s, d)


def kernel(x, mem, g_mix, g_mem, w_in, b_merge, nsa_q_norm, nsa_k_norm, cmp_pe, cmp_w1, cmp_b1, cmp_w2,
           rwkv_shift_mix, rwkv_w0, rwkv_w_up, rwkv_a0, rwkv_a_up, rwkv_g_up, rwkv_k_k, rwkv_k_a, rwkv_r_k,
           rwkv_ln_w, rwkv_ln_b, mem_w_kv, mem_q_norm, mem_k_norm, w_branch, w_out, g_ffn,
           router_w, router_b, exp_w1, exp_b1, exp_w2, exp_b2):
    s = x.shape[1]
    cos, sin = _rope_tables(jnp.arange(s), NSA_HEADS)
    n_cmp = (s - CMP_LEN) // CMP_STRIDE + 1
    cos_c, sin_c = _rope_tables(jnp.arange(n_cmp + 1) * CMP_STRIDE + CMP_LEN - 1, NSA_KV_HEADS)
    depth = g_mix.shape[0]
    for l in range(depth):
        x = _layer(x, mem, g_mix[l], g_mem[l], w_in[l], b_merge[l], nsa_q_norm[l], nsa_k_norm[l], cmp_pe[l],
                   cmp_w1[l], cmp_b1[l], cmp_w2[l], rwkv_shift_mix[l], rwkv_w0[l], rwkv_w_up[l], rwkv_a0[l],
                   rwkv_a_up[l], rwkv_g_up[l], rwkv_k_k[l], rwkv_k_a[l], rwkv_r_k[l], rwkv_ln_w[l], rwkv_ln_b[l],
                   mem_w_kv[l], mem_q_norm[l], mem_k_norm[l], w_branch[l], w_out[l], g_ffn[l], router_w[l],
                   router_b[l], exp_w1[l], exp_b1[l], exp_w2[l], exp_b2[l], cos, sin, cos_c, sin_c)
    return x
```

```python
import functools

import jax
import jax.numpy as jnp
from jax import lax
from jax.experimental import pallas as pl
from jax.experimental.pallas import tpu as pltpu

F32 = jnp.float32
BF16 = jnp.bfloat16
HI = lax.Precision.HIGHEST

D_MODEL = 1024
HEAD_DIM = 64
NSA_HEADS = 8
NSA_KV_HEADS = 2
NSA_GROUP = NSA_HEADS // NSA_KV_HEADS
NSA_WIDTH = NSA_HEADS * HEAD_DIM
NSA_KV_WIDTH = NSA_KV_HEADS * HEAD_DIM
CMP_LEN = 32
CMP_STRIDE = 16
CMP_HIDDEN = 128
SEL_BLOCK = 64
TOP_N = 8
WINDOW = 512
Q_BLOCK = 128
FORCE_BONUS = 1000.0
RWKV_HEADS = 8
RWKV_HEAD_DIM = 64
RWKV_WIDTH = RWKV_HEADS * RWKV_HEAD_DIM
DECAY_RANK = 64
AAA_RANK = 64
GATE_RANK = 128
GN_EPS = 64e-5
MEM_HEADS = 4
MEM_HEAD_DIM = 128
MEM_WIDTH = MEM_HEADS * MEM_HEAD_DIM
N_BRANCHES = 3
BRANCH_WIDTH = 512
N_EXPERTS = 32
TOP_K = 4
EXPERT_FF = 1024
SWIGLU_ALPHA = 1.702
SWIGLU_LIMIT = 7.0
ROPE_THETA = 10000.0
RMS_EPS = 1e-6

LANE = 128
NEG_BIG = -1e30

RWKV_PAD = 2048
COL_RWKV = 0
COL_MERGE = COL_RWKV + RWKV_PAD
COL_QNSA = COL_MERGE + N_BRANCHES * D_MODEL
COL_QMEM = COL_QNSA + NSA_WIDTH
COL_KV = COL_QMEM + MEM_WIDTH
COL_GATE = COL_KV + 6 * NSA_KV_WIDTH
IN_PAD = 7168
RW_R, RW_K, RW_V, RW_XW, RW_XA, RW_XG = 0, 512, 1024, 1536, 1664, 1792

RWKV_CHUNK = 64
VMEM_LIMIT = 56 * 1024 * 1024


def _dot(a, b, prec=None):
    return jnp.dot(a, b, preferred_element_type=F32, precision=prec)


def _dot_nt(a, b, prec=None):
    return lax.dot_general(a, b, (((1,), (1,)), ((), ())), preferred_element_type=F32, precision=prec)


def _dot_tn(a, b, prec=None):
    return lax.dot_general(a, b, (((0,), (0,)), ((), ())), preferred_element_type=F32, precision=prec)


def _split3(x):
    hi = x.astype(BF16)
    r1 = x - hi.astype(F32)
    mid = r1.astype(BF16)
    lo = (r1 - mid.astype(F32)).astype(BF16)
    return hi, mid, lo


def _dot01_left(m01, x):
    n = x.shape[1]
    out = _dot(m01, jnp.concatenate(_split3(x), axis=1))
    return out[:, :n] + out[:, n:2 * n] + out[:, 2 * n:]


def _dot01_right(x, m01):
    m = x.shape[0]
    out = _dot(jnp.concatenate(_split3(x), axis=0), m01)
    return out[:m] + out[m:2 * m] + out[2 * m:]


def _seg_matrix(width, seg):
    r = lax.broadcasted_iota(jnp.int32, (width, width), 0) // seg
    c = lax.broadcasted_iota(jnp.int32, (width, width), 1) // seg
    return (r == c).astype(F32)


def _params(sem):
    return pltpu.CompilerParams(dimension_semantics=sem, vmem_limit_bytes=VMEM_LIMIT)


def _inproj_kernel(x_ref, g_ref, w_ref, o_ref, hn_ref):
    @pl.when(pl.program_id(1) == 0)
    def _():
        x = x_ref[...]
        ms = jnp.mean(x * x, axis=-1, keepdims=True)
        hn_ref[...] = (x * lax.rsqrt(ms + RMS_EPS) * g_ref[...]).astype(BF16)

    o_ref[...] = _dot(hn_ref[...], w_ref[...])


def _inproj(x2, g, w, tm=1024, tn=512):
    t, d = x2.shape
    n = w.shape[1]
    return pl.pallas_call(
        _inproj_kernel,
        grid=(t // tm, n // tn),
        in_specs=[
            pl.BlockSpec((tm, d), lambda i, j: (i, 0)),
            pl.BlockSpec((1, d), lambda i, j: (0, 0)),
            pl.BlockSpec((d, tn), lambda i, j: (0, j)),
        ],
        out_specs=pl.BlockSpec((tm, tn), lambda i, j: (i, j)),
        out_shape=jax.ShapeDtypeStruct((t, n), F32),
        scratch_shapes=[pltpu.VMEM((tm, d), BF16)],
        compiler_params=_params(("parallel", "arbitrary")),
        name="inproj",
    )(x2, g, w)


def _rwkv_chunk_kernel(p_ref, pprev_ref, mix_ref, w0_ref, wup_ref, a0_ref, aup_ref, gup_ref, kk_ref, ka_ref,
                       rk_ref, rm_ref, y0_ref, bonus_ref, g_ref, gam_ref, m_ref, d0_ref):
    c = RWKV_CHUNK
    n = RWKV_HEAD_DIM
    rows = p_ref.shape[1]

    p = p_ref[0]
    row = lax.broadcasted_iota(jnp.int32, p.shape, 0)
    last_prev = jnp.where(pl.program_id(1) == 0, 0.0, 1.0) * pprev_ref[0, 7:8, :]
    prev = jnp.where(row == 0, last_prev, pltpu.roll(p, 1, axis=0))
    ps = p + (prev - p) * mix_ref[...]
    r = ps[:, RW_R:RW_R + RWKV_WIDTH]
    k = ps[:, RW_K:RW_K + RWKV_WIDTH]
    v = ps[:, RW_V:RW_V + RWKV_WIDTH]
    xw = ps[:, RW_XW:RW_XW + LANE]
    xa = ps[:, RW_XA:RW_XA + LANE]
    xg = ps[:, RW_XG:RW_XG + LANE]

    z = -(w0_ref[...] + _dot(jnp.tanh(xw), wup_ref[...], HI))
    softplus = jnp.maximum(z, 0.0) + jnp.log1p(jnp.exp(-jnp.abs(z)))
    w = -softplus - 0.5
    logw = -jnp.exp(w)
    a = jax.nn.sigmoid(a0_ref[...] + _dot(xa, aup_ref[...], HI))
    g_ref[0] = _dot(jax.nn.sigmoid(xg), gup_ref[...], HI)

    seg = _seg_matrix(RWKV_WIDTH, n).astype(BF16)
    kk = k * kk_ref[...]
    k = k * (1.0 + (a - 1.0) * ka_ref[...])
    sums = _dot01_right(jnp.concatenate([kk * kk, r * k * rk_ref[...]], axis=0), seg)
    kk = kk / jnp.maximum(jnp.sqrt(sums[:rows]), 1e-12)
    bonus_ref[0] = sums[rows:] * v
    kka = kk * a

    ti = lax.broadcasted_iota(jnp.int32, (2 * c, 2 * c), 0)
    tj = lax.broadcasted_iota(jnp.int32, (2 * c, 2 * c), 1)
    keep = (tj % c) < jnp.where(ti < c, ti, ti - c + 1)
    ci = lax.broadcasted_iota(jnp.int32, (c, c), 0)
    cj = lax.broadcasted_iota(jnp.int32, (c, c), 1)
    eye = (ci == cj).astype(F32)
    ltri = (cj <= ci).astype(BF16)
    zeros_cn = jnp.zeros((c, n), BF16)

    chains = []
    for j in range(rows // c):
        rs = slice(j * c, (j + 1) * c)
        lw = logw[rs]
        cum = _dot01_left(ltri, lw)
        tot = cum[c - 1:c, :]
        einv = jnp.exp(-cum)
        dec_end = jnp.exp(tot - cum)
        r_f = r[rs] * jnp.exp(cum)
        a_t = (-kk[rs] * jnp.exp(cum - lw)).astype(BF16)
        b_t = (kka[rs] * einv).astype(BF16)
        k_t = (k[rs] * einv).astype(BF16)
        r_t = r_f.astype(BF16)
        b_e = (kka[rs] * dec_end).astype(BF16)
        k_e = (k[rs] * dec_end).astype(BF16)
        v_b = v[rs].astype(BF16)
        gam_ref[0, j] = jnp.exp(tot)
        for h in range(RWKV_HEADS):
            sl = slice(h * n, (h + 1) * n)
            chains.append(dict(j=j, h=h, rs=rs, sl=sl, a=a_t[:, sl], r=r_t[:, sl], rf=r_f[:, sl], v=v_b[:, sl],
                               rhs=jnp.concatenate([b_t[:, sl], k_t[:, sl]], axis=0),
                               bke=jnp.concatenate([b_e[:, sl], k_e[:, sl]], axis=0)))

    for ch in chains:
        lhs = jnp.concatenate([ch["a"], ch["r"]], axis=0)
        ch["amat"] = jnp.where(keep, _dot_nt(lhs, ch["rhs"]), 0.0)
        ch["pw"] = ch["amat"][:c, :c]
        ch["tinv"] = eye + ch["pw"]
    for _ in range(5):
        for ch in chains:
            pw_b = ch["pw"].astype(BF16)
            ch["pw"] = _dot(pw_b, pw_b)
        for ch in chains:
            ch["tinv"] = ch["tinv"] + _dot(ch["tinv"].astype(BF16), ch["pw"].astype(BF16))
    for ch in chains:
        ch["akv"] = _dot(ch["amat"][:c, c:].astype(BF16), ch["v"])
    for ch in chains:
        wu = _dot(ch["tinv"].astype(BF16), jnp.concatenate([ch["a"], ch["akv"].astype(BF16)], axis=1)).astype(BF16)
        ch["x"] = jnp.concatenate([wu, jnp.concatenate([zeros_cn, ch["v"]], axis=1)], axis=0)
    for ch in chains:
        ry = _dot(ch["amat"][c:, :].astype(BF16), ch["x"])
        rm_ref[0, ch["rs"], ch["sl"]] = (ch["rf"] + ry[:, :n]).astype(BF16)
        y0_ref[0, ch["rs"], ch["sl"]] = ry[:, n:]
    for ch in chains:
        md = _dot_tn(ch["x"], ch["bke"])
        m_ref[0, ch["j"], ch["h"]] = md[:n].astype(BF16)
        d0_ref[0, ch["j"], ch["h"]] = md[n:]


def _rwkv_chunks(proj3, mix, w0, wup, a0, aup, gup, k_k, k_a, r_k, rows=128):
    b, s, _ = proj3.shape
    c = RWKV_CHUNK
    nc = s // c
    cps = rows // c
    vec = lambda width: pl.BlockSpec((1, width), lambda i, t: (0, 0))
    mat = lambda nrows: pl.BlockSpec((nrows, RWKV_WIDTH), lambda i, t: (0, 0))
    tok = pl.BlockSpec((1, rows, RWKV_WIDTH), lambda i, t: (i, t, 0))
    sq = pl.BlockSpec((1, cps, RWKV_HEADS, RWKV_HEAD_DIM, RWKV_HEAD_DIM), lambda i, t: (i, t, 0, 0, 0))
    tok_shape = lambda dt: jax.ShapeDtypeStruct((b, s, RWKV_WIDTH), dt)
    sq_shape = lambda dt: jax.ShapeDtypeStruct((b, nc, RWKV_HEADS, RWKV_HEAD_DIM, RWKV_HEAD_DIM), dt)
    return pl.pallas_call(
        _rwkv_chunk_kernel,
        grid=(b, s // rows),
        in_specs=[
            pl.BlockSpec((1, rows, RWKV_PAD), lambda i, t: (i, t, COL_RWKV // RWKV_PAD)),
            pl.BlockSpec((1, 8, RWKV_PAD), lambda i, t: (i, jnp.maximum(t * (rows // 8) - 1, 0), COL_RWKV // RWKV_PAD)),
            vec(RWKV_PAD), vec(RWKV_WIDTH), mat(LANE), vec(RWKV_WIDTH), mat(LANE), mat(LANE),
            vec(RWKV_WIDTH), vec(RWKV_WIDTH), vec(RWKV_WIDTH),
        ],
        out_specs=[tok, tok, tok, tok,
                   pl.BlockSpec((1, cps, 1, RWKV_WIDTH), lambda i, t: (i, t, 0, 0)), sq, sq],
        out_shape=[tok_shape(BF16), tok_shape(F32), tok_shape(F32), tok_shape(F32),
                   jax.ShapeDtypeStruct((b, nc, 1, RWKV_WIDTH), F32), sq_shape(BF16), sq_shape(F32)],
        compiler_params=_params(("parallel", "parallel")),
        name="rwkv7_chunks",
    )(proj3, proj3, mix, w0, wup, a0, aup, gup, k_k, k_a, r_k)


def _rwkv_scan_kernel(rm_ref, y0_ref, bonus_ref, g_ref, gam_ref, m_ref, d0_ref, lnw_ref, lnb_ref,
                      o_ref, state_ref, y_ref):
    c = RWKV_CHUNK
    n = RWKV_HEAD_DIM

    @pl.when(pl.program_id(1) == 0)
    def _():
        state_ref[...] = jnp.zeros_like(state_ref)

    for j in range(gam_ref.shape[1]):
        rs = slice(j * c, (j + 1) * c)
        gam = gam_ref[0, j]
        for h in range(RWKV_HEADS):
            sl = slice(h * n, (h + 1) * n)
            s = state_ref[h]
            s_b = s.astype(BF16)
            y_ref[rs, sl] = _dot_nt(rm_ref[0, rs, sl], s_b) + y0_ref[0, rs, sl]
            state_ref[h] = s * gam[:, sl] + _dot(s_b, m_ref[0, j, h]) + d0_ref[0, j, h]

    seg = _seg_matrix(RWKV_WIDTH, n).astype(BF16)
    y = y_ref[...]
    mu = _dot01_right(y, seg) * (1.0 / n)
    d = y - mu
    var = _dot01_right(d * d, seg) * (1.0 / n)
    yn = d * lax.rsqrt(var + GN_EPS) * lnw_ref[...] + lnb_ref[...]
    o_ref[0] = (yn + bonus_ref[0]) * g_ref[0]


def _rwkv_scan(rm, y0, bonus, g, gam, m, d0, ln_w, ln_b, rows=256):
    b, s, _ = rm.shape
    cps = rows // RWKV_CHUNK
    vec = pl.BlockSpec((1, RWKV_WIDTH), lambda i, t: (0, 0))
    tok = pl.BlockSpec((1, rows, RWKV_WIDTH), lambda i, t: (i, t, 0))
    sq = pl.BlockSpec((1, cps, RWKV_HEADS, RWKV_HEAD_DIM, RWKV_HEAD_DIM), lambda i, t: (i, t, 0, 0, 0))
    return pl.pallas_call(
        _rwkv_scan_kernel,
        grid=(b, s // rows),
        in_specs=[tok, tok, tok, tok, pl.BlockSpec((1, cps, 1, RWKV_WIDTH), lambda i, t: (i, t, 0, 0)), sq, sq,
                  vec, vec],
        out_specs=tok,
        out_shape=jax.ShapeDtypeStruct((b, s, RWKV_WIDTH), F32),
        scratch_shapes=[
            pltpu.VMEM((RWKV_HEADS, RWKV_HEAD_DIM, RWKV_HEAD_DIM), F32),
            pltpu.VMEM((rows, RWKV_WIDTH), F32),
        ],
        compiler_params=_params(("parallel", "arbitrary")),
        name="rwkv7_scan",
    )(rm, y0, bonus, g, gam, m, d0, ln_w, ln_b)


def _rwkv(proj3, mix, w0, wup, a0, aup, gup, k_k, k_a, r_k, ln_w, ln_b):
    rm, y0, bonus, g, gam, m, d0 = _rwkv_chunks(proj3, mix, w0, wup, a0, aup, gup, k_k, k_a, r_k)
    return _rwkv_scan(rm, y0, bonus, g, gam, m, d0, ln_w, ln_b)


def _rope(x, cos, sin_signed):
    w = x.shape[-1]
    first_half = (lax.broadcasted_iota(jnp.int32, x.shape, 1) % HEAD_DIM) < (HEAD_DIM // 2)
    rot = jnp.where(first_half, pltpu.roll(x, w - HEAD_DIM // 2, axis=1), pltpu.roll(x, HEAD_DIM // 2, axis=1))
    return x * cos + rot * sin_signed


def _head_rmsnorm(x, gain, seg):
    ms = _dot(x * x, seg, HI) * (1.0 / HEAD_DIM)
    return x * lax.rsqrt(ms + RMS_EPS) * gain


def _split_groups(x):
    return [x[:, g * HEAD_DIM:(g + 1) * HEAD_DIM] for g in range(NSA_KV_HEADS)]


def _nsa_prep_kernel(q_ref, ksl_ref, vsl_ref, kwn_ref, vwn_ref, cos_ref, sin_ref, qn_ref, kn_ref,
                     qo_ref, ks_ref, vs_ref, kw_ref, vw_ref):
    seg_q = _seg_matrix(NSA_WIDTH, HEAD_DIM)
    seg_k = _seg_matrix(NSA_KV_WIDTH, HEAD_DIM)
    cos_k = cos_ref[:, :NSA_KV_WIDTH]
    sin_k = sin_ref[:, :NSA_KV_WIDTH]

    q = _rope(_head_rmsnorm(q_ref[0], qn_ref[...], seg_q), cos_ref[...], sin_ref[...])
    qo_ref[0] = (q * (HEAD_DIM ** -0.5)).astype(BF16)

    ks = _rope(_head_rmsnorm(ksl_ref[0], kn_ref[1:2, :], seg_k), cos_k, sin_k)
    kw = _rope(_head_rmsnorm(kwn_ref[0], kn_ref[2:3, :], seg_k), cos_k, sin_k)
    for g, (a, b_, c_, d_) in enumerate(zip(_split_groups(ks), _split_groups(vsl_ref[0]),
                                            _split_groups(kw), _split_groups(vwn_ref[0]))):
        ks_ref[0, g] = a.astype(BF16)
        vs_ref[0, g] = b_.astype(BF16)
        kw_ref[0, g] = c_.astype(BF16)
        vw_ref[0, g] = d_.astype(BF16)


def _nsa_prep(proj3, cos, sin_signed, q_norm, k_norm, tq=512):
    b, s, _ = proj3.shape
    kvw = NSA_KV_WIDTH
    kv_spec = lambda j: pl.BlockSpec((1, tq, kvw), lambda i, t: (i, t, COL_KV // kvw + j))
    out_kv = pl.BlockSpec((1, NSA_KV_HEADS, tq, HEAD_DIM), lambda i, t: (i, 0, t, 0))
    kv_shape = jax.ShapeDtypeStruct((b, NSA_KV_HEADS, s, HEAD_DIM), BF16)
    return pl.pallas_call(
        _nsa_prep_kernel,
        grid=(b, s // tq),
        in_specs=[
            pl.BlockSpec((1, tq, NSA_WIDTH), lambda i, t: (i, t, COL_QNSA // NSA_WIDTH)),
            kv_spec(2), kv_spec(3), kv_spec(4), kv_spec(5),
            pl.BlockSpec((tq, NSA_WIDTH), lambda i, t: (t, 0)),
            pl.BlockSpec((tq, NSA_WIDTH), lambda i, t: (t, 0)),
            pl.BlockSpec((1, NSA_WIDTH), lambda i, t: (0, 0)),
            pl.BlockSpec((3, kvw), lambda i, t: (0, 0)),
        ],
        out_specs=[
            pl.BlockSpec((1, tq, NSA_WIDTH), lambda i, t: (i, t, 0)),
            out_kv, out_kv, out_kv, out_kv,
        ],
        out_shape=[jax.ShapeDtypeStruct((b, s, NSA_WIDTH), BF16), kv_shape, kv_shape, kv_shape, kv_shape],
        compiler_params=_params(("parallel", "parallel")),
        name="nsa_prep",
    )(proj3, proj3, proj3, proj3, proj3, cos, sin_signed, q_norm, k_norm)


def _gelu_tanh(x):
    return 0.5 * x * (1.0 + jnp.tanh(0.7978845608028654 * (x + 0.044715 * x * x * x)))


def _nsa_cmp_kernel(kc_in_ref, vc_in_ref, cos_ref, sin_ref, kn_ref, pe_ref, w1_ref, b1_ref, w2_ref,
                    kc_ref, vc_ref):
    n_cmp = (kc_in_ref.shape[1] - CMP_LEN) // CMP_STRIDE + 1
    n_pad = n_cmp + 1
    zero_row = jnp.zeros((1, NSA_KV_WIDTH), F32)
    outs = []
    for j, src in enumerate((kc_in_ref, vc_in_ref)):
        acc = jnp.zeros((NSA_KV_HEADS * n_pad, CMP_HIDDEN), F32)
        for l in range(CMP_LEN):
            x = src[0, pl.ds(l, n_cmp, stride=CMP_STRIDE), :]
            x = jnp.concatenate([x, zero_row], axis=0)
            xg = jnp.concatenate(_split_groups(x), axis=0) + pe_ref[j, l:l + 1, :]
            acc = acc + _dot(xg.astype(BF16), w1_ref[j, l * HEAD_DIM:(l + 1) * HEAD_DIM, :])
        hid = _gelu_tanh(acc + b1_ref[j])
        out = _dot(hid.astype(BF16), w2_ref[j])
        outs.append(jnp.concatenate([out[g * n_pad:(g + 1) * n_pad] for g in range(NSA_KV_HEADS)], axis=1))
    kc, vc = outs
    seg_k = _seg_matrix(NSA_KV_WIDTH, HEAD_DIM)
    kc = _rope(_head_rmsnorm(kc, kn_ref[0:1, :], seg_k), cos_ref[...], sin_ref[...])
    for g, (a, b_) in enumerate(zip(_split_groups(kc), _split_groups(vc))):
        kc_ref[0, g] = a.astype(BF16)
        vc_ref[0, g] = b_.astype(BF16)


def _nsa_cmp(proj3, cos_c, sin_c, k_norm, pe, w1, b1, w2):
    b, s, _ = proj3.shape
    kvw = NSA_KV_WIDTH
    n_pad = (s - CMP_LEN) // CMP_STRIDE + 2
    full = lambda shape: pl.BlockSpec(shape, lambda i: (0,) * len(shape))
    out_spec = pl.BlockSpec((1, NSA_KV_HEADS, n_pad, HEAD_DIM), lambda i: (i, 0, 0, 0))
    out_shape = jax.ShapeDtypeStruct((b, NSA_KV_HEADS, n_pad, HEAD_DIM), BF16)
    return pl.pallas_call(
        _nsa_cmp_kernel,
        grid=(b,),
        in_specs=[
            pl.BlockSpec((1, s, kvw), lambda i: (i, 0, COL_KV // kvw)),
            pl.BlockSpec((1, s, kvw), lambda i: (i, 0, COL_KV // kvw + 1)),
            full((n_pad, kvw)), full((n_pad, kvw)), full((3, kvw)),
            full((2, CMP_LEN, HEAD_DIM)), full((2, CMP_LEN * HEAD_DIM, CMP_HIDDEN)),
            full((2, 1, CMP_HIDDEN)), full((2, CMP_HIDDEN, HEAD_DIM)),
        ],
        out_specs=[out_spec, out_spec],
        out_shape=[out_shape, out_shape],
        compiler_params=_params(("parallel",)),
        name="nsa_compress",
    )(proj3, proj3, cos_c, sin_c, k_norm, pe, w1, b1, w2)


def _online_softmax_step(carry, s, mask, v):
    m_i, l_i, acc = carry
    s = jnp.where(mask, s, NEG_BIG)
    m_new = jnp.maximum(m_i, jnp.max(s, axis=-1, keepdims=True))
    alpha = jnp.exp(m_i - m_new)
    p = jnp.where(mask, jnp.exp(s - m_new), 0.0)
    l_new = alpha * l_i + jnp.sum(p, axis=-1, keepdims=True)
    acc = alpha * acc + _dot(p.astype(BF16), v)
    return m_new, l_new, acc


def _nsa_attn_kernel(q_ref, gate_ref, kc_ref, vc_ref, ks_ref, vs_ref, kw_ref, vw_ref, o_ref):
    qb = Q_BLOCK
    hg = NSA_GROUP
    g = pl.program_id(1)
    c = pl.program_id(2)
    q = q_ref[0]
    qs = jnp.concatenate([q[:, h * HEAD_DIM:(h + 1) * HEAD_DIM] for h in range(hg)], axis=0)
    rep = lambda cond: jnp.concatenate([jnp.where(cond, 1, 0)] * hg, axis=0) > 0

    row = lax.broadcasted_iota(jnp.int32, (qb, LANE), 0)
    lane = lax.broadcasted_iota(jnp.int32, (qb, LANE), 1)
    lane_f = lane.astype(F32)
    pos = c * qb + row

    n_cp = kc_ref.shape[2]
    s_c = _dot_nt(qs, kc_ref[0, 0])
    cend = lax.broadcasted_iota(jnp.int32, (qb, n_cp), 1) * CMP_STRIDE + CMP_LEN - 1
    cmask = rep(cend <= c * qb + lax.broadcasted_iota(jnp.int32, (qb, n_cp), 0))
    s_m = jnp.where(cmask, s_c, -jnp.inf)
    m = jnp.max(s_m, axis=-1, keepdims=True)
    m = jnp.where(m == -jnp.inf, 0.0, m)
    e = jnp.where(cmask, jnp.exp(s_m - m), 0.0)
    p_c = e / jnp.maximum(jnp.sum(e, axis=-1, keepdims=True), 1e-30)
    o_cmp = _dot(p_c.astype(BF16), vc_ref[0, 0])

    p_sum = p_c[0:qb]
    for h in range(1, hg):
        p_sum = p_sum + p_c[h * qb:(h + 1) * qb]
    ci = lax.broadcasted_iota(jnp.int32, (n_cp, LANE), 0) * CMP_STRIDE
    ni = lax.broadcasted_iota(jnp.int32, (n_cp, LANE), 1) * SEL_BLOCK
    cover = ((ci < ni + SEL_BLOCK) & (ci + CMP_LEN > ni)).astype(F32)
    imp = _dot(p_sum, cover, HI)
    cur = pos // SEL_BLOCK
    forced = (lane == 0) | (lane == cur) | (lane == cur - 1)
    imp = jnp.where(forced, imp + FORCE_BONUS, imp)
    imp = jnp.where(lane <= cur, imp, -jnp.inf)
    sel = jnp.zeros((qb, LANE), F32)
    for _ in range(TOP_N):
        mx = jnp.max(imp, axis=-1, keepdims=True)
        idx = jnp.min(jnp.where(imp == mx, lane_f, 2.0 * LANE), axis=-1, keepdims=True)
        hit = lane_f == idx
        sel = jnp.where(hit, 1.0, sel)
        imp = jnp.where(hit, -jnp.inf, imp)
    sel_b = sel.astype(BF16)

    kt_w = 2 * LANE
    init = (jnp.full((hg * qb, 1), NEG_BIG, F32), jnp.zeros((hg * qb, 1), F32), jnp.zeros((hg * qb, HEAD_DIM), F32))
    er = lax.broadcasted_iota(jnp.int32, (LANE, kt_w), 0)
    ec = lax.broadcasted_iota(jnp.int32, (LANE, kt_w), 1) // SEL_BLOCK
    krel = lax.broadcasted_iota(jnp.int32, (qb, kt_w), 1)
    qpos = c * qb + lax.broadcasted_iota(jnp.int32, (qb, kt_w), 0)

    def sel_body(kt, carry):
        off = pl.multiple_of(kt * kt_w, kt_w)
        k = ks_ref[0, 0, pl.ds(off, kt_w), :]
        v = vs_ref[0, 0, pl.ds(off, kt_w), :]
        expand = jnp.where(er == ec + kt * (kt_w // SEL_BLOCK), 1.0, 0.0).astype(BF16)
        in_sel = _dot(sel_b, expand) > 0.5
        mask = rep(in_sel & (krel + kt * kt_w <= qpos))
        return _online_softmax_step(carry, _dot_nt(qs, k), mask, v)

    _, l_s, acc_s = lax.fori_loop(0, (c * qb + qb + kt_w - 1) // kt_w, sel_body, init)
    o_slc = acc_s / jnp.maximum(l_s, 1e-30)

    span = WINDOW + 2 * qb
    w0 = pl.multiple_of(jnp.clip(c * qb - WINDOW, 0, kw_ref.shape[2] - span), qb)
    kpos = w0 + lax.broadcasted_iota(jnp.int32, (qb, span), 1)
    wpos = c * qb + lax.broadcasted_iota(jnp.int32, (qb, span), 0)
    wmask = rep((kpos <= wpos) & (kpos > wpos - WINDOW))
    s_w = jnp.where(wmask, _dot_nt(qs, kw_ref[0, 0, pl.ds(w0, span), :]), NEG_BIG)
    e_w = jnp.where(wmask, jnp.exp(s_w - jnp.max(s_w, axis=-1, keepdims=True)), 0.0)
    o_win = _dot(e_w.astype(BF16), vw_ref[0, 0, pl.ds(w0, span), :]) / jnp.sum(e_w, axis=-1, keepdims=True)

    gates = jax.nn.sigmoid(gate_ref[0])
    outs = []
    for h in range(hg):
        rows = slice(h * qb, (h + 1) * qb)
        col = (g * hg + h) * 3
        pick = lambda j: jnp.sum(jnp.where(lane == col + j, gates, 0.0), axis=-1, keepdims=True)
        outs.append(pick(0) * o_cmp[rows] + pick(1) * o_slc[rows] + pick(2) * o_win[rows])
    o_ref[0] = jnp.concatenate(outs, axis=1)


def _nsa_attn(q, proj3, kc, vc, ks, vs, kw, vw):
    b, s, _ = q.shape
    gw = NSA_GROUP * HEAD_DIM
    n_pad = kc.shape[2]
    cmp_spec = pl.BlockSpec((1, 1, n_pad, HEAD_DIM), lambda i, g, c: (i, g, 0, 0))
    kv_spec = pl.BlockSpec((1, 1, s, HEAD_DIM), lambda i, g, c: (i, g, 0, 0))
    return pl.pallas_call(
        _nsa_attn_kernel,
        grid=(b, NSA_KV_HEADS, s // Q_BLOCK),
        in_specs=[
            pl.BlockSpec((1, Q_BLOCK, gw), lambda i, g, c: (i, c, g)),
            pl.BlockSpec((1, Q_BLOCK, LANE), lambda i, g, c: (i, c, COL_GATE // LANE)),
            cmp_spec, cmp_spec, kv_spec, kv_spec, kv_spec, kv_spec,
        ],
        out_specs=pl.BlockSpec((1, Q_BLOCK, gw), lambda i, g, c: (i, c, g)),
        out_shape=jax.ShapeDtypeStruct((b, s, NSA_WIDTH), F32),
        compiler_params=_params(("parallel", "parallel", "arbitrary")),
        name="nsa_attn",
    )(q, proj3, kc, vc, ks, vs, kw, vw)


def _mem_kv_kernel(mem_ref, g_ref, w_ref, kn_ref, k_ref, v_ref):
    x = mem_ref[0]
    ms = jnp.mean(x * x, axis=-1, keepdims=True)
    xn = (x * lax.rsqrt(ms + RMS_EPS) * g_ref[...]).astype(BF16)
    kv = _dot(xn, w_ref[...])
    for h in range(MEM_HEADS):
        sl = slice(h * MEM_HEAD_DIM, (h + 1) * MEM_HEAD_DIM)
        kh = kv[:, sl]
        kms = jnp.mean(kh * kh, axis=-1, keepdims=True)
        k_ref[0, :, sl] = (kh * lax.rsqrt(kms + RMS_EPS) * kn_ref[...]).astype(BF16)
    v_ref[0] = kv[:, MEM_WIDTH:].astype(BF16)


def _mem_kv(mem, g_mem, w_kv, k_norm):
    b, m, d = mem.shape
    spec = pl.BlockSpec((1, m, MEM_WIDTH), lambda i: (i, 0, 0))
    shape = jax.ShapeDtypeStruct((b, m, MEM_WIDTH), BF16)
    return pl.pallas_call(
        _mem_kv_kernel,
        grid=(b,),
        in_specs=[
            pl.BlockSpec((1, m, d), lambda i: (i, 0, 0)),
            pl.BlockSpec((1, d), lambda i: (0, 0)),
            pl.BlockSpec((d, 2 * MEM_WIDTH), lambda i: (0, 0)),
            pl.BlockSpec((1, MEM_HEAD_DIM), lambda i: (0, 0)),
        ],
        out_specs=[spec, spec],
        out_shape=[shape, shape],
        compiler_params=_params(("parallel",)),
        name="mem_kv",
    )(mem, g_mem, w_kv, k_norm)


def _mem_attn_kernel(q_ref, qn_ref, k_ref, v_ref, o_ref):
    q = q_ref[0]
    for h in range(MEM_HEADS):
        sl = slice(h * MEM_HEAD_DIM, (h + 1) * MEM_HEAD_DIM)
        qh = q[:, sl]
        ms = jnp.mean(qh * qh, axis=-1, keepdims=True)
        qh = (qh * lax.rsqrt(ms + RMS_EPS) * qn_ref[...]).astype(BF16)
        s = _dot_nt(qh, k_ref[0, :, sl]) * (MEM_HEAD_DIM ** -0.5)
        m = jnp.max(s, axis=-1, keepdims=True)
        e = jnp.exp(s - m)
        p = e / jnp.sum(e, axis=-1, keepdims=True)
        o_ref[0, :, sl] = _dot(p.astype(BF16), v_ref[0, :, sl])


def _mem_attn(proj3, q_norm, k, v, tq=512):
    b, s, _ = proj3.shape
    m = k.shape[1]
    kv_spec = pl.BlockSpec((1, m, MEM_WIDTH), lambda i, t: (i, 0, 0))
    return pl.pallas_call(
        _mem_attn_kernel,
        grid=(b, s // tq),
        in_specs=[
            pl.BlockSpec((1, tq, MEM_WIDTH), lambda i, t: (i, t, COL_QMEM // MEM_WIDTH)),
            pl.BlockSpec((1, MEM_HEAD_DIM), lambda i, t: (0, 0)),
            kv_spec, kv_spec,
        ],
        out_specs=pl.BlockSpec((1, tq, MEM_WIDTH), lambda i, t: (i, t, 0)),
        out_shape=jax.ShapeDtypeStruct((b, s, MEM_WIDTH), F32),
        compiler_params=_params(("parallel", "parallel")),
        name="mem_attn",
    )(proj3, q_norm, k, v)


def _merge_kernel(oa_ref, ob_ref, om_ref, l0_ref, l1_ref, l2_ref, x_ref, bm_ref, wb_ref, wo_ref,
                  gf_ref, rw_ref, rb_ref, x1_ref, h_ref, route_ref):
    mixed = None
    for n, (o_ref, l_ref) in enumerate(((oa_ref, l0_ref), (ob_ref, l1_ref), (om_ref, l2_ref))):
        gate = jax.nn.sigmoid(l_ref[...] + bm_ref[:, n * D_MODEL:(n + 1) * D_MODEL])
        term = gate * _dot(o_ref[...].astype(BF16), wb_ref[n])
        mixed = term if mixed is None else mixed + term
    x1 = x_ref[...] + _dot(mixed.astype(BF16), wo_ref[...])
    x1_ref[...] = x1
    ms = jnp.mean(x1 * x1, axis=-1, keepdims=True)
    h = x1 * lax.rsqrt(ms + RMS_EPS) * gf_ref[...]
    h_ref[...] = h

    logits = _dot(h, rw_ref[...], HI) + rb_ref[...]
    lane_f = lax.broadcasted_iota(jnp.int32, logits.shape, 1).astype(F32)
    work = logits
    picks = []
    for _ in range(TOP_K):
        mx = jnp.max(work, axis=-1, keepdims=True)
        idx = jnp.min(jnp.where(work == mx, lane_f, 2.0 * LANE), axis=-1, keepdims=True)
        picks.append((idx, mx))
        work = jnp.where(lane_f == idx, -jnp.inf, work)
    exps = [jnp.exp(mx - picks[0][1]) for _, mx in picks]
    denom = functools.reduce(lambda a, b: a + b, exps)
    route = jnp.zeros(logits.shape, F32)
    for kk, ((idx, _), ex) in enumerate(zip(picks, exps)):
        route = jnp.where(lane_f == kk, idx, route)
        route = jnp.where(lane_f == TOP_K + kk, ex / denom, route)
    route_ref[...] = route


def _merge(o_a, o_b, o_m, proj2, x2, b_merge, w_branch, w_out, g_ffn, router_w, router_b, tm=512):
    t, d = x2.shape
    row = lambda width: pl.BlockSpec((tm, width), lambda i: (i, 0))
    logit = lambda n: pl.BlockSpec((tm, d), lambda i: (i, COL_MERGE // d + n))
    full = lambda shape: pl.BlockSpec(shape, lambda i: (0,) * len(shape))
    return pl.pallas_call(
        _merge_kernel,
        grid=(t // tm,),
        in_specs=[
            row(BRANCH_WIDTH), row(BRANCH_WIDTH), row(BRANCH_WIDTH), logit(0), logit(1), logit(2), row(d),
            full((1, N_BRANCHES * d)), full((N_BRANCHES, BRANCH_WIDTH, d)), full((d, d)), full((1, d)),
            full((d, LANE)), full((1, LANE)),
        ],
        out_specs=[row(d), row(d), row(LANE)],
        out_shape=[jax.ShapeDtypeStruct((t, d), F32), jax.ShapeDtypeStruct((t, d), F32),
                   jax.ShapeDtypeStruct((t, LANE), F32)],
        compiler_params=_params(("parallel",)),
        name="merge_router",
    )(o_a, o_b, o_m, proj2, proj2, proj2, x2, b_merge, w_branch, w_out, g_ffn, router_w, router_b)


def _deinterleave(x):
    rows, width = x.shape
    lane = lax.broadcasted_iota(jnp.int32, (rows, LANE), 1)
    half = LANE // 2
    low = lane < half
    idx = jnp.where(low, 2 * lane, 2 * (lane - half) + 1)
    evens, odds = [], []
    for j in range(0, width, 2 * LANE):
        a = jnp.take_along_axis(x[:, j:j + LANE], idx, axis=1)
        b = jnp.take_along_axis(x[:, j + LANE:j + 2 * LANE], idx, axis=1)
        evens.append(jnp.where(low, a, pltpu.roll(b, half, axis=1)))
        odds.append(jnp.where(low, pltpu.roll(a, half, axis=1), b))
    return jnp.concatenate(evens, axis=1), jnp.concatenate(odds, axis=1)


MOE_TILE = 256


def _route_kernel(route_ref, dest_ref, counts_ref, carry_ref, start_ref):
    phase = pl.program_id(0)
    i = pl.program_id(1)
    tr = route_ref.shape[0]
    route = route_ref[...]
    lane = lax.broadcasted_iota(jnp.int32, route.shape, 1)
    lane_f = lane.astype(F32)
    hits = [lane_f == route[:, kk:kk + 1] for kk in range(TOP_K)]
    sel = jnp.zeros(route.shape, F32)
    for hit in hits:
        sel = jnp.where(hit, 1.0, sel)

    @pl.when((phase == 0) & (i == 0))
    def _():
        carry_ref[...] = jnp.zeros_like(carry_ref)

    @pl.when((phase == 1) & (i == 0))
    def _():
        counts = carry_ref[...]
        counts_ref[...] = counts
        padded = jnp.ceil(counts * (1.0 / MOE_TILE)) * MOE_TILE
        ui = lax.broadcasted_iota(jnp.int32, (LANE, LANE), 0)
        uj = lax.broadcasted_iota(jnp.int32, (LANE, LANE), 1)
        start_ref[...] = _dot01_right(jnp.broadcast_to(padded, (8, LANE)), (ui < uj).astype(BF16))[0:1]
        carry_ref[...] = jnp.zeros_like(carry_ref)

    @pl.when(phase == 1)
    def _():
        ri = lax.broadcasted_iota(jnp.int32, (tr, tr), 0)
        rj = lax.broadcasted_iota(jnp.int32, (tr, tr), 1)
        rank = carry_ref[...] + _dot((rj < ri).astype(BF16), sel.astype(BF16))
        row = start_ref[...] + rank
        dest = jnp.zeros(route.shape, F32)
        for kk, hit in enumerate(hits):
            dest = jnp.where(lane == kk, jnp.sum(jnp.where(hit, row, 0.0), axis=-1, keepdims=True), dest)
        dest_ref[...] = dest.astype(jnp.int32)

    carry_ref[...] += jnp.sum(sel, axis=0, keepdims=True)


def _route(route, tr=512):
    t = route.shape[0]
    return pl.pallas_call(
        _route_kernel,
        grid=(2, t // tr),
        in_specs=[pl.BlockSpec((tr, LANE), lambda p, i: (i, 0))],
        out_specs=[pl.BlockSpec((tr, LANE), lambda p, i: (i * p, 0)), pl.BlockSpec((1, LANE), lambda p, i: (0, 0))],
        out_shape=[jax.ShapeDtypeStruct((t, LANE), jnp.int32), jax.ShapeDtypeStruct((1, LANE), F32)],
        scratch_shapes=[pltpu.VMEM((1, LANE), F32), pltpu.VMEM((1, LANE), F32)],
        compiler_params=_params(("arbitrary", "arbitrary")),
        name="moe_route",
    )(route)


def _row_copy(src, src_row, dst, dst_row, sem):
    return pltpu.make_async_copy(src.at[pl.ds(src_row, 1)], dst.at[pl.ds(dst_row, 1)], sem)


def _dispatch_kernel(dest_ref, h_ref, xs_in_ref, xs_ref, sem):
    del xs_in_ref
    td = h_ref.shape[0]

    def issue(r, carry):
        for kk in range(TOP_K):
            _row_copy(h_ref, r, xs_ref, dest_ref[r * TOP_K + kk], sem).start()
        return carry

    lax.fori_loop(0, td, issue, 0, unroll=8)

    def drain(r, carry):
        for kk in range(TOP_K):
            _row_copy(h_ref, 0, xs_ref, 0, sem).wait()
        return carry

    lax.fori_loop(0, td, drain, 0, unroll=8)


def _dispatch(dest_flat, h, xs_zero, td=256):
    t, d = h.shape
    return pl.pallas_call(
        _dispatch_kernel,
        grid=(t // td,),
        in_specs=[
            pl.BlockSpec((td * TOP_K,), lambda i: (i,), memory_space=pltpu.SMEM),
            pl.BlockSpec((td, d), lambda i: (i, 0)),
            pl.BlockSpec(memory_space=pl.ANY),
        ],
        out_specs=pl.BlockSpec(memory_space=pl.ANY),
        out_shape=jax.ShapeDtypeStruct(xs_zero.shape, xs_zero.dtype),
        scratch_shapes=[pltpu.SemaphoreType.DMA],
        input_output_aliases={2: 0},
        compiler_params=_params(("arbitrary",)),
        name="moe_dispatch",
    )(dest_flat, h, xs_zero)


def _experts_kernel(te_ref, nu_ref, xs_ref, w1_ref, b1g_ref, b1l_ref, w2_ref, b2_ref, y_ref, w1b_ref, w2b_ref):
    j = pl.program_id(0)

    @pl.when(j < nu_ref[0])
    def _():
        @pl.when((j == 0) | (te_ref[j] != te_ref[jnp.maximum(j - 1, 0)]))
        def _():
            w1b_ref[...] = w1_ref[0].astype(BF16)
            w2b_ref[...] = w2_ref[0].astype(BF16)

        hid_g, hid_l = _deinterleave(_dot(xs_ref[...].astype(BF16), w1b_ref[...]))
        x_glu = jnp.minimum(hid_g + b1g_ref[0], SWIGLU_LIMIT)
        x_lin = jnp.clip(hid_l + b1l_ref[0], -SWIGLU_LIMIT, SWIGLU_LIMIT)
        act = x_glu * jax.nn.sigmoid(SWIGLU_ALPHA * x_glu) * (x_lin + 1.0)
        y_ref[...] = _dot(act.astype(BF16), w2b_ref[...]) + b2_ref[0]

    @pl.when(j >= nu_ref[0])
    def _():
        y_ref[...] = jnp.zeros_like(y_ref)


def _experts(tile_expert, n_used, xs, w1, b1g, b1l, w2, b2):
    rows, d = xs.shape
    n_e, ff, _ = w2.shape
    tg = MOE_TILE
    used = lambda j, te, nu: jnp.minimum(j, nu[0] - 1)
    exp_of = lambda j, te, nu: te[used(j, te, nu)]
    return pl.pallas_call(
        _experts_kernel,
        grid_spec=pltpu.PrefetchScalarGridSpec(
            num_scalar_prefetch=2,
            grid=(rows // tg,),
            in_specs=[
                pl.BlockSpec((tg, d), lambda j, te, nu: (used(j, te, nu), 0)),
                pl.BlockSpec((1, d, 2 * ff), lambda j, te, nu: (exp_of(j, te, nu), 0, 0)),
                pl.BlockSpec((1, 1, ff), lambda j, te, nu: (exp_of(j, te, nu), 0, 0)),
                pl.BlockSpec((1, 1, ff), lambda j, te, nu: (exp_of(j, te, nu), 0, 0)),
                pl.BlockSpec((1, ff, d), lambda j, te, nu: (exp_of(j, te, nu), 0, 0)),
                pl.BlockSpec((1, 1, d), lambda j, te, nu: (exp_of(j, te, nu), 0, 0)),
            ],
            out_specs=pl.BlockSpec((tg, d), lambda j, te, nu: (j, 0)),
            scratch_shapes=[pltpu.VMEM((d, 2 * ff), BF16), pltpu.VMEM((ff, d), BF16)],
        ),
        out_shape=jax.ShapeDtypeStruct((rows, d), F32),
        compiler_params=_params(("arbitrary",)),
        name="moe_experts",
    )(tile_expert, n_used, xs, w1, b1g, b1l, w2, b2)


def _combine_kernel(dest_ref, route_ref, x1_ref, y_ref, o_ref, buf_ref, sem):
    tc = x1_ref.shape[0]

    def issue(r, carry):
        for kk in range(TOP_K):
            _row_copy(y_ref, dest_ref[r * TOP_K + kk], buf_ref.at[kk], r, sem).start()
        return carry

    lax.fori_loop(0, tc, issue, 0, unroll=8)

    def drain(r, carry):
        for kk in range(TOP_K):
            _row_copy(y_ref, 0, buf_ref.at[kk], 0, sem).wait()
        return carry

    lax.fori_loop(0, tc, drain, 0, unroll=8)

    route = route_ref[...]
    acc = x1_ref[...]
    for kk in range(TOP_K):
        acc = acc + route[:, TOP_K + kk:TOP_K + kk + 1] * buf_ref[kk]
    o_ref[...] = acc


def _combine(dest_flat, route, x1, y, tc=256):
    t, d = x1.shape
    return pl.pallas_call(
        _combine_kernel,
        grid=(t // tc,),
        in_specs=[
            pl.BlockSpec((tc * TOP_K,), lambda i: (i,), memory_space=pltpu.SMEM),
            pl.BlockSpec((tc, LANE), lambda i: (i, 0)),
            pl.BlockSpec((tc, d), lambda i: (i, 0)),
            pl.BlockSpec(memory_space=pl.ANY),
        ],
        out_specs=pl.BlockSpec((tc, d), lambda i: (i, 0)),
        out_shape=jax.ShapeDtypeStruct((t, d), F32),
        scratch_shapes=[pltpu.VMEM((TOP_K, tc, d), F32), pltpu.SemaphoreType.DMA],
        compiler_params=_params(("arbitrary",)),
        name="moe_combine",
    )(dest_flat, route, x1, y)


def _moe(h, route, x1, w1, b1g, b1l, w2, b2):
    t, d = h.shape
    n_e = w2.shape[0]
    n_tiles = (t * TOP_K) // MOE_TILE + n_e
    dest, counts = _route(route)
    tiles_per = jnp.ceil(counts[0, :n_e] * (1.0 / MOE_TILE)).astype(jnp.int32)
    tile_end = jnp.cumsum(tiles_per)
    tile_expert = jnp.minimum(jnp.searchsorted(tile_end, jnp.arange(n_tiles), side="right"), n_e - 1).astype(jnp.int32)
    n_used = tile_end[-1:].astype(jnp.int32)
    dest_flat = dest[:, :TOP_K].reshape(-1)
    xs = _dispatch(dest_flat, h, jnp.zeros((n_tiles * MOE_TILE, d), F32))
    y = _experts(tile_expert, n_used, xs, w1, b1g, b1l, w2, b2)
    return _combine(dest_flat, route, x1, y)


def _pad_cols(w, width):
    return jnp.pad(w, ((0, 0), (0, width - w.shape[1])))


def _rope_tables(pos, reps):
    inv = ROPE_THETA ** (-jnp.arange(0, HEAD_DIM, 2, dtype=F32) / HEAD_DIM)
    ang = pos.astype(F32)[:, None] * inv[None, :]
    ang = jnp.concatenate([ang, ang], axis=-1)
    sign = jnp.concatenate([-jnp.ones((HEAD_DIM // 2,), F32), jnp.ones((HEAD_DIM // 2,), F32)])
    return jnp.tile(jnp.cos(ang), (1, reps)), jnp.tile(jnp.sin(ang) * sign, (1, reps))


def _layer(x, mem, g_mix, g_mem, w_in, b_merge, nsa_q_norm, nsa_k_norm, cmp_pe, cmp_w1, cmp_b1, cmp_w2,
           rwkv_shift_mix, rwkv_w0, rwkv_w_up, rwkv_a0, rwkv_a_up, rwkv_g_up, rwkv_k_k, rwkv_k_a, rwkv_r_k,
           rwkv_ln_w, rwkv_ln_b, mem_w_kv, mem_q_norm, mem_k_norm, w_branch, w_out, g_ffn,
           router_w, router_b, exp_w1, exp_b1, exp_w2, exp_b2, cos, sin, cos_c, sin_c):
    b, s, d = x.shape
    t = b * s
    x2 = x.reshape(t, d)

    o = 0
    parts = []
    for width in (NSA_WIDTH, 6 * NSA_KV_WIDTH, 3 * NSA_HEADS,
                  3 * RWKV_WIDTH + DECAY_RANK + AAA_RANK + GATE_RANK, MEM_WIDTH, N_BRANCHES * D_MODEL):
        parts.append(w_in[:, o:o + width])
        o += width
    w_q, w_kv, w_gate, w_rwkv, w_qm, w_merge = parts

    def rwkv_layout(m):
        r3 = m[:, :3 * RWKV_WIDTH]
        xw = m[:, 3 * RWKV_WIDTH:3 * RWKV_WIDTH + DECAY_RANK]
        xa = m[:, 3 * RWKV_WIDTH + DECAY_RANK:3 * RWKV_WIDTH + DECAY_RANK + AAA_RANK]
        xg = m[:, 3 * RWKV_WIDTH + DECAY_RANK + AAA_RANK:]
        return _pad_cols(jnp.concatenate([r3, _pad_cols(xw, LANE), _pad_cols(xa, LANE), xg], axis=1), RWKV_PAD)

    w_all = jnp.concatenate([rwkv_layout(w_rwkv), w_merge, w_q, w_qm, w_kv, w_gate], axis=1)
    w_all = _pad_cols(w_all, IN_PAD).astype(BF16)
    proj2 = _inproj(x2, g_mix.reshape(1, d), w_all)
    proj3 = proj2.reshape(b, s, IN_PAD)

    rowv = lambda a: a.reshape(1, -1)
    pad_rows = lambda m: jnp.pad(m, ((0, LANE - m.shape[0]), (0, 0)))
    o_b = _rwkv(proj3, rwkv_layout(rowv(rwkv_shift_mix)), rowv(rwkv_w0), pad_rows(rwkv_w_up), rowv(rwkv_a0),
                pad_rows(rwkv_a_up), rwkv_g_up, rowv(rwkv_k_k), rowv(rwkv_k_a), rowv(rwkv_r_k),
                rowv(rwkv_ln_w), rowv(rwkv_ln_b))

    q_gain = jnp.tile(nsa_q_norm.reshape(1, HEAD_DIM), (1, NSA_HEADS))
    k_gain = jnp.tile(nsa_k_norm, (1, NSA_KV_HEADS))
    qn, ks, vs, kw, vw = _nsa_prep(proj3, cos, sin, q_gain, k_gain)
    kc, vc = _nsa_cmp(proj3, cos_c, sin_c, k_gain, cmp_pe, cmp_w1.astype(BF16),
                      cmp_b1.reshape(2, 1, CMP_HIDDEN), cmp_w2.astype(BF16))
    o_a = _nsa_attn(qn, proj3, kc, vc, ks, vs, kw, vw)

    mk, mv = _mem_kv(mem, g_mem.reshape(1, d), mem_w_kv.astype(BF16), mem_k_norm.reshape(1, MEM_HEAD_DIM))
    o_m = _mem_attn(proj3, mem_q_norm.reshape(1, MEM_HEAD_DIM), mk, mv)

    rw = _pad_cols(router_w, LANE)
    rb = jnp.concatenate([router_b, jnp.full((LANE - N_EXPERTS,), NEG_BIG, F32)]).reshape(1, LANE)
    x1, h2, route = _merge(o_a.reshape(t, NSA_WIDTH), o_b.reshape(t, RWKV_WIDTH), o_m.reshape(t, MEM_WIDTH),
                          proj2, x2, b_merge.reshape(1, -1), w_branch.astype(BF16), w_out.astype(BF16),
                          g_ffn.reshape(1, d), rw, rb)

    b1g = exp_b1[:, None, 0::2]
    b1l = exp_b1[:, None, 1::2]
    out = _moe(h2, route, x1, exp_w1, b1g, b1l, exp_w2, exp_b2[:, None, :])
    return out.reshape(b, s, d)


def kernel(x, mem, g_mix, g_mem, w_in, b_merge, nsa_q_norm, nsa_k_norm, cmp_pe, cmp_w1, cmp_b1, cmp_w2,
           rwkv_shift_mix, rwkv_w0, rwkv_w_up, rwkv_a0, rwkv_a_up, rwkv_g_up, rwkv_k_k, rwkv_k_a, rwkv_r_k,
           rwkv_ln_w, rwkv_ln_b, mem_w_kv, mem_q_norm, mem_k_norm, w_branch, w_out, g_ffn,
           router_w, router_b, exp_w1, exp_b1, exp_w2, exp_b2):
    s = x.shape[1]
    cos, sin = _rope_tables(jnp.arange(s), NSA_HEADS)
    n_cmp = (s - CMP_LEN) // CMP_STRIDE + 1
    cos_c, sin_c = _rope_tables(jnp.arange(n_cmp + 1) * CMP_STRIDE + CMP_LEN - 1, NSA_KV_HEADS)
    depth = g_mix.shape[0]
    for l in range(depth):
        x = _layer(x, mem, g_mix[l], g_mem[l], w_in[l], b_merge[l], nsa_q_norm[l], nsa_k_norm[l], cmp_pe[l],
                   cmp_w1[l], cmp_b1[l], cmp_w2[l], rwkv_shift_mix[l], rwkv_w0[l], rwkv_w_up[l], rwkv_a0[l],
                   rwkv_a_up[l], rwkv_g_up[l], rwkv_k_k[l], rwkv_k_a[l], rwkv_r_k[l], rwkv_ln_w[l], rwkv_ln_b[l],
                   mem_w_kv[l], mem_q_norm[l], mem_k_norm[l], w_branch[l], w_out[l], g_ffn[l], router_w[l],
                   router_b[l], exp_w1[l], exp_b1[l], exp_w2[l], exp_b2[l], cos, sin, cos_c, sin_c)
    return x
```

```python
import functools

import jax
import jax.numpy as jnp
from jax import lax
from jax.experimental import pallas as pl
from jax.experimental.pallas import tpu as pltpu

F32 = jnp.float32
BF16 = jnp.bfloat16
HI = lax.Precision.HIGHEST

D_MODEL = 1024
HEAD_DIM = 64
NSA_HEADS = 8
NSA_KV_HEADS = 2
NSA_GROUP = NSA_HEADS // NSA_KV_HEADS
NSA_WIDTH = NSA_HEADS * HEAD_DIM
NSA_KV_WIDTH = NSA_KV_HEADS * HEAD_DIM
CMP_LEN = 32
CMP_STRIDE = 16
CMP_HIDDEN = 128
SEL_BLOCK = 64
TOP_N = 8
WINDOW = 512
Q_BLOCK = 128
FORCE_BONUS = 1000.0
RWKV_HEADS = 8
RWKV_HEAD_DIM = 64
RWKV_WIDTH = RWKV_HEADS * RWKV_HEAD_DIM
DECAY_RANK = 64
AAA_RANK = 64
GATE_RANK = 128
GN_EPS = 64e-5
MEM_HEADS = 4
MEM_HEAD_DIM = 128
MEM_WIDTH = MEM_HEADS * MEM_HEAD_DIM
N_BRANCHES = 3
BRANCH_WIDTH = 512
N_EXPERTS = 32
TOP_K = 4
EXPERT_FF = 1024
SWIGLU_ALPHA = 1.702
SWIGLU_LIMIT = 7.0
ROPE_THETA = 10000.0
RMS_EPS = 1e-6

LANE = 128
NEG_BIG = -1e30

RWKV_PAD = 2048
COL_RWKV = 0
COL_MERGE = COL_RWKV + RWKV_PAD
COL_QNSA = COL_MERGE + N_BRANCHES * D_MODEL
COL_QMEM = COL_QNSA + NSA_WIDTH
COL_KV = COL_QMEM + MEM_WIDTH
COL_GATE = COL_KV + 6 * NSA_KV_WIDTH
IN_PAD = 7168
RW_R, RW_K, RW_V, RW_XW, RW_XA, RW_XG = 0, 512, 1024, 1536, 1664, 1792

RWKV_CHUNK = 64
VMEM_LIMIT = 56 * 1024 * 1024


def _dot(a, b, prec=None):
    return jnp.dot(a, b, preferred_element_type=F32, precision=prec)


def _dot_nt(a, b, prec=None):
    return lax.dot_general(a, b, (((1,), (1,)), ((), ())), preferred_element_type=F32, precision=prec)


def _dot_tn(a, b, prec=None):
    return lax.dot_general(a, b, (((0,), (0,)), ((), ())), preferred_element_type=F32, precision=prec)


def _split3(x):
    hi = x.astype(BF16)
    r1 = x - hi.astype(F32)
    mid = r1.astype(BF16)
    lo = (r1 - mid.astype(F32)).astype(BF16)
    return hi, mid, lo


def _dot01_left(m01, x):
    n = x.shape[1]
    out = _dot(m01, jnp.concatenate(_split3(x), axis=1))
    return out[:, :n] + out[:, n:2 * n] + out[:, 2 * n:]


def _dot01_right(x, m01):
    m = x.shape[0]
    out = _dot(jnp.concatenate(_split3(x), axis=0), m01)
    return out[:m] + out[m:2 * m] + out[2 * m:]


def _seg_matrix(width, seg):
    r = lax.broadcasted_iota(jnp.int32, (width, width), 0) // seg
    c = lax.broadcasted_iota(jnp.int32, (width, width), 1) // seg
    return (r == c).astype(F32)


def _params(sem):
    return pltpu.CompilerParams(dimension_semantics=sem, vmem_limit_bytes=VMEM_LIMIT)


def _inproj_kernel(x_ref, g_ref, w_ref, o_ref, hn_ref):
    @pl.when(pl.program_id(1) == 0)
    def _():
        x = x_ref[...]
        ms = jnp.mean(x * x, axis=-1, keepdims=True)
        hn_ref[...] = (x * lax.rsqrt(ms + RMS_EPS) * g_ref[...]).astype(BF16)

    o_ref[...] = _dot(hn_ref[...], w_ref[...])


def _inproj(x2, g, w, tm=1024, tn=512):
    t, d = x2.shape
    n = w.shape[1]
    return pl.pallas_call(
        _inproj_kernel,
        grid=(t // tm, n // tn),
        in_specs=[
            pl.BlockSpec((tm, d), lambda i, j: (i, 0)),
            pl.BlockSpec((1, d), lambda i, j: (0, 0)),
            pl.BlockSpec((d, tn), lambda i, j: (0, j)),
        ],
        out_specs=pl.BlockSpec((tm, tn), lambda i, j: (i, j)),
        out_shape=jax.ShapeDtypeStruct((t, n), F32),
        scratch_shapes=[pltpu.VMEM((tm, d), BF16)],
        compiler_params=_params(("parallel", "arbitrary")),
        name="inproj",
    )(x2, g, w)


def _rwkv_chunk_kernel(p_ref, pprev_ref, mix_ref, w0_ref, wup_ref, a0_ref, aup_ref, gup_ref, kk_ref, ka_ref,
                       rk_ref, rm_ref, y0_ref, bonus_ref, g_ref, gam_ref, m_ref, d0_ref):
    c = RWKV_CHUNK
    n = RWKV_HEAD_DIM
    rows = p_ref.shape[1]

    p = p_ref[0]
    row = lax.broadcasted_iota(jnp.int32, p.shape, 0)
    last_prev = jnp.where(pl.program_id(1) == 0, 0.0, 1.0) * pprev_ref[0, 7:8, :]
    prev = jnp.where(row == 0, last_prev, pltpu.roll(p, 1, axis=0))
    ps = p + (prev - p) * mix_ref[...]
    r = ps[:, RW_R:RW_R + RWKV_WIDTH]
    k = ps[:, RW_K:RW_K + RWKV_WIDTH]
    v = ps[:, RW_V:RW_V + RWKV_WIDTH]
    xw = ps[:, RW_XW:RW_XW + LANE]
    xa = ps[:, RW_XA:RW_XA + LANE]
    xg = ps[:, RW_XG:RW_XG + LANE]

    z = -(w0_ref[...] + _dot(jnp.tanh(xw), wup_ref[...], HI))
    softplus = jnp.maximum(z, 0.0) + jnp.log1p(jnp.exp(-jnp.abs(z)))
    w = -softplus - 0.5
    logw = -jnp.exp(w)
    a = jax.nn.sigmoid(a0_ref[...] + _dot(xa, aup_ref[...], HI))
    g_ref[0] = _dot(jax.nn.sigmoid(xg), gup_ref[...], HI)

    seg = _seg_matrix(RWKV_WIDTH, n).astype(BF16)
    kk = k * kk_ref[...]
    k = k * (1.0 + (a - 1.0) * ka_ref[...])
    sums = _dot01_right(jnp.concatenate([kk * kk, r * k * rk_ref[...]], axis=0), seg)
    kk = kk / jnp.maximum(jnp.sqrt(sums[:rows]), 1e-12)
    bonus_ref[0] = sums[rows:] * v
    kka = kk * a

    ti = lax.broadcasted_iota(jnp.int32, (2 * c, 2 * c), 0)
    tj = lax.broadcasted_iota(jnp.int32, (2 * c, 2 * c), 1)
    keep = (tj % c) < jnp.where(ti < c, ti, ti - c + 1)
    ci = lax.broadcasted_iota(jnp.int32, (c, c), 0)
    cj = lax.broadcasted_iota(jnp.int32, (c, c), 1)
    eye = (ci == cj).astype(F32)
    ltri = (cj <= ci).astype(BF16)
    zeros_cn = jnp.zeros((c, n), BF16)

    chains = []
    for j in range(rows // c):
        rs = slice(j * c, (j + 1) * c)
        lw = logw[rs]
        cum = _dot01_left(ltri, lw)
        tot = cum[c - 1:c, :]
        einv = jnp.exp(-cum)
        dec_end = jnp.exp(tot - cum)
        r_f = r[rs] * jnp.exp(cum)
        a_t = (-kk[rs] * jnp.exp(cum - lw)).astype(BF16)
        b_t = (kka[rs] * einv).astype(BF16)
        k_t = (k[rs] * einv).astype(BF16)
        r_t = r_f.astype(BF16)
        b_e = (kka[rs] * dec_end).astype(BF16)
        k_e = (k[rs] * dec_end).astype(BF16)
        v_b = v[rs].astype(BF16)
        gam_ref[0, j] = jnp.exp(tot)
        for h in range(RWKV_HEADS):
            sl = slice(h * n, (h + 1) * n)
            chains.append(dict(j=j, h=h, rs=rs, sl=sl, a=a_t[:, sl], r=r_t[:, sl], rf=r_f[:, sl], v=v_b[:, sl],
                               rhs=jnp.concatenate([b_t[:, sl], k_t[:, sl]], axis=0),
                               bke=jnp.concatenate([b_e[:, sl], k_e[:, sl]], axis=0)))

    for ch in chains:
        lhs = jnp.concatenate([ch["a"], ch["r"]], axis=0)
        ch["amat"] = jnp.where(keep, _dot_nt(lhs, ch["rhs"]), 0.0)
        ch["pw"] = ch["amat"][:c, :c]
        ch["tinv"] = eye + ch["pw"]
    for _ in range(5):
        for ch in chains:
            pw_b = ch["pw"].astype(BF16)
            ch["pw"] = _dot(pw_b, pw_b)
        for ch in chains:
            ch["tinv"] = ch["tinv"] + _dot(ch["tinv"].astype(BF16), ch["pw"].astype(BF16))
    for ch in chains:
        ch["akv"] = _dot(ch["amat"][:c, c:].astype(BF16), ch["v"])
    for ch in chains:
        wu = _dot(ch["tinv"].astype(BF16), jnp.concatenate([ch["a"], ch["akv"].astype(BF16)], axis=1)).astype(BF16)
        ch["x"] = jnp.concatenate([wu, jnp.concatenate([zeros_cn, ch["v"]], axis=1)], axis=0)
    for ch in chains:
        ry = _dot(ch["amat"][c:, :].astype(BF16), ch["x"])
        rm_ref[0, ch["rs"], ch["sl"]] = (ch["rf"] + ry[:, :n]).astype(BF16)
        y0_ref[0, ch["rs"], ch["sl"]] = ry[:, n:]
    for ch in chains:
        md = _dot_tn(ch["x"], ch["bke"])
        m_ref[0, ch["j"], ch["h"]] = md[:n].astype(BF16)
        d0_ref[0, ch["j"], ch["h"]] = md[n:]


def _rwkv_chunks(proj3, mix, w0, wup, a0, aup, gup, k_k, k_a, r_k, rows=128):
    b, s, _ = proj3.shape
    c = RWKV_CHUNK
    nc = s // c
    cps = rows // c
    vec = lambda width: pl.BlockSpec((1, width), lambda i, t: (0, 0))
    mat = lambda nrows: pl.BlockSpec((nrows, RWKV_WIDTH), lambda i, t: (0, 0))
    tok = pl.BlockSpec((1, rows, RWKV_WIDTH), lambda i, t: (i, t, 0))
    sq = pl.BlockSpec((1, cps, RWKV_HEADS, RWKV_HEAD_DIM, RWKV_HEAD_DIM), lambda i, t: (i, t, 0, 0, 0))
    tok_shape = lambda dt: jax.ShapeDtypeStruct((b, s, RWKV_WIDTH), dt)
    sq_shape = lambda dt: jax.ShapeDtypeStruct((b, nc, RWKV_HEADS, RWKV_HEAD_DIM, RWKV_HEAD_DIM), dt)
    return pl.pallas_call(
        _rwkv_chunk_kernel,
        grid=(b, s // rows),
        in_specs=[
            pl.BlockSpec((1, rows, RWKV_PAD), lambda i, t: (i, t, COL_RWKV // RWKV_PAD)),
            pl.BlockSpec((1, 8, RWKV_PAD), lambda i, t: (i, jnp.maximum(t * (rows // 8) - 1, 0), COL_RWKV // RWKV_PAD)),
            vec(RWKV_PAD), vec(RWKV_WIDTH), mat(LANE), vec(RWKV_WIDTH), mat(LANE), mat(LANE),
            vec(RWKV_WIDTH), vec(RWKV_WIDTH), vec(RWKV_WIDTH),
        ],
        out_specs=[tok, tok, tok, tok,
                   pl.BlockSpec((1, cps, 1, RWKV_WIDTH), lambda i, t: (i, t, 0, 0)), sq, sq],
        out_shape=[tok_shape(BF16), tok_shape(F32), tok_shape(F32), tok_shape(F32),
                   jax.ShapeDtypeStruct((b, nc, 1, RWKV_WIDTH), F32), sq_shape(BF16), sq_shape(F32)],
        compiler_params=_params(("parallel", "parallel")),
        name="rwkv7_chunks",
    )(proj3, proj3, mix, w0, wup, a0, aup, gup, k_k, k_a, r_k)


def _rwkv_scan_kernel(rm_ref, y0_ref, bonus_ref, g_ref, gam_ref, m_ref, d0_ref, lnw_ref, lnb_ref,
                      o_ref, state_ref, y_ref):
    c = RWKV_CHUNK
    n = RWKV_HEAD_DIM

    @pl.when(pl.program_id(1) == 0)
    def _():
        state_ref[...] = jnp.zeros_like(state_ref)

    for j in range(gam_ref.shape[1]):
        rs = slice(j * c, (j + 1) * c)
        gam = gam_ref[0, j]
        for h in range(RWKV_HEADS):
            sl = slice(h * n, (h + 1) * n)
            s = state_ref[h]
            s_b = s.astype(BF16)
            y_ref[rs, sl] = _dot_nt(rm_ref[0, rs, sl], s_b) + y0_ref[0, rs, sl]
            state_ref[h] = s * gam[:, sl] + _dot(s_b, m_ref[0, j, h]) + d0_ref[0, j, h]

    seg = _seg_matrix(RWKV_WIDTH, n).astype(BF16)
    y = y_ref[...]
    mu = _dot01_right(y, seg) * (1.0 / n)
    d = y - mu
    var = _dot01_right(d * d, seg) * (1.0 / n)
    yn = d * lax.rsqrt(var + GN_EPS) * lnw_ref[...] + lnb_ref[...]
    o_ref[0] = (yn + bonus_ref[0]) * g_ref[0]


def _rwkv_scan(rm, y0, bonus, g, gam, m, d0, ln_w, ln_b, rows=256):
    b, s, _ = rm.shape
    cps = rows // RWKV_CHUNK
    vec = pl.BlockSpec((1, RWKV_WIDTH), lambda i, t: (0, 0))
    tok = pl.BlockSpec((1, rows, RWKV_WIDTH), lambda i, t: (i, t, 0))
    sq = pl.BlockSpec((1, cps, RWKV_HEADS, RWKV_HEAD_DIM, RWKV_HEAD_DIM), lambda i, t: (i, t, 0, 0, 0))
    return pl.pallas_call(
        _rwkv_scan_kernel,
        grid=(b, s // rows),
        in_specs=[tok, tok, tok, tok, pl.BlockSpec((1, cps, 1, RWKV_WIDTH), lambda i, t: (i, t, 0, 0)), sq, sq,
                  vec, vec],
        out_specs=tok,
        out_shape=jax.ShapeDtypeStruct((b, s, RWKV_WIDTH), F32),
        scratch_shapes=[
            pltpu.VMEM((RWKV_HEADS, RWKV_HEAD_DIM, RWKV_HEAD_DIM), F32),
            pltpu.VMEM((rows, RWKV_WIDTH), F32),
        ],
        compiler_params=_params(("parallel", "arbitrary")),
        name="rwkv7_scan",
    )(rm, y0, bonus, g, gam, m, d0, ln_w, ln_b)


def _rwkv(proj3, mix, w0, wup, a0, aup, gup, k_k, k_a, r_k, ln_w, ln_b):
    rm, y0, bonus, g, gam, m, d0 = _rwkv_chunks(proj3, mix, w0, wup, a0, aup, gup, k_k, k_a, r_k)
    return _rwkv_scan(rm, y0, bonus, g, gam, m, d0, ln_w, ln_b)


def _rope(x, cos, sin_signed):
    w = x.shape[-1]
    first_half = (lax.broadcasted_iota(jnp.int32, x.shape, 1) % HEAD_DIM) < (HEAD_DIM // 2)
    rot = jnp.where(first_half, pltpu.roll(x, w - HEAD_DIM // 2, axis=1), pltpu.roll(x, HEAD_DIM // 2, axis=1))
    return x * cos + rot * sin_signed


def _head_rmsnorm(x, gain, seg):
    ms = _dot(x * x, seg, HI) * (1.0 / HEAD_DIM)
    return x * lax.rsqrt(ms + RMS_EPS) * gain


def _split_groups(x):
    return [x[:, g * HEAD_DIM:(g + 1) * HEAD_DIM] for g in range(NSA_KV_HEADS)]


def _nsa_prep_kernel(q_ref, ksl_ref, vsl_ref, kwn_ref, vwn_ref, cos_ref, sin_ref, qn_ref, kn_ref,
                     qo_ref, ks_ref, vs_ref, kw_ref, vw_ref):
    seg_q = _seg_matrix(NSA_WIDTH, HEAD_DIM)
    seg_k = _seg_matrix(NSA_KV_WIDTH, HEAD_DIM)
    cos_k = cos_ref[:, :NSA_KV_WIDTH]
    sin_k = sin_ref[:, :NSA_KV_WIDTH]

    q = _rope(_head_rmsnorm(q_ref[0], qn_ref[...], seg_q), cos_ref[...], sin_ref[...])
    q_t = (q * (HEAD_DIM ** -0.5)).T
    for h in range(NSA_HEADS):
        qo_ref[0, h] = q_t[h * HEAD_DIM:(h + 1) * HEAD_DIM].astype(BF16)

    ks = _rope(_head_rmsnorm(ksl_ref[0], kn_ref[1:2, :], seg_k), cos_k, sin_k)
    kw = _rope(_head_rmsnorm(kwn_ref[0], kn_ref[2:3, :], seg_k), cos_k, sin_k)
    vs_t = vsl_ref[0].T
    vw_t = vwn_ref[0].T
    for g, (a, c_) in enumerate(zip(_split_groups(ks), _split_groups(kw))):
        ks_ref[0, g] = a.astype(BF16)
        kw_ref[0, g] = c_.astype(BF16)
        vs_ref[0, g] = vs_t[g * HEAD_DIM:(g + 1) * HEAD_DIM].astype(BF16)
        vw_ref[0, g] = vw_t[g * HEAD_DIM:(g + 1) * HEAD_DIM].astype(BF16)


def _nsa_prep(proj3, cos, sin_signed, q_norm, k_norm, tq=512):
    b, s, _ = proj3.shape
    kvw = NSA_KV_WIDTH
    kv_spec = lambda j: pl.BlockSpec((1, tq, kvw), lambda i, t: (i, t, COL_KV // kvw + j))
    out_k = pl.BlockSpec((1, NSA_KV_HEADS, tq, HEAD_DIM), lambda i, t: (i, 0, t, 0))
    out_vt = pl.BlockSpec((1, NSA_KV_HEADS, HEAD_DIM, tq), lambda i, t: (i, 0, 0, t))
    k_shape = jax.ShapeDtypeStruct((b, NSA_KV_HEADS, s, HEAD_DIM), BF16)
    vt_shape = jax.ShapeDtypeStruct((b, NSA_KV_HEADS, HEAD_DIM, s), BF16)
    return pl.pallas_call(
        _nsa_prep_kernel,
        grid=(b, s // tq),
        in_specs=[
            pl.BlockSpec((1, tq, NSA_WIDTH), lambda i, t: (i, t, COL_QNSA // NSA_WIDTH)),
            kv_spec(2), kv_spec(3), kv_spec(4), kv_spec(5),
            pl.BlockSpec((tq, NSA_WIDTH), lambda i, t: (t, 0)),
            pl.BlockSpec((tq, NSA_WIDTH), lambda i, t: (t, 0)),
            pl.BlockSpec((1, NSA_WIDTH), lambda i, t: (0, 0)),
            pl.BlockSpec((3, kvw), lambda i, t: (0, 0)),
        ],
        out_specs=[
            pl.BlockSpec((1, NSA_HEADS, HEAD_DIM, tq), lambda i, t: (i, 0, 0, t)),
            out_k, out_vt, out_k, out_vt,
        ],
        out_shape=[jax.ShapeDtypeStruct((b, NSA_HEADS, HEAD_DIM, s), BF16), k_shape, vt_shape, k_shape, vt_shape],
        compiler_params=_params(("parallel", "parallel")),
        name="nsa_prep",
    )(proj3, proj3, proj3, proj3, proj3, cos, sin_signed, q_norm, k_norm)


def _gelu_tanh(x):
    return 0.5 * x * (1.0 + jnp.tanh(0.7978845608028654 * (x + 0.044715 * x * x * x)))


def _nsa_cmp_kernel(kc_in_ref, vc_in_ref, cos_ref, sin_ref, kn_ref, pe_ref, w1_ref, b1_ref, w2_ref,
                    kc_ref, vc_ref):
    n_cmp = (kc_in_ref.shape[1] - CMP_LEN) // CMP_STRIDE + 1
    n_pad = n_cmp + 1
    zero_row = jnp.zeros((1, NSA_KV_WIDTH), F32)
    outs = []
    for j, src in enumerate((kc_in_ref, vc_in_ref)):
        acc = jnp.zeros((NSA_KV_HEADS * n_pad, CMP_HIDDEN), F32)
        for l in range(CMP_LEN):
            x = src[0, pl.ds(l, n_cmp, stride=CMP_STRIDE), :]
            x = jnp.concatenate([x, zero_row], axis=0)
            xg = jnp.concatenate(_split_groups(x), axis=0) + pe_ref[j, l:l + 1, :]
            acc = acc + _dot(xg.astype(BF16), w1_ref[j, l * HEAD_DIM:(l + 1) * HEAD_DIM, :])
        hid = _gelu_tanh(acc + b1_ref[j])
        out = _dot(hid.astype(BF16), w2_ref[j])
        outs.append(jnp.concatenate([out[g * n_pad:(g + 1) * n_pad] for g in range(NSA_KV_HEADS)], axis=1))
    kc, vc = outs
    seg_k = _seg_matrix(NSA_KV_WIDTH, HEAD_DIM)
    kc = _rope(_head_rmsnorm(kc, kn_ref[0:1, :], seg_k), cos_ref[...], sin_ref[...])
    vc_t = vc.T
    for g, a in enumerate(_split_groups(kc)):
        kc_ref[0, g] = a.astype(BF16)
        vc_ref[0, g] = vc_t[g * HEAD_DIM:(g + 1) * HEAD_DIM].astype(BF16)


def _nsa_cmp(proj3, cos_c, sin_c, k_norm, pe, w1, b1, w2):
    b, s, _ = proj3.shape
    kvw = NSA_KV_WIDTH
    n_pad = (s - CMP_LEN) // CMP_STRIDE + 2
    full = lambda shape: pl.BlockSpec(shape, lambda i: (0,) * len(shape))
    k_spec = pl.BlockSpec((1, NSA_KV_HEADS, n_pad, HEAD_DIM), lambda i: (i, 0, 0, 0))
    vt_spec = pl.BlockSpec((1, NSA_KV_HEADS, HEAD_DIM, n_pad), lambda i: (i, 0, 0, 0))
    k_shape = jax.ShapeDtypeStruct((b, NSA_KV_HEADS, n_pad, HEAD_DIM), BF16)
    vt_shape = jax.ShapeDtypeStruct((b, NSA_KV_HEADS, HEAD_DIM, n_pad), BF16)
    return pl.pallas_call(
        _nsa_cmp_kernel,
        grid=(b,),
        in_specs=[
            pl.BlockSpec((1, s, kvw), lambda i: (i, 0, COL_KV // kvw)),
            pl.BlockSpec((1, s, kvw), lambda i: (i, 0, COL_KV // kvw + 1)),
            full((n_pad, kvw)), full((n_pad, kvw)), full((3, kvw)),
            full((2, CMP_LEN, HEAD_DIM)), full((2, CMP_LEN * HEAD_DIM, CMP_HIDDEN)),
            full((2, 1, CMP_HIDDEN)), full((2, CMP_HIDDEN, HEAD_DIM)),
        ],
        out_specs=[k_spec, vt_spec],
        out_shape=[k_shape, vt_shape],
        compiler_params=_params(("parallel",)),
        name="nsa_compress",
    )(proj3, proj3, cos_c, sin_c, k_norm, pe, w1, b1, w2)


def _nsa_attn_kernel(q_ref, gate_ref, kc_ref, vc_ref, ks_ref, vs_ref, kw_ref, vw_ref, o_ref):
    qb = Q_BLOCK
    hg = NSA_GROUP
    g = pl.program_id(1)
    c = pl.program_id(2)
    cols = hg * qb
    qt = jnp.concatenate([q_ref[0, h] for h in range(hg)], axis=1)
    tile_heads = lambda x: jnp.concatenate([x] * hg, axis=1)

    n_cp = kc_ref.shape[2]
    s_c = _dot(kc_ref[0, 0], qt)
    cend = lax.broadcasted_iota(jnp.int32, (n_cp, qb), 0) * CMP_STRIDE + CMP_LEN - 1
    cmask = tile_heads(jnp.where(cend <= c * qb + lax.broadcasted_iota(jnp.int32, (n_cp, qb), 1), 1, 0)) > 0
    s_m = jnp.where(cmask, s_c, -jnp.inf)
    m = jnp.max(s_m, axis=0, keepdims=True)
    m = jnp.where(m == -jnp.inf, 0.0, m)
    e = jnp.where(cmask, jnp.exp(s_m - m), 0.0)
    p_c = e / jnp.maximum(jnp.sum(e, axis=0, keepdims=True), 1e-30)
    o_cmp = _dot(vc_ref[0, 0], p_c.astype(BF16))

    n_blk = ks_ref.shape[2] // SEL_BLOCK
    p_sum = p_c[:, 0:qb]
    for h in range(1, hg):
        p_sum = p_sum + p_c[:, h * qb:(h + 1) * qb]
    bi = lax.broadcasted_iota(jnp.int32, (n_blk, n_cp), 0) * SEL_BLOCK
    ci = lax.broadcasted_iota(jnp.int32, (n_blk, n_cp), 1) * CMP_STRIDE
    cover_t = ((ci < bi + SEL_BLOCK) & (ci + CMP_LEN > bi)).astype(F32)
    imp = _dot(cover_t, p_sum, HI)
    blk = lax.broadcasted_iota(jnp.int32, (n_blk, qb), 0)
    qpos = c * qb + lax.broadcasted_iota(jnp.int32, (n_blk, qb), 1)
    cur = qpos // SEL_BLOCK
    forced = (blk == 0) | (blk == cur) | (blk == cur - 1)
    imp = jnp.where(forced, imp + FORCE_BONUS, imp)
    imp = jnp.where(blk <= cur, imp, -jnp.inf)
    rank = jnp.zeros((n_blk, qb), F32)
    for j in range(n_blk):
        other = imp[j:j + 1, :]
        rank = rank + jnp.where(blk > j, jnp.where(other >= imp, 1.0, 0.0), jnp.where(other > imp, 1.0, 0.0))
    sel_t = jnp.where(rank < min(TOP_N, n_blk), 1.0, 0.0).astype(BF16)

    kt_w = 4 * LANE
    kq = c * qb + lax.broadcasted_iota(jnp.int32, (kt_w, qb), 1)
    krel = lax.broadcasted_iota(jnp.int32, (kt_w, qb), 0)
    er = lax.broadcasted_iota(jnp.int32, (kt_w, n_blk), 0) // SEL_BLOCK
    ec = lax.broadcasted_iota(jnp.int32, (kt_w, n_blk), 1)

    def sel_body(kt, carry):
        m_i, l_i, acc = carry
        off = pl.multiple_of(kt * kt_w, kt_w)
        expand = jnp.where(ec == er + kt * (kt_w // SEL_BLOCK), 1.0, 0.0).astype(BF16)
        in_sel = _dot(expand, sel_t)
        bias = jnp.where((in_sel > 0.5) & (krel + off <= kq), 0.0, NEG_BIG)
        s = _dot(ks_ref[0, 0, pl.ds(off, kt_w), :], qt) + tile_heads(bias)
        m_new = jnp.maximum(m_i, jnp.max(s, axis=0, keepdims=True))
        alpha = jnp.exp(m_i - m_new)
        p = jnp.exp(s - m_new)
        l_new = alpha * l_i + jnp.sum(p, axis=0, keepdims=True)
        acc = alpha * acc + _dot(vs_ref[0, 0, :, pl.ds(off, kt_w)], p.astype(BF16))
        return m_new, l_new, acc

    init = (jnp.full((1, cols), NEG_BIG, F32), jnp.zeros((1, cols), F32), jnp.zeros((HEAD_DIM, cols), F32))
    _, l_s, acc_s = lax.fori_loop(0, (c * qb + qb + kt_w - 1) // kt_w, sel_body, init)
    o_slc = acc_s / l_s

    span = WINDOW + 2 * qb
    w0 = pl.multiple_of(jnp.clip(c * qb - WINDOW, 0, kw_ref.shape[2] - span), qb)
    kpos = w0 + lax.broadcasted_iota(jnp.int32, (span, qb), 0)
    wq = c * qb + lax.broadcasted_iota(jnp.int32, (span, qb), 1)
    wbias = jnp.where((kpos <= wq) & (kpos > wq - WINDOW), 0.0, NEG_BIG)
    s_w = _dot(kw_ref[0, 0, pl.ds(w0, span), :], qt) + tile_heads(wbias)
    e_w = jnp.exp(s_w - jnp.max(s_w, axis=0, keepdims=True))
    o_win = _dot(vw_ref[0, 0, :, pl.ds(w0, span)], e_w.astype(BF16)) / jnp.sum(e_w, axis=0, keepdims=True)

    gates_t = jax.nn.sigmoid(gate_ref[0]).T
    grow = lax.broadcasted_iota(jnp.int32, gates_t.shape, 0)
    outs = []
    for h in range(hg):
        hc = slice(h * qb, (h + 1) * qb)
        first = (g * hg + h) * 3
        pick = lambda j: jnp.sum(jnp.where(grow == first + j, gates_t, 0.0), axis=0, keepdims=True)
        outs.append(pick(0) * o_cmp[:, hc] + pick(1) * o_slc[:, hc] + pick(2) * o_win[:, hc])
    o_ref[0] = jnp.concatenate(outs, axis=0).T


def _nsa_attn(q, proj3, kc, vc, ks, vs, kw, vw):
    b, _, _, s = q.shape
    gw = NSA_GROUP * HEAD_DIM
    n_pad = kc.shape[2]
    per_group = lambda rows, width: pl.BlockSpec((1, 1, rows, width), lambda i, g, c: (i, g, 0, 0))
    return pl.pallas_call(
        _nsa_attn_kernel,
        grid=(b, NSA_KV_HEADS, s // Q_BLOCK),
        in_specs=[
            pl.BlockSpec((1, NSA_GROUP, HEAD_DIM, Q_BLOCK), lambda i, g, c: (i, g, 0, c)),
            pl.BlockSpec((1, Q_BLOCK, LANE), lambda i, g, c: (i, c, COL_GATE // LANE)),
            per_group(n_pad, HEAD_DIM), per_group(HEAD_DIM, n_pad),
            per_group(s, HEAD_DIM), per_group(HEAD_DIM, s), per_group(s, HEAD_DIM), per_group(HEAD_DIM, s),
        ],
        out_specs=pl.BlockSpec((1, Q_BLOCK, gw), lambda i, g, c: (i, c, g)),
        out_shape=jax.ShapeDtypeStruct((b, s, NSA_WIDTH), F32),
        compiler_params=_params(("parallel", "parallel", "arbitrary")),
        name="nsa_attn",
    )(q, proj3, kc, vc, ks, vs, kw, vw)


def _mem_kv_kernel(mem_ref, g_ref, w_ref, kn_ref, k_ref, v_ref):
    x = mem_ref[0]
    ms = jnp.mean(x * x, axis=-1, keepdims=True)
    xn = (x * lax.rsqrt(ms + RMS_EPS) * g_ref[...]).astype(BF16)
    kv = _dot(xn, w_ref[...])
    for h in range(MEM_HEADS):
        sl = slice(h * MEM_HEAD_DIM, (h + 1) * MEM_HEAD_DIM)
        kh = kv[:, sl]
        kms = jnp.mean(kh * kh, axis=-1, keepdims=True)
        k_ref[0, :, sl] = (kh * lax.rsqrt(kms + RMS_EPS) * kn_ref[...]).astype(BF16)
    v_ref[0] = kv[:, MEM_WIDTH:].astype(BF16)


def _mem_kv(mem, g_mem, w_kv, k_norm):
    b, m, d = mem.shape
    spec = pl.BlockSpec((1, m, MEM_WIDTH), lambda i: (i, 0, 0))
    shape = jax.ShapeDtypeStruct((b, m, MEM_WIDTH), BF16)
    return pl.pallas_call(
        _mem_kv_kernel,
        grid=(b,),
        in_specs=[
            pl.BlockSpec((1, m, d), lambda i: (i, 0, 0)),
            pl.BlockSpec((1, d), lambda i: (0, 0)),
            pl.BlockSpec((d, 2 * MEM_WIDTH), lambda i: (0, 0)),
            pl.BlockSpec((1, MEM_HEAD_DIM), lambda i: (0, 0)),
        ],
        out_specs=[spec, spec],
        out_shape=[shape, shape],
        compiler_params=_params(("parallel",)),
        name="mem_kv",
    )(mem, g_mem, w_kv, k_norm)


def _mem_attn_kernel(q_ref, qn_ref, k_ref, v_ref, o_ref):
    q = q_ref[0]
    for h in range(MEM_HEADS):
        sl = slice(h * MEM_HEAD_DIM, (h + 1) * MEM_HEAD_DIM)
        qh = q[:, sl]
        ms = jnp.mean(qh * qh, axis=-1, keepdims=True)
        qh = (qh * lax.rsqrt(ms + RMS_EPS) * qn_ref[...]).astype(BF16)
        s = _dot_nt(qh, k_ref[0, :, sl]) * (MEM_HEAD_DIM ** -0.5)
        m = jnp.max(s, axis=-1, keepdims=True)
        e = jnp.exp(s - m)
        p = e / jnp.sum(e, axis=-1, keepdims=True)
        o_ref[0, :, sl] = _dot(p.astype(BF16), v_ref[0, :, sl])


def _mem_attn(proj3, q_norm, k, v, tq=512):
    b, s, _ = proj3.shape
    m = k.shape[1]
    kv_spec = pl.BlockSpec((1, m, MEM_WIDTH), lambda i, t: (i, 0, 0))
    return pl.pallas_call(
        _mem_attn_kernel,
        grid=(b, s // tq),
        in_specs=[
            pl.BlockSpec((1, tq, MEM_WIDTH), lambda i, t: (i, t, COL_QMEM // MEM_WIDTH)),
            pl.BlockSpec((1, MEM_HEAD_DIM), lambda i, t: (0, 0)),
            kv_spec, kv_spec,
        ],
        out_specs=pl.BlockSpec((1, tq, MEM_WIDTH), lambda i, t: (i, t, 0)),
        out_shape=jax.ShapeDtypeStruct((b, s, MEM_WIDTH), F32),
        compiler_params=_params(("parallel", "parallel")),
        name="mem_attn",
    )(proj3, q_norm, k, v)


def _merge_kernel(oa_ref, ob_ref, om_ref, l0_ref, l1_ref, l2_ref, x_ref, bm_ref, wb_ref, wo_ref,
                  gf_ref, rw_ref, rb_ref, x1_ref, h_ref, route_ref):
    mixed = None
    for n, (o_ref, l_ref) in enumerate(((oa_ref, l0_ref), (ob_ref, l1_ref), (om_ref, l2_ref))):
        gate = jax.nn.sigmoid(l_ref[...] + bm_ref[:, n * D_MODEL:(n + 1) * D_MODEL])
        term = gate * _dot(o_ref[...].astype(BF16), wb_ref[n])
        mixed = term if mixed is None else mixed + term
    x1 = x_ref[...] + _dot(mixed.astype(BF16), wo_ref[...])
    x1_ref[...] = x1
    ms = jnp.mean(x1 * x1, axis=-1, keepdims=True)
    h = x1 * lax.rsqrt(ms + RMS_EPS) * gf_ref[...]
    h_ref[...] = h

    logits = _dot(h, rw_ref[...], HI) + rb_ref[...]
    lane_f = lax.broadcasted_iota(jnp.int32, logits.shape, 1).astype(F32)
    work = logits
    picks = []
    for _ in range(TOP_K):
        mx = jnp.max(work, axis=-1, keepdims=True)
        idx = jnp.min(jnp.where(work == mx, lane_f, 2.0 * LANE), axis=-1, keepdims=True)
        picks.append((idx, mx))
        work = jnp.where(lane_f == idx, -jnp.inf, work)
    exps = [jnp.exp(mx - picks[0][1]) for _, mx in picks]
    denom = functools.reduce(lambda a, b: a + b, exps)
    route = jnp.zeros(logits.shape, F32)
    for kk, ((idx, _), ex) in enumerate(zip(picks, exps)):
        route = jnp.where(lane_f == kk, idx, route)
        route = jnp.where(lane_f == TOP_K + kk, ex / denom, route)
    route_ref[...] = route


def _merge(o_a, o_b, o_m, proj2, x2, b_merge, w_branch, w_out, g_ffn, router_w, router_b, tm=512):
    t, d = x2.shape
    row = lambda width: pl.BlockSpec((tm, width), lambda i: (i, 0))
    logit = lambda n: pl.BlockSpec((tm, d), lambda i: (i, COL_MERGE // d + n))
    full = lambda shape: pl.BlockSpec(shape, lambda i: (0,) * len(shape))
    return pl.pallas_call(
        _merge_kernel,
        grid=(t // tm,),
        in_specs=[
            row(BRANCH_WIDTH), row(BRANCH_WIDTH), row(BRANCH_WIDTH), logit(0), logit(1), logit(2), row(d),
            full((1, N_BRANCHES * d)), full((N_BRANCHES, BRANCH_WIDTH, d)), full((d, d)), full((1, d)),
            full((d, LANE)), full((1, LANE)),
        ],
        out_specs=[row(d), row(d), row(LANE)],
        out_shape=[jax.ShapeDtypeStruct((t, d), F32), jax.ShapeDtypeStruct((t, d), F32),
                   jax.ShapeDtypeStruct((t, LANE), F32)],
        compiler_params=_params(("parallel",)),
        name="merge_router",
    )(o_a, o_b, o_m, proj2, proj2, proj2, x2, b_merge, w_branch, w_out, g_ffn, router_w, router_b)


def _deinterleave(x):
    rows, width = x.shape
    lane = lax.broadcasted_iota(jnp.int32, (rows, LANE), 1)
    half = LANE // 2
    low = lane < half
    idx = jnp.where(low, 2 * lane, 2 * (lane - half) + 1)
    evens, odds = [], []
    for j in range(0, width, 2 * LANE):
        a = jnp.take_along_axis(x[:, j:j + LANE], idx, axis=1)
        b = jnp.take_along_axis(x[:, j + LANE:j + 2 * LANE], idx, axis=1)
        evens.append(jnp.where(low, a, pltpu.roll(b, half, axis=1)))
        odds.append(jnp.where(low, pltpu.roll(a, half, axis=1), b))
    return jnp.concatenate(evens, axis=1), jnp.concatenate(odds, axis=1)


MOE_TILE = 256


def _route_kernel(route_ref, dest_ref, counts_ref, carry_ref, start_ref):
    phase = pl.program_id(0)
    i = pl.program_id(1)
    tr = route_ref.shape[0]
    route = route_ref[...]
    lane = lax.broadcasted_iota(jnp.int32, route.shape, 1)
    lane_f = lane.astype(F32)
    hits = [lane_f == route[:, kk:kk + 1] for kk in range(TOP_K)]
    sel = jnp.zeros(route.shape, F32)
    for hit in hits:
        sel = jnp.where(hit, 1.0, sel)

    @pl.when((phase == 0) & (i == 0))
    def _():
        carry_ref[...] = jnp.zeros_like(carry_ref)

    @pl.when((phase == 1) & (i == 0))
    def _():
        counts = carry_ref[...]
        counts_ref[...] = counts
        padded = jnp.ceil(counts * (1.0 / MOE_TILE)) * MOE_TILE
        ui = lax.broadcasted_iota(jnp.int32, (LANE, LANE), 0)
        uj = lax.broadcasted_iota(jnp.int32, (LANE, LANE), 1)
        start_ref[...] = _dot01_right(jnp.broadcast_to(padded, (8, LANE)), (ui < uj).astype(BF16))[0:1]
        carry_ref[...] = jnp.zeros_like(carry_ref)

    @pl.when(phase == 1)
    def _():
        ri = lax.broadcasted_iota(jnp.int32, (tr, tr), 0)
        rj = lax.broadcasted_iota(jnp.int32, (tr, tr), 1)
        rank = carry_ref[...] + _dot((rj < ri).astype(BF16), sel.astype(BF16))
        row = start_ref[...] + rank
        dest = jnp.zeros(route.shape, F32)
        for kk, hit in enumerate(hits):
            dest = jnp.where(lane == kk, jnp.sum(jnp.where(hit, row, 0.0), axis=-1, keepdims=True), dest)
        dest_ref[...] = dest.astype(jnp.int32)

    carry_ref[...] += jnp.sum(sel, axis=0, keepdims=True)


def _route(route, tr=512):
    t = route.shape[0]
    return pl.pallas_call(
        _route_kernel,
        grid=(2, t // tr),
        in_specs=[pl.BlockSpec((tr, LANE), lambda p, i: (i, 0))],
        out_specs=[pl.BlockSpec((tr, LANE), lambda p, i: (i * p, 0)), pl.BlockSpec((1, LANE), lambda p, i: (0, 0))],
        out_shape=[jax.ShapeDtypeStruct((t, LANE), jnp.int32), jax.ShapeDtypeStruct((1, LANE), F32)],
        scratch_shapes=[pltpu.VMEM((1, LANE), F32), pltpu.VMEM((1, LANE), F32)],
        compiler_params=_params(("arbitrary", "arbitrary")),
        name="moe_route",
    )(route)


def _row_copy(src, src_row, dst, dst_row, sem):
    return pltpu.make_async_copy(src.at[pl.ds(src_row, 1)], dst.at[pl.ds(dst_row, 1)], sem)


def _dispatch_kernel(dest_ref, h_ref, xs_in_ref, xs_ref, sem):
    del xs_in_ref
    td = h_ref.shape[0]

    def issue(r, carry):
        for kk in range(TOP_K):
            _row_copy(h_ref, r, xs_ref, dest_ref[r * TOP_K + kk], sem).start(priority=kk % 2)
        return carry

    lax.fori_loop(0, td, issue, 0, unroll=8)

    def drain(r, carry):
        for kk in range(TOP_K):
            _row_copy(h_ref, 0, xs_ref, 0, sem).wait()
        return carry

    lax.fori_loop(0, td, drain, 0, unroll=8)


def _dispatch(dest_flat, h, xs_zero, td=256):
    t, d = h.shape
    return pl.pallas_call(
        _dispatch_kernel,
        grid=(t // td,),
        in_specs=[
            pl.BlockSpec((td * TOP_K,), lambda i: (i,), memory_space=pltpu.SMEM),
            pl.BlockSpec((td, d), lambda i: (i, 0)),
            pl.BlockSpec(memory_space=pl.ANY),
        ],
        out_specs=pl.BlockSpec(memory_space=pl.ANY),
        out_shape=jax.ShapeDtypeStruct(xs_zero.shape, xs_zero.dtype),
        scratch_shapes=[pltpu.SemaphoreType.DMA],
        input_output_aliases={2: 0},
        compiler_params=_params(("arbitrary",)),
        name="moe_dispatch",
    )(dest_flat, h, xs_zero)


def _experts_kernel(te_ref, nu_ref, xs_ref, w1_ref, b1g_ref, b1l_ref, w2_ref, b2_ref, y_ref, w1b_ref, w2b_ref):
    j = pl.program_id(0)

    @pl.when(j < nu_ref[0])
    def _():
        @pl.when((j == 0) | (te_ref[j] != te_ref[jnp.maximum(j - 1, 0)]))
        def _():
            w1b_ref[...] = w1_ref[0].astype(BF16)
            w2b_ref[...] = w2_ref[0].astype(BF16)

        hid_g, hid_l = _deinterleave(_dot(xs_ref[...].astype(BF16), w1b_ref[...]))
        x_glu = jnp.minimum(hid_g + b1g_ref[0], SWIGLU_LIMIT)
        x_lin = jnp.clip(hid_l + b1l_ref[0], -SWIGLU_LIMIT, SWIGLU_LIMIT)
        act = x_glu * jax.nn.sigmoid(SWIGLU_ALPHA * x_glu) * (x_lin + 1.0)
        y_ref[...] = _dot(act.astype(BF16), w2b_ref[...]) + b2_ref[0]

    @pl.when(j >= nu_ref[0])
    def _():
        y_ref[...] = jnp.zeros_like(y_ref)


def _experts(tile_expert, n_used, xs, w1, b1g, b1l, w2, b2):
    rows, d = xs.shape
    n_e, ff, _ = w2.shape
    tg = MOE_TILE
    used = lambda j, te, nu: jnp.minimum(j, nu[0] - 1)
    exp_of = lambda j, te, nu: te[used(j, te, nu)]
    return pl.pallas_call(
        _experts_kernel,
        grid_spec=pltpu.PrefetchScalarGridSpec(
            num_scalar_prefetch=2,
            grid=(rows // tg,),
            in_specs=[
                pl.BlockSpec((tg, d), lambda j, te, nu: (used(j, te, nu), 0)),
                pl.BlockSpec((1, d, 2 * ff), lambda j, te, nu: (exp_of(j, te, nu), 0, 0)),
                pl.BlockSpec((1, 1, ff), lambda j, te, nu: (exp_of(j, te, nu), 0, 0)),
                pl.BlockSpec((1, 1, ff), lambda j, te, nu: (exp_of(j, te, nu), 0, 0)),
                pl.BlockSpec((1, ff, d), lambda j, te, nu: (exp_of(j, te, nu), 0, 0)),
                pl.BlockSpec((1, 1, d), lambda j, te, nu: (exp_of(j, te, nu), 0, 0)),
            ],
            out_specs=pl.BlockSpec((tg, d), lambda j, te, nu: (j, 0)),
            scratch_shapes=[pltpu.VMEM((d, 2 * ff), BF16), pltpu.VMEM((ff, d), BF16)],
        ),
        out_shape=jax.ShapeDtypeStruct((rows, d), F32),
        compiler_params=_params(("arbitrary",)),
        name="moe_experts",
    )(tile_expert, n_used, xs, w1, b1g, b1l, w2, b2)


def _combine_kernel(dest_ref, route_ref, x1_ref, y_ref, o_ref, buf_ref, sem):
    tc = x1_ref.shape[0]

    def issue(r, carry):
        for kk in range(TOP_K):
            _row_copy(y_ref, dest_ref[r * TOP_K + kk], buf_ref.at[kk], r, sem).start(priority=kk % 2)
        return carry

    lax.fori_loop(0, tc, issue, 0, unroll=8)

    def drain(r, carry):
        for kk in range(TOP_K):
            _row_copy(y_ref, 0, buf_ref.at[kk], 0, sem).wait()
        return carry

    lax.fori_loop(0, tc, drain, 0, unroll=8)

    route = route_ref[...]
    acc = x1_ref[...]
    for kk in range(TOP_K):
        acc = acc + route[:, TOP_K + kk:TOP_K + kk + 1] * buf_ref[kk]
    o_ref[...] = acc


def _combine(dest_flat, route, x1, y, tc=256):
    t, d = x1.shape
    return pl.pallas_call(
        _combine_kernel,
        grid=(t // tc,),
        in_specs=[
            pl.BlockSpec((tc * TOP_K,), lambda i: (i,), memory_space=pltpu.SMEM),
            pl.BlockSpec((tc, LANE), lambda i: (i, 0)),
            pl.BlockSpec((tc, d), lambda i: (i, 0)),
            pl.BlockSpec(memory_space=pl.ANY),
        ],
        out_specs=pl.BlockSpec((tc, d), lambda i: (i, 0)),
        out_shape=jax.ShapeDtypeStruct((t, d), F32),
        scratch_shapes=[pltpu.VMEM((TOP_K, tc, d), F32), pltpu.SemaphoreType.DMA],
        compiler_params=_params(("arbitrary",)),
        name="moe_combine",
    )(dest_flat, route, x1, y)


def _moe(h, route, x1, w1, b1g, b1l, w2, b2):
    t, d = h.shape
    n_e = w2.shape[0]
    n_tiles = (t * TOP_K) // MOE_TILE + n_e
    dest, counts = _route(route)
    tiles_per = jnp.ceil(counts[0, :n_e] * (1.0 / MOE_TILE)).astype(jnp.int32)
    tile_end = jnp.cumsum(tiles_per)
    past = (tile_end[None, :] <= jnp.arange(n_tiles, dtype=jnp.int32)[:, None]).astype(jnp.int32)
    tile_expert = jnp.minimum(jnp.sum(past, axis=1), n_e - 1).astype(jnp.int32)
    n_used = tile_end[-1:].astype(jnp.int32)
    dest_flat = dest[:, :TOP_K].reshape(-1)
    xs = _dispatch(dest_flat, h, jnp.zeros((n_tiles * MOE_TILE, d), F32))
    y = _experts(tile_expert, n_used, xs, w1, b1g, b1l, w2, b2)
    return _combine(dest_flat, route, x1, y)


def _pad_cols(w, width):
    return jnp.pad(w, ((0, 0), (0, width - w.shape[1])))


def _rope_tables(pos, reps):
    inv = ROPE_THETA ** (-jnp.arange(0, HEAD_DIM, 2, dtype=F32) / HEAD_DIM)
    ang = pos.astype(F32)[:, None] * inv[None, :]
    ang = jnp.concatenate([ang, ang], axis=-1)
    sign = jnp.concatenate([-jnp.ones((HEAD_DIM // 2,), F32), jnp.ones((HEAD_DIM // 2,), F32)])
    return jnp.tile(jnp.cos(ang), (1, reps)), jnp.tile(jnp.sin(ang) * sign, (1, reps))


def _layer(x, mem, g_mix, g_mem, w_in, b_merge, nsa_q_norm, nsa_k_norm, cmp_pe, cmp_w1, cmp_b1, cmp_w2,
           rwkv_shift_mix, rwkv_w0, rwkv_w_up, rwkv_a0, rwkv_a_up, rwkv_g_up, rwkv_k_k, rwkv_k_a, rwkv_r_k,
           rwkv_ln_w, rwkv_ln_b, mem_w_kv, mem_q_norm, mem_k_norm, w_branch, w_out, g_ffn,
           router_w, router_b, exp_w1, exp_b1, exp_w2, exp_b2, cos, sin, cos_c, sin_c):
    b, s, d = x.shape
    t = b * s
    x2 = x.reshape(t, d)

    o = 0
    parts = []
    for width in (NSA_WIDTH, 6 * NSA_KV_WIDTH, 3 * NSA_HEADS,
                  3 * RWKV_WIDTH + DECAY_RANK + AAA_RANK + GATE_RANK, MEM_WIDTH, N_BRANCHES * D_MODEL):
        parts.append(w_in[:, o:o + width])
        o += width
    w_q, w_kv, w_gate, w_rwkv, w_qm, w_merge = parts

    def rwkv_layout(m):
        r3 = m[:, :3 * RWKV_WIDTH]
        xw = m[:, 3 * RWKV_WIDTH:3 * RWKV_WIDTH + DECAY_RANK]
        xa = m[:, 3 * RWKV_WIDTH + DECAY_RANK:3 * RWKV_WIDTH + DECAY_RANK + AAA_RANK]
        xg = m[:, 3 * RWKV_WIDTH + DECAY_RANK + AAA_RANK:]
        return _pad_cols(jnp.concatenate([r3, _pad_cols(xw, LANE), _pad_cols(xa, LANE), xg], axis=1), RWKV_PAD)

    w_all = jnp.concatenate([rwkv_layout(w_rwkv), w_merge, w_q, w_qm, w_kv, w_gate], axis=1)
    w_all = _pad_cols(w_all, IN_PAD).astype(BF16)
    proj2 = _inproj(x2, g_mix.reshape(1, d), w_all)
    proj3 = proj2.reshape(b, s, IN_PAD)

    rowv = lambda a: a.reshape(1, -1)
    pad_rows = lambda m: jnp.pad(m, ((0, LANE - m.shape[0]), (0, 0)))
    o_b = _rwkv(proj3, rwkv_layout(rowv(rwkv_shift_mix)), rowv(rwkv_w0), pad_rows(rwkv_w_up), rowv(rwkv_a0),
                pad_rows(rwkv_a_up), rwkv_g_up, rowv(rwkv_k_k), rowv(rwkv_k_a), rowv(rwkv_r_k),
                rowv(rwkv_ln_w), rowv(rwkv_ln_b))

    q_gain = jnp.tile(nsa_q_norm.reshape(1, HEAD_DIM), (1, NSA_HEADS))
    k_gain = jnp.tile(nsa_k_norm, (1, NSA_KV_HEADS))
    qn, ks, vs, kw, vw = _nsa_prep(proj3, cos, sin, q_gain, k_gain)
    kc, vc = _nsa_cmp(proj3, cos_c, sin_c, k_gain, cmp_pe, cmp_w1.astype(BF16),
                      cmp_b1.reshape(2, 1, CMP_HIDDEN), cmp_w2.astype(BF16))
    o_a = _nsa_attn(qn, proj3, kc, vc, ks, vs, kw, vw)

    mk, mv = _mem_kv(mem, g_mem.reshape(1, d), mem_w_kv.astype(BF16), mem_k_norm.reshape(1, MEM_HEAD_DIM))
    o_m = _mem_attn(proj3, mem_q_norm.reshape(1, MEM_HEAD_DIM), mk, mv)

    rw = _pad_cols(router_w, LANE)
    rb = jnp.concatenate([router_b, jnp.full((LANE - N_EXPERTS,), NEG_BIG, F32)]).reshape(1, LANE)
    x1, h2, route = _merge(o_a.reshape(t, NSA_WIDTH), o_b.reshape(t, RWKV_WIDTH), o_m.reshape(t, MEM_WIDTH),
                          proj2, x2, b_merge.reshape(1, -1), w_branch.astype(BF16), w_out.astype(BF16),
                          g_ffn.reshape(1, d), rw, rb)

    b1g = exp_b1[:, None, 0::2]
    b1l = exp_b1[:, None, 1::2]
    out = _moe(h2, route, x1, exp_w1, b1g, b1l, exp_w2, exp_b2[:, None, :])
    return out.reshape(b, s, d)


def kernel(x, mem, g_mix, g_mem, w_in, b_merge, nsa_q_norm, nsa_k_norm, cmp_pe, cmp_w1, cmp_b1, cmp_w2,
           rwkv_shift_mix, rwkv_w0, rwkv_w_up, rwkv_a0, rwkv_a_up, rwkv_g_up, rwkv_k_k, rwkv_k_a, rwkv_r_k,
           rwkv_ln_w, rwkv_ln_b, mem_w_kv, mem_q_norm, mem_k_norm, w_branch, w_out, g_ffn,
           router_w, router_b, exp_w1, exp_b1, exp_w2, exp_b2):
    s = x.shape[1]
    cos, sin = _rope_tables(jnp.arange(s), NSA_HEADS)
    n_cmp = (s - CMP_LEN) // CMP_STRIDE + 1
    cos_c, sin_c = _rope_tables(jnp.arange(n_cmp + 1) * CMP_STRIDE + CMP_LEN - 1, NSA_KV_HEADS)
    depth = g_mix.shape[0]
    for l in range(depth):
        x = _layer(x, mem, g_mix[l], g_mem[l], w_in[l], b_merge[l], nsa_q_norm[l], nsa_k_norm[l], cmp_pe[l],
                   cmp_w1[l], cmp_b1[l], cmp_w2[l], rwkv_shift_mix[l], rwkv_w0[l], rwkv_w_up[l], rwkv_a0[l],
                   rwkv_a_up[l], rwkv_g_up[l], rwkv_k_k[l], rwkv_k_a[l], rwkv_r_k[l], rwkv_ln_w[l], rwkv_ln_b[l],
                   mem_w_kv[l], mem_q_norm[l], mem_k_norm[l], w_branch[l], w_out[l], g_ffn[l], router_w[l],
                   router_b[l], exp_w1[l], exp_b1[l], exp_w2[l], exp_b2[l], cos, sin, cos_c, sin_c)
    return x
```

```python
import functools

import jax
import jax.numpy as jnp
from jax import lax
from jax.experimental import pallas as pl
from jax.experimental.pallas import tpu as pltpu

F32 = jnp.float32
BF16 = jnp.bfloat16
HI = lax.Precision.HIGHEST

D_MODEL = 1024
HEAD_DIM = 64
NSA_HEADS = 8
NSA_KV_HEADS = 2
NSA_GROUP = NSA_HEADS // NSA_KV_HEADS
NSA_WIDTH = NSA_HEADS * HEAD_DIM
NSA_KV_WIDTH = NSA_KV_HEADS * HEAD_DIM
CMP_LEN = 32
CMP_STRIDE = 16
CMP_HIDDEN = 128
SEL_BLOCK = 64
TOP_N = 8
WINDOW = 512
Q_BLOCK = 128
FORCE_BONUS = 1000.0
RWKV_HEADS = 8
RWKV_HEAD_DIM = 64
RWKV_WIDTH = RWKV_HEADS * RWKV_HEAD_DIM
DECAY_RANK = 64
AAA_RANK = 64
GATE_RANK = 128
GN_EPS = 64e-5
MEM_HEADS = 4
MEM_HEAD_DIM = 128
MEM_WIDTH = MEM_HEADS * MEM_HEAD_DIM
N_BRANCHES = 3
BRANCH_WIDTH = 512
N_EXPERTS = 32
TOP_K = 4
EXPERT_FF = 1024
SWIGLU_ALPHA = 1.702
SWIGLU_LIMIT = 7.0
ROPE_THETA = 10000.0
RMS_EPS = 1e-6

LANE = 128
NEG_BIG = -1e30

RWKV_PAD = 2048
COL_RWKV = 0
COL_MERGE = COL_RWKV + RWKV_PAD
COL_QNSA = COL_MERGE + N_BRANCHES * D_MODEL
COL_QMEM = COL_QNSA + NSA_WIDTH
COL_KV = COL_QMEM + MEM_WIDTH
COL_GATE = COL_KV + 6 * NSA_KV_WIDTH
IN_PAD = 7168
RW_R, RW_K, RW_V, RW_XW, RW_XA, RW_XG = 0, 512, 1024, 1536, 1664, 1792

RWKV_CHUNK = 64
VMEM_LIMIT = 56 * 1024 * 1024


def _dot(a, b, prec=None):
    return jnp.dot(a, b, preferred_element_type=F32, precision=prec)


def _dot_nt(a, b, prec=None):
    return lax.dot_general(a, b, (((1,), (1,)), ((), ())), preferred_element_type=F32, precision=prec)


def _dot_tn(a, b, prec=None):
    return lax.dot_general(a, b, (((0,), (0,)), ((), ())), preferred_element_type=F32, precision=prec)


def _split3(x):
    hi = x.astype(BF16)
    r1 = x - hi.astype(F32)
    mid = r1.astype(BF16)
    lo = (r1 - mid.astype(F32)).astype(BF16)
    return hi, mid, lo


def _dot01_left(m01, x):
    n = x.shape[1]
    out = _dot(m01, jnp.concatenate(_split3(x), axis=1))
    return out[:, :n] + out[:, n:2 * n] + out[:, 2 * n:]


def _dot01_right(x, m01):
    m = x.shape[0]
    out = _dot(jnp.concatenate(_split3(x), axis=0), m01)
    return out[:m] + out[m:2 * m] + out[2 * m:]


def _seg_matrix(width, seg):
    r = lax.broadcasted_iota(jnp.int32, (width, width), 0) // seg
    c = lax.broadcasted_iota(jnp.int32, (width, width), 1) // seg
    return (r == c).astype(F32)


SUBLANES = 8


def _store_slabs(ref, x, lead=()):
    rows = x.shape[0]
    for s in range(SUBLANES):
        ref[lead + (pl.ds(s, rows, stride=SUBLANES), slice(None))] = x[:, s * LANE:(s + 1) * LANE]


def _load_slabs(ref, rows, lead=()):
    return jnp.concatenate(
        [ref[lead + (pl.ds(s, rows, stride=SUBLANES), slice(None))] for s in range(SUBLANES)], axis=1)


def _params(sem):
    return pltpu.CompilerParams(dimension_semantics=sem, vmem_limit_bytes=VMEM_LIMIT)


def _inproj_kernel(x_ref, g_ref, w_ref, o_ref, hn_ref):
    @pl.when(pl.program_id(1) == 0)
    def _():
        x = x_ref[...]
        ms = jnp.mean(x * x, axis=-1, keepdims=True)
        hn_ref[...] = (x * lax.rsqrt(ms + RMS_EPS) * g_ref[...]).astype(BF16)

    o_ref[...] = _dot(hn_ref[...], w_ref[...])


def _inproj(x2, g, w, tm=1024, tn=1792):
    t, d = x2.shape
    n = w.shape[1]
    return pl.pallas_call(
        _inproj_kernel,
        grid=(t // tm, n // tn),
        in_specs=[
            pl.BlockSpec((tm, d), lambda i, j: (i, 0)),
            pl.BlockSpec((1, d), lambda i, j: (0, 0)),
            pl.BlockSpec((d, tn), lambda i, j: (0, j)),
        ],
        out_specs=pl.BlockSpec((tm, tn), lambda i, j: (i, j)),
        out_shape=jax.ShapeDtypeStruct((t, n), F32),
        scratch_shapes=[pltpu.VMEM((tm, d), BF16)],
        compiler_params=_params(("parallel", "arbitrary")),
        name="inproj",
    )(x2, g, w)


def _rwkv_chunk_kernel(p_ref, pprev_ref, mix_ref, w0_ref, wup_ref, a0_ref, aup_ref, gup_ref, kk_ref, ka_ref,
                       rk_ref, rm_ref, y0_ref, bonus_ref, g_ref, gam_ref, m_ref, d0_ref):
    c = RWKV_CHUNK
    n = RWKV_HEAD_DIM
    rows = p_ref.shape[1]

    p = p_ref[0]
    row = lax.broadcasted_iota(jnp.int32, p.shape, 0)
    last_prev = jnp.where(pl.program_id(1) == 0, 0.0, 1.0) * pprev_ref[0, 7:8, :]
    prev = jnp.where(row == 0, last_prev, pltpu.roll(p, 1, axis=0))
    ps = p + (prev - p) * mix_ref[...]
    r = ps[:, RW_R:RW_R + RWKV_WIDTH]
    k = ps[:, RW_K:RW_K + RWKV_WIDTH]
    v = ps[:, RW_V:RW_V + RWKV_WIDTH]
    xw = ps[:, RW_XW:RW_XW + LANE]
    xa = ps[:, RW_XA:RW_XA + LANE]
    xg = ps[:, RW_XG:RW_XG + LANE]

    z = -(w0_ref[...] + _dot(jnp.tanh(xw), wup_ref[...], HI))
    softplus = jnp.maximum(z, 0.0) + jnp.log1p(jnp.exp(-jnp.abs(z)))
    w = -softplus - 0.5
    logw = -jnp.exp(w)
    a = jax.nn.sigmoid(a0_ref[...] + _dot(xa, aup_ref[...], HI))
    g_ref[0] = _dot(jax.nn.sigmoid(xg), gup_ref[...], HI)

    seg = _seg_matrix(RWKV_WIDTH, n).astype(BF16)
    kk = k * kk_ref[...]
    k = k * (1.0 + (a - 1.0) * ka_ref[...])
    sums = _dot01_right(jnp.concatenate([kk * kk, r * k * rk_ref[...]], axis=0), seg)
    kk = kk / jnp.maximum(jnp.sqrt(sums[:rows]), 1e-12)
    bonus_ref[0] = sums[rows:] * v
    kka = kk * a

    ti = lax.broadcasted_iota(jnp.int32, (2 * c, 2 * c), 0)
    tj = lax.broadcasted_iota(jnp.int32, (2 * c, 2 * c), 1)
    keep = (tj % c) < jnp.where(ti < c, ti, ti - c + 1)
    ci = lax.broadcasted_iota(jnp.int32, (c, c), 0)
    cj = lax.broadcasted_iota(jnp.int32, (c, c), 1)
    eye = (ci == cj).astype(F32)
    ltri = (cj <= ci).astype(BF16)
    zeros_cn = jnp.zeros((c, n), BF16)

    chains = []
    for j in range(rows // c):
        rs = slice(j * c, (j + 1) * c)
        lw = logw[rs]
        cum = _dot01_left(ltri, lw)
        tot = cum[c - 1:c, :]
        einv = jnp.exp(-cum)
        dec_end = jnp.exp(tot - cum)
        r_f = r[rs] * jnp.exp(cum)
        a_t = (-kk[rs] * jnp.exp(cum - lw)).astype(BF16)
        b_t = (kka[rs] * einv).astype(BF16)
        k_t = (k[rs] * einv).astype(BF16)
        r_t = r_f.astype(BF16)
        b_e = (kka[rs] * dec_end).astype(BF16)
        k_e = (k[rs] * dec_end).astype(BF16)
        v_b = v[rs].astype(BF16)
        gam_ref[0, j] = jnp.exp(tot)
        for h in range(RWKV_HEADS):
            sl = slice(h * n, (h + 1) * n)
            chains.append(dict(j=j, h=h, rs=rs, sl=sl, a=a_t[:, sl], r=r_t[:, sl], rf=r_f[:, sl], v=v_b[:, sl],
                               rhs=jnp.concatenate([b_t[:, sl], k_t[:, sl]], axis=0),
                               bke=jnp.concatenate([b_e[:, sl], k_e[:, sl]], axis=0)))

    for ch in chains:
        lhs = jnp.concatenate([ch["a"], ch["r"]], axis=0)
        ch["amat"] = jnp.where(keep, _dot_nt(lhs, ch["rhs"]), 0.0)
        ch["pw"] = ch["amat"][:c, :c]
        ch["tinv"] = eye + ch["pw"]
    for _ in range(5):
        for ch in chains:
            pw_b = ch["pw"].astype(BF16)
            ch["pw"] = _dot(pw_b, pw_b)
        for ch in chains:
            ch["tinv"] = ch["tinv"] + _dot(ch["tinv"].astype(BF16), ch["pw"].astype(BF16))
    for ch in chains:
        ch["akv"] = _dot(ch["amat"][:c, c:].astype(BF16), ch["v"])
    for ch in chains:
        wu = _dot(ch["tinv"].astype(BF16), jnp.concatenate([ch["a"], ch["akv"].astype(BF16)], axis=1)).astype(BF16)
        ch["x"] = jnp.concatenate([wu, jnp.concatenate([zeros_cn, ch["v"]], axis=1)], axis=0)
    for ch in chains:
        ry = _dot(ch["amat"][c:, :].astype(BF16), ch["x"])
        rm_ref[0, ch["rs"], ch["sl"]] = (ch["rf"] + ry[:, :n]).astype(BF16)
        y0_ref[0, ch["rs"], ch["sl"]] = ry[:, n:]
    for ch in chains:
        md = _dot_tn(ch["x"], ch["bke"])
        m_ref[0, ch["j"], ch["h"]] = md[:n].astype(BF16)
        d0_ref[0, ch["j"], ch["h"]] = md[n:]


def _rwkv_chunks(proj3, mix, w0, wup, a0, aup, gup, k_k, k_a, r_k, rows=256):
    b, s, _ = proj3.shape
    c = RWKV_CHUNK
    nc = s // c
    cps = rows // c
    vec = lambda width: pl.BlockSpec((1, width), lambda i, t: (0, 0))
    mat = lambda nrows: pl.BlockSpec((nrows, RWKV_WIDTH), lambda i, t: (0, 0))
    tok = pl.BlockSpec((1, rows, RWKV_WIDTH), lambda i, t: (i, t, 0))
    sq = pl.BlockSpec((1, cps, RWKV_HEADS, RWKV_HEAD_DIM, RWKV_HEAD_DIM), lambda i, t: (i, t, 0, 0, 0))
    tok_shape = lambda dt: jax.ShapeDtypeStruct((b, s, RWKV_WIDTH), dt)
    sq_shape = lambda dt: jax.ShapeDtypeStruct((b, nc, RWKV_HEADS, RWKV_HEAD_DIM, RWKV_HEAD_DIM), dt)
    return pl.pallas_call(
        _rwkv_chunk_kernel,
        grid=(b, s // rows),
        in_specs=[
            pl.BlockSpec((1, rows, RWKV_PAD), lambda i, t: (i, t, COL_RWKV // RWKV_PAD)),
            pl.BlockSpec((1, 8, RWKV_PAD), lambda i, t: (i, jnp.maximum(t * (rows // 8) - 1, 0), COL_RWKV // RWKV_PAD)),
            vec(RWKV_PAD), vec(RWKV_WIDTH), mat(LANE), vec(RWKV_WIDTH), mat(LANE), mat(LANE),
            vec(RWKV_WIDTH), vec(RWKV_WIDTH), vec(RWKV_WIDTH),
        ],
        out_specs=[tok, tok, tok, tok,
                   pl.BlockSpec((1, cps, 1, RWKV_WIDTH), lambda i, t: (i, t, 0, 0)), sq, sq],
        out_shape=[tok_shape(BF16), tok_shape(F32), tok_shape(F32), tok_shape(F32),
                   jax.ShapeDtypeStruct((b, nc, 1, RWKV_WIDTH), F32), sq_shape(BF16), sq_shape(F32)],
        compiler_params=_params(("parallel", "parallel")),
        name="rwkv7_chunks",
    )(proj3, proj3, mix, w0, wup, a0, aup, gup, k_k, k_a, r_k)


def _rwkv_scan_kernel(rm_ref, y0_ref, bonus_ref, g_ref, gam_ref, m_ref, d0_ref, lnw_ref, lnb_ref,
                      o_ref, state_ref, y_ref):
    c = RWKV_CHUNK
    n = RWKV_HEAD_DIM

    @pl.when(pl.program_id(1) == 0)
    def _():
        state_ref[...] = jnp.zeros_like(state_ref)

    for j in range(gam_ref.shape[1]):
        rs = slice(j * c, (j + 1) * c)
        gam = gam_ref[0, j]
        for h in range(RWKV_HEADS):
            sl = slice(h * n, (h + 1) * n)
            s = state_ref[h]
            s_b = s.astype(BF16)
            y_ref[rs, sl] = _dot_nt(rm_ref[0, rs, sl], s_b) + y0_ref[0, rs, sl]
            state_ref[h] = s * gam[:, sl] + _dot(s_b, m_ref[0, j, h]) + d0_ref[0, j, h]

    seg = _seg_matrix(RWKV_WIDTH, n).astype(BF16)
    y = y_ref[...]
    mu = _dot01_right(y, seg) * (1.0 / n)
    d = y - mu
    var = _dot01_right(d * d, seg) * (1.0 / n)
    yn = d * lax.rsqrt(var + GN_EPS) * lnw_ref[...] + lnb_ref[...]
    o_ref[0] = (yn + bonus_ref[0]) * g_ref[0]


def _rwkv_scan(rm, y0, bonus, g, gam, m, d0, ln_w, ln_b, rows=256):
    b, s, _ = rm.shape
    cps = rows // RWKV_CHUNK
    vec = pl.BlockSpec((1, RWKV_WIDTH), lambda i, t: (0, 0))
    tok = pl.BlockSpec((1, rows, RWKV_WIDTH), lambda i, t: (i, t, 0))
    sq = pl.BlockSpec((1, cps, RWKV_HEADS, RWKV_HEAD_DIM, RWKV_HEAD_DIM), lambda i, t: (i, t, 0, 0, 0))
    return pl.pallas_call(
        _rwkv_scan_kernel,
        grid=(b, s // rows),
        in_specs=[tok, tok, tok, tok, pl.BlockSpec((1, cps, 1, RWKV_WIDTH), lambda i, t: (i, t, 0, 0)), sq, sq,
                  vec, vec],
        out_specs=tok,
        out_shape=jax.ShapeDtypeStruct((b, s, RWKV_WIDTH), F32),
        scratch_shapes=[
            pltpu.VMEM((RWKV_HEADS, RWKV_HEAD_DIM, RWKV_HEAD_DIM), F32),
            pltpu.VMEM((rows, RWKV_WIDTH), F32),
        ],
        compiler_params=_params(("parallel", "arbitrary")),
        name="rwkv7_scan",
    )(rm, y0, bonus, g, gam, m, d0, ln_w, ln_b)


def _rwkv(proj3, mix, w0, wup, a0, aup, gup, k_k, k_a, r_k, ln_w, ln_b):
    rm, y0, bonus, g, gam, m, d0 = _rwkv_chunks(proj3, mix, w0, wup, a0, aup, gup, k_k, k_a, r_k)
    return _rwkv_scan(rm, y0, bonus, g, gam, m, d0, ln_w, ln_b)


def _rope(x, cos, sin_signed):
    w = x.shape[-1]
    first_half = (lax.broadcasted_iota(jnp.int32, x.shape, 1) % HEAD_DIM) < (HEAD_DIM // 2)
    rot = jnp.where(first_half, pltpu.roll(x, w - HEAD_DIM // 2, axis=1), pltpu.roll(x, HEAD_DIM // 2, axis=1))
    return x * cos + rot * sin_signed


def _head_rmsnorm(x, gain, seg):
    ms = _dot(x * x, seg, HI) * (1.0 / HEAD_DIM)
    return x * lax.rsqrt(ms + RMS_EPS) * gain


def _split_groups(x):
    return [x[:, g * HEAD_DIM:(g + 1) * HEAD_DIM] for g in range(NSA_KV_HEADS)]


def _nsa_prep_kernel(q_ref, ksl_ref, vsl_ref, kwn_ref, vwn_ref, cos_ref, sin_ref, qn_ref, kn_ref,
                     qo_ref, ks_ref, vs_ref, kw_ref, vw_ref):
    seg_q = _seg_matrix(NSA_WIDTH, HEAD_DIM)
    seg_k = _seg_matrix(NSA_KV_WIDTH, HEAD_DIM)
    cos_k = cos_ref[:, :NSA_KV_WIDTH]
    sin_k = sin_ref[:, :NSA_KV_WIDTH]

    q = _rope(_head_rmsnorm(q_ref[0], qn_ref[...], seg_q), cos_ref[...], sin_ref[...])
    q_t = (q * (HEAD_DIM ** -0.5)).T
    for h in range(NSA_HEADS):
        qo_ref[0, h] = q_t[h * HEAD_DIM:(h + 1) * HEAD_DIM].astype(BF16)

    ks = _rope(_head_rmsnorm(ksl_ref[0], kn_ref[1:2, :], seg_k), cos_k, sin_k)
    kw = _rope(_head_rmsnorm(kwn_ref[0], kn_ref[2:3, :], seg_k), cos_k, sin_k)
    vs_t = vsl_ref[0].T
    vw_t = vwn_ref[0].T
    for g, (a, c_) in enumerate(zip(_split_groups(ks), _split_groups(kw))):
        ks_ref[0, g] = a.astype(BF16)
        kw_ref[0, g] = c_.astype(BF16)
        vs_ref[0, g] = vs_t[g * HEAD_DIM:(g + 1) * HEAD_DIM].astype(BF16)
        vw_ref[0, g] = vw_t[g * HEAD_DIM:(g + 1) * HEAD_DIM].astype(BF16)


def _nsa_prep(proj3, cos, sin_signed, q_norm, k_norm, tq=512):
    b, s, _ = proj3.shape
    kvw = NSA_KV_WIDTH
    kv_spec = lambda j: pl.BlockSpec((1, tq, kvw), lambda i, t: (i, t, COL_KV // kvw + j))
    out_k = pl.BlockSpec((1, NSA_KV_HEADS, tq, HEAD_DIM), lambda i, t: (i, 0, t, 0))
    out_vt = pl.BlockSpec((1, NSA_KV_HEADS, HEAD_DIM, tq), lambda i, t: (i, 0, 0, t))
    k_shape = jax.ShapeDtypeStruct((b, NSA_KV_HEADS, s, HEAD_DIM), BF16)
    vt_shape = jax.ShapeDtypeStruct((b, NSA_KV_HEADS, HEAD_DIM, s), BF16)
    return pl.pallas_call(
        _nsa_prep_kernel,
        grid=(b, s // tq),
        in_specs=[
            pl.BlockSpec((1, tq, NSA_WIDTH), lambda i, t: (i, t, COL_QNSA // NSA_WIDTH)),
            kv_spec(2), kv_spec(3), kv_spec(4), kv_spec(5),
            pl.BlockSpec((tq, NSA_WIDTH), lambda i, t: (t, 0)),
            pl.BlockSpec((tq, NSA_WIDTH), lambda i, t: (t, 0)),
            pl.BlockSpec((1, NSA_WIDTH), lambda i, t: (0, 0)),
            pl.BlockSpec((3, kvw), lambda i, t: (0, 0)),
        ],
        out_specs=[
            pl.BlockSpec((1, NSA_HEADS, HEAD_DIM, tq), lambda i, t: (i, 0, 0, t)),
            out_k, out_vt, out_k, out_vt,
        ],
        out_shape=[jax.ShapeDtypeStruct((b, NSA_HEADS, HEAD_DIM, s), BF16), k_shape, vt_shape, k_shape, vt_shape],
        compiler_params=_params(("parallel", "parallel")),
        name="nsa_prep",
    )(proj3, proj3, proj3, proj3, proj3, cos, sin_signed, q_norm, k_norm)


def _gelu_tanh(x):
    return 0.5 * x * (1.0 + jnp.tanh(0.7978845608028654 * (x + 0.044715 * x * x * x)))


def _nsa_cmp_kernel(kc_in_ref, vc_in_ref, cos_ref, sin_ref, kn_ref, pe_ref, w1_ref, b1_ref, w2_ref,
                    kc_ref, vc_ref):
    n_cmp = (kc_in_ref.shape[1] - CMP_LEN) // CMP_STRIDE + 1
    n_pad = n_cmp + 1
    zero_row = jnp.zeros((1, NSA_KV_WIDTH), F32)
    outs = []
    for j, src in enumerate((kc_in_ref, vc_in_ref)):
        acc = jnp.zeros((NSA_KV_HEADS * n_pad, CMP_HIDDEN), F32)
        for l in range(CMP_LEN):
            x = src[0, pl.ds(l, n_cmp, stride=CMP_STRIDE), :]
            x = jnp.concatenate([x, zero_row], axis=0)
            xg = jnp.concatenate(_split_groups(x), axis=0) + pe_ref[j, l:l + 1, :]
            acc = acc + _dot(xg.astype(BF16), w1_ref[j, l * HEAD_DIM:(l + 1) * HEAD_DIM, :])
        hid = _gelu_tanh(acc + b1_ref[j])
        out = _dot(hid.astype(BF16), w2_ref[j])
        outs.append(jnp.concatenate([out[g * n_pad:(g + 1) * n_pad] for g in range(NSA_KV_HEADS)], axis=1))
    kc, vc = outs
    seg_k = _seg_matrix(NSA_KV_WIDTH, HEAD_DIM)
    kc = _rope(_head_rmsnorm(kc, kn_ref[0:1, :], seg_k), cos_ref[...], sin_ref[...])
    vc_t = vc.T
    for g, a in enumerate(_split_groups(kc)):
        kc_ref[0, g] = a.astype(BF16)
        vc_ref[0, g] = vc_t[g * HEAD_DIM:(g + 1) * HEAD_DIM].astype(BF16)


def _nsa_cmp(proj3, cos_c, sin_c, k_norm, pe, w1, b1, w2):
    b, s, _ = proj3.shape
    kvw = NSA_KV_WIDTH
    n_pad = (s - CMP_LEN) // CMP_STRIDE + 2
    full = lambda shape: pl.BlockSpec(shape, lambda i: (0,) * len(shape))
    k_spec = pl.BlockSpec((1, NSA_KV_HEADS, n_pad, HEAD_DIM), lambda i: (i, 0, 0, 0))
    vt_spec = pl.BlockSpec((1, NSA_KV_HEADS, HEAD_DIM, n_pad), lambda i: (i, 0, 0, 0))
    k_shape = jax.ShapeDtypeStruct((b, NSA_KV_HEADS, n_pad, HEAD_DIM), BF16)
    vt_shape = jax.ShapeDtypeStruct((b, NSA_KV_HEADS, HEAD_DIM, n_pad), BF16)
    return pl.pallas_call(
        _nsa_cmp_kernel,
        grid=(b,),
        in_specs=[
            pl.BlockSpec((1, s, kvw), lambda i: (i, 0, COL_KV // kvw)),
            pl.BlockSpec((1, s, kvw), lambda i: (i, 0, COL_KV // kvw + 1)),
            full((n_pad, kvw)), full((n_pad, kvw)), full((3, kvw)),
            full((2, CMP_LEN, HEAD_DIM)), full((2, CMP_LEN * HEAD_DIM, CMP_HIDDEN)),
            full((2, 1, CMP_HIDDEN)), full((2, CMP_HIDDEN, HEAD_DIM)),
        ],
        out_specs=[k_spec, vt_spec],
        out_shape=[k_shape, vt_shape],
        compiler_params=_params(("parallel",)),
        name="nsa_compress",
    )(proj3, proj3, cos_c, sin_c, k_norm, pe, w1, b1, w2)


def _nsa_attn_kernel(q_ref, gate_ref, kc_ref, vc_ref, ks_ref, vs_ref, kw_ref, vw_ref, o_ref):
    qb = Q_BLOCK
    hg = NSA_GROUP
    g = pl.program_id(1)
    c = pl.program_id(2)
    cols = hg * qb
    qt = jnp.concatenate([q_ref[0, h] for h in range(hg)], axis=1)
    tile_heads = lambda x: jnp.concatenate([x] * hg, axis=1)

    n_cp = kc_ref.shape[2]
    s_c = _dot(kc_ref[0, 0], qt)
    cend = lax.broadcasted_iota(jnp.int32, (n_cp, qb), 0) * CMP_STRIDE + CMP_LEN - 1
    cmask = tile_heads(jnp.where(cend <= c * qb + lax.broadcasted_iota(jnp.int32, (n_cp, qb), 1), 1, 0)) > 0
    s_m = jnp.where(cmask, s_c, -jnp.inf)
    m = jnp.max(s_m, axis=0, keepdims=True)
    m = jnp.where(m == -jnp.inf, 0.0, m)
    e = jnp.where(cmask, jnp.exp(s_m - m), 0.0)
    p_c = e / jnp.maximum(jnp.sum(e, axis=0, keepdims=True), 1e-30)
    o_cmp = _dot(vc_ref[0, 0], p_c.astype(BF16))

    n_blk = ks_ref.shape[2] // SEL_BLOCK
    p_sum = p_c[:, 0:qb]
    for h in range(1, hg):
        p_sum = p_sum + p_c[:, h * qb:(h + 1) * qb]
    bi = lax.broadcasted_iota(jnp.int32, (n_blk, n_cp), 0) * SEL_BLOCK
    ci = lax.broadcasted_iota(jnp.int32, (n_blk, n_cp), 1) * CMP_STRIDE
    cover_t = ((ci < bi + SEL_BLOCK) & (ci + CMP_LEN > bi)).astype(F32)
    imp = _dot(cover_t, p_sum, HI)
    blk = lax.broadcasted_iota(jnp.int32, (n_blk, qb), 0)
    qpos = c * qb + lax.broadcasted_iota(jnp.int32, (n_blk, qb), 1)
    cur = qpos // SEL_BLOCK
    forced = (blk == 0) | (blk == cur) | (blk == cur - 1)
    imp = jnp.where(forced, imp + FORCE_BONUS, imp)
    imp = jnp.where(blk <= cur, imp, -jnp.inf)
    rank = jnp.zeros((n_blk, qb), F32)
    for j in range(n_blk):
        other = imp[j:j + 1, :]
        rank = rank + jnp.where(blk > j, jnp.where(other >= imp, 1.0, 0.0), jnp.where(other > imp, 1.0, 0.0))
    sel_t = jnp.where(rank < min(TOP_N, n_blk), 1.0, 0.0).astype(BF16)

    kt_w = 4 * LANE
    kq = c * qb + lax.broadcasted_iota(jnp.int32, (kt_w, qb), 1)
    krel = lax.broadcasted_iota(jnp.int32, (kt_w, qb), 0)
    er = lax.broadcasted_iota(jnp.int32, (kt_w, n_blk), 0) // SEL_BLOCK
    ec = lax.broadcasted_iota(jnp.int32, (kt_w, n_blk), 1)

    def sel_body(kt, carry):
        m_i, l_i, acc = carry
        off = pl.multiple_of(kt * kt_w, kt_w)
        expand = jnp.where(ec == er + kt * (kt_w // SEL_BLOCK), 1.0, 0.0).astype(BF16)
        in_sel = _dot(expand, sel_t)
        bias = jnp.where((in_sel > 0.5) & (krel + off <= kq), 0.0, NEG_BIG)
        s = _dot(ks_ref[0, 0, pl.ds(off, kt_w), :], qt) + tile_heads(bias)
        m_new = jnp.maximum(m_i, jnp.max(s, axis=0, keepdims=True))
        alpha = jnp.exp(m_i - m_new)
        p = jnp.exp(s - m_new)
        l_new = alpha * l_i + jnp.sum(p, axis=0, keepdims=True)
        acc = alpha * acc + _dot(vs_ref[0, 0, :, pl.ds(off, kt_w)], p.astype(BF16))
        return m_new, l_new, acc

    init = (jnp.full((1, cols), NEG_BIG, F32), jnp.zeros((1, cols), F32), jnp.zeros((HEAD_DIM, cols), F32))
    _, l_s, acc_s = lax.fori_loop(0, (c * qb + qb + kt_w - 1) // kt_w, sel_body, init)
    o_slc = acc_s / l_s

    span = WINDOW + 2 * qb
    w0 = pl.multiple_of(jnp.clip(c * qb - WINDOW, 0, kw_ref.shape[2] - span), qb)
    kpos = w0 + lax.broadcasted_iota(jnp.int32, (span, qb), 0)
    wq = c * qb + lax.broadcasted_iota(jnp.int32, (span, qb), 1)
    wbias = jnp.where((kpos <= wq) & (kpos > wq - WINDOW), 0.0, NEG_BIG)
    s_w = _dot(kw_ref[0, 0, pl.ds(w0, span), :], qt) + tile_heads(wbias)
    e_w = jnp.exp(s_w - jnp.max(s_w, axis=0, keepdims=True))
    o_win = _dot(vw_ref[0, 0, :, pl.ds(w0, span)], e_w.astype(BF16)) / jnp.sum(e_w, axis=0, keepdims=True)

    gates_t = jax.nn.sigmoid(gate_ref[0]).T
    grow = lax.broadcasted_iota(jnp.int32, gates_t.shape, 0)
    outs = []
    for h in range(hg):
        hc = slice(h * qb, (h + 1) * qb)
        first = (g * hg + h) * 3
        pick = lambda j: jnp.sum(jnp.where(grow == first + j, gates_t, 0.0), axis=0, keepdims=True)
        outs.append(pick(0) * o_cmp[:, hc] + pick(1) * o_slc[:, hc] + pick(2) * o_win[:, hc])
    o_ref[0] = jnp.concatenate(outs, axis=0).T


def _nsa_attn(q, proj3, kc, vc, ks, vs, kw, vw):
    b, _, _, s = q.shape
    gw = NSA_GROUP * HEAD_DIM
    n_pad = kc.shape[2]
    per_group = lambda rows, width: pl.BlockSpec((1, 1, rows, width), lambda i, g, c: (i, g, 0, 0))
    return pl.pallas_call(
        _nsa_attn_kernel,
        grid=(b, NSA_KV_HEADS, s // Q_BLOCK),
        in_specs=[
            pl.BlockSpec((1, NSA_GROUP, HEAD_DIM, Q_BLOCK), lambda i, g, c: (i, g, 0, c)),
            pl.BlockSpec((1, Q_BLOCK, LANE), lambda i, g, c: (i, c, COL_GATE // LANE)),
            per_group(n_pad, HEAD_DIM), per_group(HEAD_DIM, n_pad),
            per_group(s, HEAD_DIM), per_group(HEAD_DIM, s), per_group(s, HEAD_DIM), per_group(HEAD_DIM, s),
        ],
        out_specs=pl.BlockSpec((1, Q_BLOCK, gw), lambda i, g, c: (i, c, g)),
        out_shape=jax.ShapeDtypeStruct((b, s, NSA_WIDTH), F32),
        compiler_params=_params(("parallel", "parallel", "arbitrary")),
        name="nsa_attn",
    )(q, proj3, kc, vc, ks, vs, kw, vw)


def _mem_kv_kernel(mem_ref, g_ref, w_ref, kn_ref, k_ref, v_ref):
    x = mem_ref[0]
    ms = jnp.mean(x * x, axis=-1, keepdims=True)
    xn = (x * lax.rsqrt(ms + RMS_EPS) * g_ref[...]).astype(BF16)
    kv = _dot(xn, w_ref[...])
    for h in range(MEM_HEADS):
        sl = slice(h * MEM_HEAD_DIM, (h + 1) * MEM_HEAD_DIM)
        kh = kv[:, sl]
        kms = jnp.mean(kh * kh, axis=-1, keepdims=True)
        k_ref[0, :, sl] = (kh * lax.rsqrt(kms + RMS_EPS) * kn_ref[...]).astype(BF16)
    v_ref[0] = kv[:, MEM_WIDTH:].astype(BF16)


def _mem_kv(mem, g_mem, w_kv, k_norm):
    b, m, d = mem.shape
    spec = pl.BlockSpec((1, m, MEM_WIDTH), lambda i: (i, 0, 0))
    shape = jax.ShapeDtypeStruct((b, m, MEM_WIDTH), BF16)
    return pl.pallas_call(
        _mem_kv_kernel,
        grid=(b,),
        in_specs=[
            pl.BlockSpec((1, m, d), lambda i: (i, 0, 0)),
            pl.BlockSpec((1, d), lambda i: (0, 0)),
            pl.BlockSpec((d, 2 * MEM_WIDTH), lambda i: (0, 0)),
            pl.BlockSpec((1, MEM_HEAD_DIM), lambda i: (0, 0)),
        ],
        out_specs=[spec, spec],
        out_shape=[shape, shape],
        compiler_params=_params(("parallel",)),
        name="mem_kv",
    )(mem, g_mem, w_kv, k_norm)


def _mem_attn_kernel(q_ref, qn_ref, k_ref, v_ref, o_ref):
    q = q_ref[0]
    for h in range(MEM_HEADS):
        sl = slice(h * MEM_HEAD_DIM, (h + 1) * MEM_HEAD_DIM)
        qh = q[:, sl]
        ms = jnp.mean(qh * qh, axis=-1, keepdims=True)
        qh = (qh * lax.rsqrt(ms + RMS_EPS) * qn_ref[...]).astype(BF16)
        s = _dot_nt(qh, k_ref[0, :, sl]) * (MEM_HEAD_DIM ** -0.5)
        m = jnp.max(s, axis=-1, keepdims=True)
        e = jnp.exp(s - m)
        p = e / jnp.sum(e, axis=-1, keepdims=True)
        o_ref[0, :, sl] = _dot(p.astype(BF16), v_ref[0, :, sl])


def _mem_attn(proj3, q_norm, k, v, tq=512):
    b, s, _ = proj3.shape
    m = k.shape[1]
    kv_spec = pl.BlockSpec((1, m, MEM_WIDTH), lambda i, t: (i, 0, 0))
    return pl.pallas_call(
        _mem_attn_kernel,
        grid=(b, s // tq),
        in_specs=[
            pl.BlockSpec((1, tq, MEM_WIDTH), lambda i, t: (i, t, COL_QMEM // MEM_WIDTH)),
            pl.BlockSpec((1, MEM_HEAD_DIM), lambda i, t: (0, 0)),
            kv_spec, kv_spec,
        ],
        out_specs=pl.BlockSpec((1, tq, MEM_WIDTH), lambda i, t: (i, t, 0)),
        out_shape=jax.ShapeDtypeStruct((b, s, MEM_WIDTH), F32),
        compiler_params=_params(("parallel", "parallel")),
        name="mem_attn",
    )(proj3, q_norm, k, v)


def _merge_kernel(oa_ref, ob_ref, om_ref, l0_ref, l1_ref, l2_ref, x_ref, bm_ref, wb_ref, wo_ref,
                  gf_ref, rw_ref, rb_ref, x1_ref, h_ref, route_ref):
    mixed = None
    for n, (o_ref, l_ref) in enumerate(((oa_ref, l0_ref), (ob_ref, l1_ref), (om_ref, l2_ref))):
        gate = jax.nn.sigmoid(l_ref[...] + bm_ref[:, n * D_MODEL:(n + 1) * D_MODEL])
        term = gate * _dot(o_ref[...].astype(BF16), wb_ref[n])
        mixed = term if mixed is None else mixed + term
    x1 = x_ref[...] + _dot(mixed.astype(BF16), wo_ref[...])
    x1_ref[...] = x1
    ms = jnp.mean(x1 * x1, axis=-1, keepdims=True)
    h = x1 * lax.rsqrt(ms + RMS_EPS) * gf_ref[...]
    _store_slabs(h_ref, h)

    logits = _dot(h, rw_ref[...], HI) + rb_ref[...]
    lane_f = lax.broadcasted_iota(jnp.int32, logits.shape, 1).astype(F32)
    work = logits
    picks = []
    for _ in range(TOP_K):
        mx = jnp.max(work, axis=-1, keepdims=True)
        idx = jnp.min(jnp.where(work == mx, lane_f, 2.0 * LANE), axis=-1, keepdims=True)
        picks.append((idx, mx))
        work = jnp.where(lane_f == idx, -jnp.inf, work)
    exps = [jnp.exp(mx - picks[0][1]) for _, mx in picks]
    denom = functools.reduce(lambda a, b: a + b, exps)
    route = jnp.zeros(logits.shape, F32)
    for kk, ((idx, _), ex) in enumerate(zip(picks, exps)):
        route = jnp.where(lane_f == kk, idx, route)
        route = jnp.where(lane_f == TOP_K + kk, ex / denom, route)
    route_ref[...] = route


def _merge(o_a, o_b, o_m, proj2, x2, b_merge, w_branch, w_out, g_ffn, router_w, router_b, tm=512):
    t, d = x2.shape
    row = lambda width: pl.BlockSpec((tm, width), lambda i: (i, 0))
    logit = lambda n: pl.BlockSpec((tm, d), lambda i: (i, COL_MERGE // d + n))
    full = lambda shape: pl.BlockSpec(shape, lambda i: (0,) * len(shape))
    return pl.pallas_call(
        _merge_kernel,
        grid=(t // tm,),
        in_specs=[
            row(BRANCH_WIDTH), row(BRANCH_WIDTH), row(BRANCH_WIDTH), logit(0), logit(1), logit(2), row(d),
            full((1, N_BRANCHES * d)), full((N_BRANCHES, BRANCH_WIDTH, d)), full((d, d)), full((1, d)),
            full((d, LANE)), full((1, LANE)),
        ],
        out_specs=[row(d), pl.BlockSpec((tm * SUBLANES, LANE), lambda i: (i, 0)), row(LANE)],
        out_shape=[jax.ShapeDtypeStruct((t, d), F32), jax.ShapeDtypeStruct((t * SUBLANES, LANE), F32),
                   jax.ShapeDtypeStruct((t, LANE), F32)],
        compiler_params=_params(("parallel",)),
        name="merge_router",
    )(o_a, o_b, o_m, proj2, proj2, proj2, x2, b_merge, w_branch, w_out, g_ffn, router_w, router_b)


def _deinterleave(x):
    rows, width = x.shape
    lane = lax.broadcasted_iota(jnp.int32, (rows, LANE), 1)
    half = LANE // 2
    low = lane < half
    idx = jnp.where(low, 2 * lane, 2 * (lane - half) + 1)
    evens, odds = [], []
    for j in range(0, width, 2 * LANE):
        a = jnp.take_along_axis(x[:, j:j + LANE], idx, axis=1)
        b = jnp.take_along_axis(x[:, j + LANE:j + 2 * LANE], idx, axis=1)
        evens.append(jnp.where(low, a, pltpu.roll(b, half, axis=1)))
        odds.append(jnp.where(low, pltpu.roll(a, half, axis=1), b))
    return jnp.concatenate(evens, axis=1), jnp.concatenate(odds, axis=1)


MOE_TILE = 512


def _route_kernel(route_ref, dest_ref, counts_ref, carry_ref, start_ref):
    phase = pl.program_id(0)
    i = pl.program_id(1)
    tr = route_ref.shape[0]
    route = route_ref[...]
    lane = lax.broadcasted_iota(jnp.int32, route.shape, 1)
    lane_f = lane.astype(F32)
    hits = [lane_f == route[:, kk:kk + 1] for kk in range(TOP_K)]
    sel = jnp.zeros(route.shape, F32)
    for hit in hits:
        sel = jnp.where(hit, 1.0, sel)

    @pl.when((phase == 0) & (i == 0))
    def _():
        carry_ref[...] = jnp.zeros_like(carry_ref)

    @pl.when((phase == 1) & (i == 0))
    def _():
        counts = carry_ref[...]
        counts_ref[...] = counts
        padded = jnp.ceil(counts * (1.0 / MOE_TILE)) * MOE_TILE
        ui = lax.broadcasted_iota(jnp.int32, (LANE, LANE), 0)
        uj = lax.broadcasted_iota(jnp.int32, (LANE, LANE), 1)
        start_ref[...] = _dot01_right(jnp.broadcast_to(padded, (8, LANE)), (ui < uj).astype(BF16))[0:1]
        carry_ref[...] = jnp.zeros_like(carry_ref)

    @pl.when(phase == 1)
    def _():
        ri = lax.broadcasted_iota(jnp.int32, (tr, tr), 0)
        rj = lax.broadcasted_iota(jnp.int32, (tr, tr), 1)
        rank = carry_ref[...] + _dot((rj < ri).astype(BF16), sel.astype(BF16))
        row = start_ref[...] + rank
        dest = jnp.zeros(route.shape, F32)
        for kk, hit in enumerate(hits):
            dest = jnp.where(lane == kk, jnp.sum(jnp.where(hit, row, 0.0), axis=-1, keepdims=True), dest)
        dest_ref[...] = dest.astype(jnp.int32)

    carry_ref[...] += jnp.sum(sel, axis=0, keepdims=True)


def _route(route, tr=512):
    t = route.shape[0]
    return pl.pallas_call(
        _route_kernel,
        grid=(2, t // tr),
        in_specs=[pl.BlockSpec((tr, LANE), lambda p, i: (i, 0))],
        out_specs=[pl.BlockSpec((tr, LANE), lambda p, i: (i * p, 0)), pl.BlockSpec((1, LANE), lambda p, i: (0, 0))],
        out_shape=[jax.ShapeDtypeStruct((t, LANE), jnp.int32), jax.ShapeDtypeStruct((1, LANE), F32)],
        scratch_shapes=[pltpu.VMEM((1, LANE), F32), pltpu.VMEM((1, LANE), F32)],
        compiler_params=_params(("arbitrary", "arbitrary")),
        name="moe_route",
    )(route)


def _row_copy(src, src_row, dst, dst_row, sem):
    src_at = pl.ds(pl.multiple_of(src_row * SUBLANES, SUBLANES), SUBLANES)
    dst_at = pl.ds(pl.multiple_of(dst_row * SUBLANES, SUBLANES), SUBLANES)
    return pltpu.make_async_copy(src.at[src_at], dst.at[dst_at], sem)


def _dispatch_kernel(dest_ref, h_ref, xs_in_ref, xs_ref, sem):
    del xs_in_ref
    td = h_ref.shape[0] // SUBLANES

    def issue(r, carry):
        for kk in range(TOP_K):
            _row_copy(h_ref, r, xs_ref, dest_ref[r * TOP_K + kk], sem).start(priority=kk % 2)
        return carry

    lax.fori_loop(0, td, issue, 0, unroll=8)

    def drain(r, carry):
        for kk in range(TOP_K):
            _row_copy(h_ref, 0, xs_ref, 0, sem).wait()
        return carry

    lax.fori_loop(0, td, drain, 0, unroll=8)


def _dispatch(dest_flat, h, xs_zero, td=256):
    t = h.shape[0] // SUBLANES
    return pl.pallas_call(
        _dispatch_kernel,
        grid=(t // td,),
        in_specs=[
            pl.BlockSpec((td * TOP_K,), lambda i: (i,), memory_space=pltpu.SMEM),
            pl.BlockSpec((td * SUBLANES, LANE), lambda i: (i, 0)),
            pl.BlockSpec(memory_space=pl.ANY),
        ],
        out_specs=pl.BlockSpec(memory_space=pl.ANY),
        out_shape=jax.ShapeDtypeStruct(xs_zero.shape, xs_zero.dtype),
        scratch_shapes=[pltpu.SemaphoreType.DMA],
        input_output_aliases={2: 0},
        compiler_params=_params(("arbitrary",)),
        name="moe_dispatch",
    )(dest_flat, h, xs_zero)


def _experts_kernel(te_ref, nu_ref, xs_ref, w1_ref, b1g_ref, b1l_ref, w2_ref, b2_ref, y_ref, w1b_ref, w2b_ref):
    j = pl.program_id(0)

    @pl.when(j < nu_ref[0])
    def _():
        @pl.when((j == 0) | (te_ref[j] != te_ref[jnp.maximum(j - 1, 0)]))
        def _():
            w1b_ref[...] = w1_ref[0].astype(BF16)
            w2b_ref[...] = w2_ref[0].astype(BF16)

        x = _load_slabs(xs_ref, MOE_TILE).astype(BF16)
        hid_g, hid_l = _deinterleave(_dot(x, w1b_ref[...]))
        x_glu = jnp.minimum(hid_g + b1g_ref[0], SWIGLU_LIMIT)
        x_lin = jnp.clip(hid_l + b1l_ref[0], -SWIGLU_LIMIT, SWIGLU_LIMIT)
        act = x_glu * jax.nn.sigmoid(SWIGLU_ALPHA * x_glu) * (x_lin + 1.0)
        _store_slabs(y_ref, _dot(act.astype(BF16), w2b_ref[...]) + b2_ref[0])

    @pl.when(j >= nu_ref[0])
    def _():
        y_ref[...] = jnp.zeros_like(y_ref)


def _experts(tile_expert, n_used, xs, w1, b1g, b1l, w2, b2):
    rows = xs.shape[0] // SUBLANES
    n_e, ff, d = w2.shape
    tg = MOE_TILE
    used = lambda j, te, nu: jnp.minimum(j, nu[0] - 1)
    exp_of = lambda j, te, nu: te[used(j, te, nu)]
    return pl.pallas_call(
        _experts_kernel,
        grid_spec=pltpu.PrefetchScalarGridSpec(
            num_scalar_prefetch=2,
            grid=(rows // tg,),
            in_specs=[
                pl.BlockSpec((tg * SUBLANES, LANE), lambda j, te, nu: (used(j, te, nu), 0)),
                pl.BlockSpec((1, d, 2 * ff), lambda j, te, nu: (exp_of(j, te, nu), 0, 0)),
                pl.BlockSpec((1, 1, ff), lambda j, te, nu: (exp_of(j, te, nu), 0, 0)),
                pl.BlockSpec((1, 1, ff), lambda j, te, nu: (exp_of(j, te, nu), 0, 0)),
                pl.BlockSpec((1, ff, d), lambda j, te, nu: (exp_of(j, te, nu), 0, 0)),
                pl.BlockSpec((1, 1, d), lambda j, te, nu: (exp_of(j, te, nu), 0, 0)),
            ],
            out_specs=pl.BlockSpec((tg * SUBLANES, LANE), lambda j, te, nu: (j, 0)),
            scratch_shapes=[pltpu.VMEM((d, 2 * ff), BF16), pltpu.VMEM((ff, d), BF16)],
        ),
        out_shape=jax.ShapeDtypeStruct((rows * SUBLANES, LANE), F32),
        compiler_params=_params(("arbitrary",)),
        name="moe_experts",
    )(tile_expert, n_used, xs, w1, b1g, b1l, w2, b2)


def _combine_kernel(dest_ref, route_ref, x1_ref, y_ref, o_ref, buf_ref, sem):
    tc = x1_ref.shape[0]

    def issue(r, carry):
        for kk in range(TOP_K):
            _row_copy(y_ref, dest_ref[r * TOP_K + kk], buf_ref.at[kk], r, sem).start(priority=kk % 2)
        return carry

    lax.fori_loop(0, tc, issue, 0, unroll=8)

    def drain(r, carry):
        for kk in range(TOP_K):
            _row_copy(y_ref, 0, buf_ref.at[kk], 0, sem).wait()
        return carry

    lax.fori_loop(0, tc, drain, 0, unroll=8)

    route = route_ref[...]
    acc = x1_ref[...]
    for kk in range(TOP_K):
        acc = acc + route[:, TOP_K + kk:TOP_K + kk + 1] * _load_slabs(buf_ref, tc, lead=(kk,))
    o_ref[...] = acc


def _combine(dest_flat, route, x1, y, tc=256):
    t, d = x1.shape
    return pl.pallas_call(
        _combine_kernel,
        grid=(t // tc,),
        in_specs=[
            pl.BlockSpec((tc * TOP_K,), lambda i: (i,), memory_space=pltpu.SMEM),
            pl.BlockSpec((tc, LANE), lambda i: (i, 0)),
            pl.BlockSpec((tc, d), lambda i: (i, 0)),
            pl.BlockSpec(memory_space=pl.ANY),
        ],
        out_specs=pl.BlockSpec((tc, d), lambda i: (i, 0)),
        out_shape=jax.ShapeDtypeStruct((t, d), F32),
        scratch_shapes=[pltpu.VMEM((TOP_K, tc * SUBLANES, LANE), F32), pltpu.SemaphoreType.DMA],
        compiler_params=_params(("arbitrary",)),
        name="moe_combine",
    )(dest_flat, route, x1, y)


def _moe(h, route, x1, w1, b1g, b1l, w2, b2):
    t, d = x1.shape
    assert d == SUBLANES * LANE
    n_e = w2.shape[0]
    n_tiles = (t * TOP_K) // MOE_TILE + n_e
    dest, counts = _route(route)
    tiles_per = jnp.ceil(counts[0, :n_e] * (1.0 / MOE_TILE)).astype(jnp.int32)
    tile_end = jnp.cumsum(tiles_per)
    past = (tile_end[None, :] <= jnp.arange(n_tiles, dtype=jnp.int32)[:, None]).astype(jnp.int32)
    tile_expert = jnp.minimum(jnp.sum(past, axis=1), n_e - 1).astype(jnp.int32)
    n_used = tile_end[-1:].astype(jnp.int32)
    dest_flat = dest[:, :TOP_K].reshape(-1)
    xs = _dispatch(dest_flat, h, jnp.zeros((n_tiles * MOE_TILE * SUBLANES, LANE), F32))
    y = _experts(tile_expert, n_used, xs, w1, b1g, b1l, w2, b2)
    return _combine(dest_flat, route, x1, y)


def _pad_cols(w, width):
    return jnp.pad(w, ((0, 0), (0, width - w.shape[1])))


def _rope_tables(pos, reps):
    inv = ROPE_THETA ** (-jnp.arange(0, HEAD_DIM, 2, dtype=F32) / HEAD_DIM)
    ang = pos.astype(F32)[:, None] * inv[None, :]
    ang = jnp.concatenate([ang, ang], axis=-1)
    sign = jnp.concatenate([-jnp.ones((HEAD_DIM // 2,), F32), jnp.ones((HEAD_DIM // 2,), F32)])
    return jnp.tile(jnp.cos(ang), (1, reps)), jnp.tile(jnp.sin(ang) * sign, (1, reps))


def _layer(x, mem, g_mix, g_mem, w_in, b_merge, nsa_q_norm, nsa_k_norm, cmp_pe, cmp_w1, cmp_b1, cmp_w2,
           rwkv_shift_mix, rwkv_w0, rwkv_w_up, rwkv_a0, rwkv_a_up, rwkv_g_up, rwkv_k_k, rwkv_k_a, rwkv_r_k,
           rwkv_ln_w, rwkv_ln_b, mem_w_kv, mem_q_norm, mem_k_norm, w_branch, w_out, g_ffn,
           router_w, router_b, exp_w1, exp_b1, exp_w2, exp_b2, cos, sin, cos_c, sin_c):
    b, s, d = x.shape
    t = b * s
    x2 = x.reshape(t, d)

    o = 0
    parts = []
    for width in (NSA_WIDTH, 6 * NSA_KV_WIDTH, 3 * NSA_HEADS,
                  3 * RWKV_WIDTH + DECAY_RANK + AAA_RANK + GATE_RANK, MEM_WIDTH, N_BRANCHES * D_MODEL):
        parts.append(w_in[:, o:o + width])
        o += width
    w_q, w_kv, w_gate, w_rwkv, w_qm, w_merge = parts

    def rwkv_layout(m):
        r3 = m[:, :3 * RWKV_WIDTH]
        xw = m[:, 3 * RWKV_WIDTH:3 * RWKV_WIDTH + DECAY_RANK]
        xa = m[:, 3 * RWKV_WIDTH + DECAY_RANK:3 * RWKV_WIDTH + DECAY_RANK + AAA_RANK]
        xg = m[:, 3 * RWKV_WIDTH + DECAY_RANK + AAA_RANK:]
        return _pad_cols(jnp.concatenate([r3, _pad_cols(xw, LANE), _pad_cols(xa, LANE), xg], axis=1), RWKV_PAD)

    w_all = jnp.concatenate([rwkv_layout(w_rwkv), w_merge, w_q, w_qm, w_kv, w_gate], axis=1)
    w_all = _pad_cols(w_all, IN_PAD).astype(BF16)
    proj2 = _inproj(x2, g_mix.reshape(1, d), w_all)
    proj3 = proj2.reshape(b, s, IN_PAD)

    rowv = lambda a: a.reshape(1, -1)
    pad_rows = lambda m: jnp.pad(m, ((0, LANE - m.shape[0]), (0, 0)))
    o_b = _rwkv(proj3, rwkv_layout(rowv(rwkv_shift_mix)), rowv(rwkv_w0), pad_rows(rwkv_w_up), rowv(rwkv_a0),
                pad_rows(rwkv_a_up), rwkv_g_up, rowv(rwkv_k_k), rowv(rwkv_k_a), rowv(rwkv_r_k),
                rowv(rwkv_ln_w), rowv(rwkv_ln_b))

    q_gain = jnp.tile(nsa_q_norm.reshape(1, HEAD_DIM), (1, NSA_HEADS))
    k_gain = jnp.tile(nsa_k_norm, (1, NSA_KV_HEADS))
    qn, ks, vs, kw, vw = _nsa_prep(proj3, cos, sin, q_gain, k_gain)
    kc, vc = _nsa_cmp(proj3, cos_c, sin_c, k_gain, cmp_pe, cmp_w1.astype(BF16),
                      cmp_b1.reshape(2, 1, CMP_HIDDEN), cmp_w2.astype(BF16))
    o_a = _nsa_attn(qn, proj3, kc, vc, ks, vs, kw, vw)

    mk, mv = _mem_kv(mem, g_mem.reshape(1, d), mem_w_kv.astype(BF16), mem_k_norm.reshape(1, MEM_HEAD_DIM))
    o_m = _mem_attn(proj3, mem_q_norm.reshape(1, MEM_HEAD_DIM), mk, mv)

    rw = _pad_cols(router_w, LANE)
    rb = jnp.concatenate([router_b, jnp.full((LANE - N_EXPERTS,), NEG_BIG, F32)]).reshape(1, LANE)
    x1, h2, route = _merge(o_a.reshape(t, NSA_WIDTH), o_b.reshape(t, RWKV_WIDTH), o_m.reshape(t, MEM_WIDTH),
                          proj2, x2, b_merge.reshape(1, -1), w_branch.astype(BF16), w_out.astype(BF16),
                          g_ffn.reshape(1, d), rw, rb)

    b1g = exp_b1[:, None, 0::2]
    b1l = exp_b1[:, None, 1::2]
    out = _moe(h2, route, x1, exp_w1, b1g, b1l, exp_w2, exp_b2[:, None, :])
    return out.reshape(b, s, d)


def kernel(x, mem, g_mix, g_mem, w_in, b_merge, nsa_q_norm, nsa_k_norm, cmp_pe, cmp_w1, cmp_b1, cmp_w2,
           rwkv_shift_mix, rwkv_w0, rwkv_w_up, rwkv_a0, rwkv_a_up, rwkv_g_up, rwkv_k_k, rwkv_k_a, rwkv_r_k,
           rwkv_ln_w, rwkv_ln_b, mem_w_kv, mem_q_norm, mem_k_norm, w_branch, w_out, g_ffn,
           router_w, router_b, exp_w1, exp_b1, exp_w2, exp_b2):
    s = x.shape[1]
    cos, sin = _rope_tables(jnp.arange(s), NSA_HEADS)
    n_cmp = (s - CMP_LEN) // CMP_STRIDE + 1
    cos_c, sin_c = _rope_tables(jnp.arange(n_cmp + 1) * CMP_STRIDE + CMP_LEN - 1, NSA_KV_HEADS)
    depth = g_mix.shape[0]
    for l in range(depth):
        x = _layer(x, mem, g_mix[l], g_mem[l], w_in[l], b_merge[l], nsa_q_norm[l], nsa_k_norm[l], cmp_pe[l],
                   cmp_w1[l], cmp_b1[l], cmp_w2[l], rwkv_shift_mix[l], rwkv_w0[l], rwkv_w_up[l], rwkv_a0[l],
                   rwkv_a_up[l], rwkv_g_up[l], rwkv_k_k[l], rwkv_k_a[l], rwkv_r_k[l], rwkv_ln_w[l], rwkv_ln_b[l],
                   mem_w_kv[l], mem_q_norm[l], mem_k_norm[l], w_branch[l], w_out[l], g_ffn[l], router_w[l],
                   router_b[l], exp_w1[l], exp_b1[l], exp_w2[l], exp_b2[l], cos, sin, cos_c, sin_c)
    return x
```

```python
import functools

import jax
import jax.numpy as jnp
from jax import lax
from jax.experimental import pallas as pl
from jax.experimental.pallas import tpu as pltpu

F32 = jnp.float32
BF16 = jnp.bfloat16
HI = lax.Precision.HIGHEST

D_MODEL = 1024
HEAD_DIM = 64
NSA_HEADS = 8
NSA_KV_HEADS = 2
NSA_GROUP = NSA_HEADS // NSA_KV_HEADS
NSA_WIDTH = NSA_HEADS * HEAD_DIM
NSA_KV_WIDTH = NSA_KV_HEADS * HEAD_DIM
CMP_LEN = 32
CMP_STRIDE = 16
CMP_HIDDEN = 128
SEL_BLOCK = 64
TOP_N = 8
WINDOW = 512
Q_BLOCK = 128
FORCE_BONUS = 1000.0
RWKV_HEADS = 8
RWKV_HEAD_DIM = 64
RWKV_WIDTH = RWKV_HEADS * RWKV_HEAD_DIM
DECAY_RANK = 64
AAA_RANK = 64
GATE_RANK = 128
GN_EPS = 64e-5
MEM_HEADS = 4
MEM_HEAD_DIM = 128
MEM_WIDTH = MEM_HEADS * MEM_HEAD_DIM
N_BRANCHES = 3
BRANCH_WIDTH = 512
N_EXPERTS = 32
TOP_K = 4
EXPERT_FF = 1024
SWIGLU_ALPHA = 1.702
SWIGLU_LIMIT = 7.0
ROPE_THETA = 10000.0
RMS_EPS = 1e-6

LANE = 128
LOG2_E = 1.4426950408889634
NEG_BIG = -1e30

RWKV_PAD = 2048
COL_RWKV = 0
COL_MERGE = COL_RWKV + RWKV_PAD
COL_QNSA = COL_MERGE + N_BRANCHES * D_MODEL
COL_QMEM = COL_QNSA + NSA_WIDTH
COL_KV = COL_QMEM + MEM_WIDTH
COL_GATE = COL_KV + 6 * NSA_KV_WIDTH
IN_PAD = 7168
RW_R, RW_K, RW_V, RW_XW, RW_XA, RW_XG = 0, 512, 1024, 1536, 1664, 1792

RWKV_CHUNK = 64
VMEM_LIMIT = 56 * 1024 * 1024


def _dot(a, b, prec=None):
    return jnp.dot(a, b, preferred_element_type=F32, precision=prec)


def _dot_nt(a, b, prec=None):
    return lax.dot_general(a, b, (((1,), (1,)), ((), ())), preferred_element_type=F32, precision=prec)


def _dot_tn(a, b, prec=None):
    return lax.dot_general(a, b, (((0,), (0,)), ((), ())), preferred_element_type=F32, precision=prec)


def _split3(x):
    hi = x.astype(BF16)
    r1 = x - hi.astype(F32)
    mid = r1.astype(BF16)
    lo = (r1 - mid.astype(F32)).astype(BF16)
    return hi, mid, lo


def _dot01_left(m01, x):
    n = x.shape[1]
    out = _dot(m01, jnp.concatenate(_split3(x), axis=1))
    return out[:, :n] + out[:, n:2 * n] + out[:, 2 * n:]


def _dot01_right(x, m01):
    m = x.shape[0]
    out = _dot(jnp.concatenate(_split3(x), axis=0), m01)
    return out[:m] + out[m:2 * m] + out[2 * m:]


def _seg_matrix(width, seg):
    r = lax.broadcasted_iota(jnp.int32, (width, width), 0) // seg
    c = lax.broadcasted_iota(jnp.int32, (width, width), 1) // seg
    return (r == c).astype(F32)


SUBLANES = 8


def _store_slabs(ref, x, lead=()):
    rows = x.shape[0]
    for s in range(SUBLANES):
        ref[lead + (pl.ds(s, rows, stride=SUBLANES), slice(None))] = x[:, s * LANE:(s + 1) * LANE]


def _load_slabs(ref, rows, lead=()):
    return jnp.concatenate(
        [ref[lead + (pl.ds(s, rows, stride=SUBLANES), slice(None))] for s in range(SUBLANES)], axis=1)


def _params(sem):
    return pltpu.CompilerParams(dimension_semantics=sem, vmem_limit_bytes=VMEM_LIMIT)


def _inproj_kernel(x_ref, g_ref, w_ref, o_ref, hn_ref):
    @pl.when(pl.program_id(1) == 0)
    def _():
        x = x_ref[...]
        ms = jnp.mean(x * x, axis=-1, keepdims=True)
        hn_ref[...] = (x * lax.rsqrt(ms + RMS_EPS) * g_ref[...]).astype(BF16)

    o_ref[...] = _dot(hn_ref[...], w_ref[...])


def _inproj(x2, g, w, tm=1024, tn=1792):
    t, d = x2.shape
    n = w.shape[1]
    return pl.pallas_call(
        _inproj_kernel,
        grid=(t // tm, n // tn),
        in_specs=[
            pl.BlockSpec((tm, d), lambda i, j: (i, 0)),
            pl.BlockSpec((1, d), lambda i, j: (0, 0)),
            pl.BlockSpec((d, tn), lambda i, j: (0, j)),
        ],
        out_specs=pl.BlockSpec((tm, tn), lambda i, j: (i, j)),
        out_shape=jax.ShapeDtypeStruct((t, n), F32),
        scratch_shapes=[pltpu.VMEM((tm, d), BF16)],
        compiler_params=_params(("parallel", "arbitrary")),
        name="inproj",
    )(x2, g, w)


def _rwkv_chunk_kernel(p_ref, pprev_ref, mix_ref, w0_ref, wup_ref, a0_ref, aup_ref, gup_ref, kk_ref, ka_ref,
                       rk_ref, rm_ref, y0_ref, bonus_ref, g_ref, gam_ref, m_ref, d0_ref):
    c = RWKV_CHUNK
    n = RWKV_HEAD_DIM
    rows = p_ref.shape[1]

    p = p_ref[0]
    row = lax.broadcasted_iota(jnp.int32, p.shape, 0)
    last_prev = jnp.where(pl.program_id(1) == 0, 0.0, 1.0) * pprev_ref[0, 7:8, :]
    prev = jnp.where(row == 0, last_prev, pltpu.roll(p, 1, axis=0))
    ps = p + (prev - p) * mix_ref[...]
    r = ps[:, RW_R:RW_R + RWKV_WIDTH]
    k = ps[:, RW_K:RW_K + RWKV_WIDTH]
    v = ps[:, RW_V:RW_V + RWKV_WIDTH]
    xw = ps[:, RW_XW:RW_XW + LANE]
    xa = ps[:, RW_XA:RW_XA + LANE]
    xg = ps[:, RW_XG:RW_XG + LANE]

    z = -(w0_ref[...] + _dot(jnp.tanh(xw), wup_ref[...], HI))
    softplus = jnp.maximum(z, 0.0) + jnp.log1p(jnp.exp(-jnp.abs(z)))
    w = -softplus - 0.5
    logw = -jnp.exp(w)
    a = jax.nn.sigmoid(a0_ref[...] + _dot(xa, aup_ref[...], HI))
    g_ref[0] = _dot(jax.nn.sigmoid(xg), gup_ref[...], HI)

    seg = _seg_matrix(RWKV_WIDTH, n).astype(BF16)
    kk = k * kk_ref[...]
    k = k * (1.0 + (a - 1.0) * ka_ref[...])
    sums = _dot01_right(jnp.concatenate([kk * kk, r * k * rk_ref[...]], axis=0), seg)
    kk = kk / jnp.maximum(jnp.sqrt(sums[:rows]), 1e-12)
    bonus_ref[0] = sums[rows:] * v
    kka = kk * a

    ti = lax.broadcasted_iota(jnp.int32, (2 * c, 2 * c), 0)
    tj = lax.broadcasted_iota(jnp.int32, (2 * c, 2 * c), 1)
    keep = (tj % c) < jnp.where(ti < c, ti, ti - c + 1)
    ci = lax.broadcasted_iota(jnp.int32, (c, c), 0)
    cj = lax.broadcasted_iota(jnp.int32, (c, c), 1)
    eye = (ci == cj).astype(F32)
    ltri = (cj <= ci).astype(BF16)
    zeros_cn = jnp.zeros((c, n), BF16)

    chains = []
    for j in range(rows // c):
        rs = slice(j * c, (j + 1) * c)
        lw = logw[rs]
        cum = _dot01_left(ltri, lw)
        tot = cum[c - 1:c, :]
        einv = jnp.exp(-cum)
        dec_end = jnp.exp(tot - cum)
        r_f = r[rs] * jnp.exp(cum)
        a_t = (-kk[rs] * jnp.exp(cum - lw)).astype(BF16)
        b_t = (kka[rs] * einv).astype(BF16)
        k_t = (k[rs] * einv).astype(BF16)
        r_t = r_f.astype(BF16)
        b_e = (kka[rs] * dec_end).astype(BF16)
        k_e = (k[rs] * dec_end).astype(BF16)
        v_b = v[rs].astype(BF16)
        gam_ref[0, j] = jnp.exp(tot)
        for h in range(RWKV_HEADS):
            sl = slice(h * n, (h + 1) * n)
            chains.append(dict(j=j, h=h, rs=rs, sl=sl, a=a_t[:, sl], r=r_t[:, sl], rf=r_f[:, sl], v=v_b[:, sl],
                               rhs=jnp.concatenate([b_t[:, sl], k_t[:, sl]], axis=0),
                               bke=jnp.concatenate([b_e[:, sl], k_e[:, sl]], axis=0)))

    for ch in chains:
        lhs = jnp.concatenate([ch["a"], ch["r"]], axis=0)
        ch["amat"] = jnp.where(keep, _dot_nt(lhs, ch["rhs"]), 0.0)
        ch["pw"] = ch["amat"][:c, :c]
        ch["tinv"] = eye + ch["pw"]
    for _ in range(5):
        for ch in chains:
            pw_b = ch["pw"].astype(BF16)
            ch["pw"] = _dot(pw_b, pw_b)
        for ch in chains:
            ch["tinv"] = ch["tinv"] + _dot(ch["tinv"].astype(BF16), ch["pw"].astype(BF16))
    for ch in chains:
        ch["akv"] = _dot(ch["amat"][:c, c:].astype(BF16), ch["v"])
    for ch in chains:
        wu = _dot(ch["tinv"].astype(BF16), jnp.concatenate([ch["a"], ch["akv"].astype(BF16)], axis=1)).astype(BF16)
        ch["x"] = jnp.concatenate([wu, jnp.concatenate([zeros_cn, ch["v"]], axis=1)], axis=0)
    for ch in chains:
        ry = _dot(ch["amat"][c:, :].astype(BF16), ch["x"])
        rm_ref[0, ch["rs"], ch["sl"]] = (ch["rf"] + ry[:, :n]).astype(BF16)
        y0_ref[0, ch["rs"], ch["sl"]] = ry[:, n:]
    for ch in chains:
        md = _dot_tn(ch["x"], ch["bke"])
        m_ref[0, ch["j"], ch["h"]] = md[:n].astype(BF16)
        d0_ref[0, ch["j"], ch["h"]] = md[n:]


def _rwkv_chunks(proj3, mix, w0, wup, a0, aup, gup, k_k, k_a, r_k, rows=256):
    b, s, _ = proj3.shape
    c = RWKV_CHUNK
    nc = s // c
    cps = rows // c
    vec = lambda width: pl.BlockSpec((1, width), lambda i, t: (0, 0))
    mat = lambda nrows: pl.BlockSpec((nrows, RWKV_WIDTH), lambda i, t: (0, 0))
    tok = pl.BlockSpec((1, rows, RWKV_WIDTH), lambda i, t: (i, t, 0))
    sq = pl.BlockSpec((1, cps, RWKV_HEADS, RWKV_HEAD_DIM, RWKV_HEAD_DIM), lambda i, t: (i, t, 0, 0, 0))
    tok_shape = lambda dt: jax.ShapeDtypeStruct((b, s, RWKV_WIDTH), dt)
    sq_shape = lambda dt: jax.ShapeDtypeStruct((b, nc, RWKV_HEADS, RWKV_HEAD_DIM, RWKV_HEAD_DIM), dt)
    return pl.pallas_call(
        _rwkv_chunk_kernel,
        grid=(b, s // rows),
        in_specs=[
            pl.BlockSpec((1, rows, RWKV_PAD), lambda i, t: (i, t, COL_RWKV // RWKV_PAD)),
            pl.BlockSpec((1, 8, RWKV_PAD), lambda i, t: (i, jnp.maximum(t * (rows // 8) - 1, 0), COL_RWKV // RWKV_PAD)),
            vec(RWKV_PAD), vec(RWKV_WIDTH), mat(LANE), vec(RWKV_WIDTH), mat(LANE), mat(LANE),
            vec(RWKV_WIDTH), vec(RWKV_WIDTH), vec(RWKV_WIDTH),
        ],
        out_specs=[tok, tok, tok, tok,
                   pl.BlockSpec((1, cps, 1, RWKV_WIDTH), lambda i, t: (i, t, 0, 0)), sq, sq],
        out_shape=[tok_shape(BF16), tok_shape(F32), tok_shape(F32), tok_shape(F32),
                   jax.ShapeDtypeStruct((b, nc, 1, RWKV_WIDTH), F32), sq_shape(BF16), sq_shape(F32)],
        compiler_params=_params(("parallel", "parallel")),
        name="rwkv7_chunks",
    )(proj3, proj3, mix, w0, wup, a0, aup, gup, k_k, k_a, r_k)


def _rwkv_scan_kernel(rm_ref, y0_ref, bonus_ref, g_ref, gam_ref, m_ref, d0_ref, lnw_ref, lnb_ref,
                      o_ref, state_ref, y_ref):
    c = RWKV_CHUNK
    n = RWKV_HEAD_DIM

    @pl.when(pl.program_id(1) == 0)
    def _():
        state_ref[...] = jnp.zeros_like(state_ref)

    for j in range(gam_ref.shape[1]):
        rs = slice(j * c, (j + 1) * c)
        gam = gam_ref[0, j]
        for h in range(RWKV_HEADS):
            sl = slice(h * n, (h + 1) * n)
            s = state_ref[h]
            s_b = s.astype(BF16)
            y_ref[rs, sl] = _dot_nt(rm_ref[0, rs, sl], s_b) + y0_ref[0, rs, sl]
            state_ref[h] = s * gam[:, sl] + _dot(s_b, m_ref[0, j, h]) + d0_ref[0, j, h]

    seg = _seg_matrix(RWKV_WIDTH, n).astype(BF16)
    y = y_ref[...]
    mu = _dot01_right(y, seg) * (1.0 / n)
    d = y - mu
    var = _dot01_right(d * d, seg) * (1.0 / n)
    yn = d * lax.rsqrt(var + GN_EPS) * lnw_ref[...] + lnb_ref[...]
    o_ref[0] = (yn + bonus_ref[0]) * g_ref[0]


def _rwkv_scan(rm, y0, bonus, g, gam, m, d0, ln_w, ln_b, rows=256):
    b, s, _ = rm.shape
    cps = rows // RWKV_CHUNK
    vec = pl.BlockSpec((1, RWKV_WIDTH), lambda i, t: (0, 0))
    tok = pl.BlockSpec((1, rows, RWKV_WIDTH), lambda i, t: (i, t, 0))
    sq = pl.BlockSpec((1, cps, RWKV_HEADS, RWKV_HEAD_DIM, RWKV_HEAD_DIM), lambda i, t: (i, t, 0, 0, 0))
    return pl.pallas_call(
        _rwkv_scan_kernel,
        grid=(b, s // rows),
        in_specs=[tok, tok, tok, tok, pl.BlockSpec((1, cps, 1, RWKV_WIDTH), lambda i, t: (i, t, 0, 0)), sq, sq,
                  vec, vec],
        out_specs=tok,
        out_shape=jax.ShapeDtypeStruct((b, s, RWKV_WIDTH), F32),
        scratch_shapes=[
            pltpu.VMEM((RWKV_HEADS, RWKV_HEAD_DIM, RWKV_HEAD_DIM), F32),
            pltpu.VMEM((rows, RWKV_WIDTH), F32),
        ],
        compiler_params=_params(("parallel", "arbitrary")),
        name="rwkv7_scan",
    )(rm, y0, bonus, g, gam, m, d0, ln_w, ln_b)


def _rwkv(proj3, mix, w0, wup, a0, aup, gup, k_k, k_a, r_k, ln_w, ln_b):
    rm, y0, bonus, g, gam, m, d0 = _rwkv_chunks(proj3, mix, w0, wup, a0, aup, gup, k_k, k_a, r_k)
    return _rwkv_scan(rm, y0, bonus, g, gam, m, d0, ln_w, ln_b)


def _rope(x, cos, sin_signed):
    w = x.shape[-1]
    first_half = (lax.broadcasted_iota(jnp.int32, x.shape, 1) % HEAD_DIM) < (HEAD_DIM // 2)
    rot = jnp.where(first_half, pltpu.roll(x, w - HEAD_DIM // 2, axis=1), pltpu.roll(x, HEAD_DIM // 2, axis=1))
    return x * cos + rot * sin_signed


def _head_rmsnorm(x, gain, seg):
    ms = _dot(x * x, seg, HI) * (1.0 / HEAD_DIM)
    return x * lax.rsqrt(ms + RMS_EPS) * gain


def _split_groups(x):
    return [x[:, g * HEAD_DIM:(g + 1) * HEAD_DIM] for g in range(NSA_KV_HEADS)]


def _nsa_prep_kernel(q_ref, ksl_ref, vsl_ref, kwn_ref, vwn_ref, cos_ref, sin_ref, qn_ref, kn_ref,
                     qo_ref, ks_ref, vs_ref, kw_ref, vw_ref):
    seg_q = _seg_matrix(NSA_WIDTH, HEAD_DIM)
    seg_k = _seg_matrix(NSA_KV_WIDTH, HEAD_DIM)
    cos_k = cos_ref[:, :NSA_KV_WIDTH]
    sin_k = sin_ref[:, :NSA_KV_WIDTH]

    q = _rope(_head_rmsnorm(q_ref[0], qn_ref[...], seg_q), cos_ref[...], sin_ref[...])
    q_t = (q * (HEAD_DIM ** -0.5 * LOG2_E)).T
    for h in range(NSA_HEADS):
        qo_ref[0, h] = q_t[h * HEAD_DIM:(h + 1) * HEAD_DIM].astype(BF16)

    ks = _rope(_head_rmsnorm(ksl_ref[0], kn_ref[1:2, :], seg_k), cos_k, sin_k)
    kw = _rope(_head_rmsnorm(kwn_ref[0], kn_ref[2:3, :], seg_k), cos_k, sin_k)
    vs_t = vsl_ref[0].T
    vw_t = vwn_ref[0].T
    for g, (a, c_) in enumerate(zip(_split_groups(ks), _split_groups(kw))):
        ks_ref[0, g] = a.astype(BF16)
        kw_ref[0, g] = c_.astype(BF16)
        vs_ref[0, g] = vs_t[g * HEAD_DIM:(g + 1) * HEAD_DIM].astype(BF16)
        vw_ref[0, g] = vw_t[g * HEAD_DIM:(g + 1) * HEAD_DIM].astype(BF16)


def _nsa_prep(proj3, cos, sin_signed, q_norm, k_norm, tq=512):
    b, s, _ = proj3.shape
    kvw = NSA_KV_WIDTH
    kv_spec = lambda j: pl.BlockSpec((1, tq, kvw), lambda i, t: (i, t, COL_KV // kvw + j))
    out_k = pl.BlockSpec((1, NSA_KV_HEADS, tq, HEAD_DIM), lambda i, t: (i, 0, t, 0))
    out_vt = pl.BlockSpec((1, NSA_KV_HEADS, HEAD_DIM, tq), lambda i, t: (i, 0, 0, t))
    k_shape = jax.ShapeDtypeStruct((b, NSA_KV_HEADS, s, HEAD_DIM), BF16)
    vt_shape = jax.ShapeDtypeStruct((b, NSA_KV_HEADS, HEAD_DIM, s), BF16)
    return pl.pallas_call(
        _nsa_prep_kernel,
        grid=(b, s // tq),
        in_specs=[
            pl.BlockSpec((1, tq, NSA_WIDTH), lambda i, t: (i, t, COL_QNSA // NSA_WIDTH)),
            kv_spec(2), kv_spec(3), kv_spec(4), kv_spec(5),
            pl.BlockSpec((tq, NSA_WIDTH), lambda i, t: (t, 0)),
            pl.BlockSpec((tq, NSA_WIDTH), lambda i, t: (t, 0)),
            pl.BlockSpec((1, NSA_WIDTH), lambda i, t: (0, 0)),
            pl.BlockSpec((3, kvw), lambda i, t: (0, 0)),
        ],
        out_specs=[
            pl.BlockSpec((1, NSA_HEADS, HEAD_DIM, tq), lambda i, t: (i, 0, 0, t)),
            out_k, out_vt, out_k, out_vt,
        ],
        out_shape=[jax.ShapeDtypeStruct((b, NSA_HEADS, HEAD_DIM, s), BF16), k_shape, vt_shape, k_shape, vt_shape],
        compiler_params=_params(("parallel", "parallel")),
        name="nsa_prep",
    )(proj3, proj3, proj3, proj3, proj3, cos, sin_signed, q_norm, k_norm)


def _gelu_tanh(x):
    return 0.5 * x * (1.0 + jnp.tanh(0.7978845608028654 * (x + 0.044715 * x * x * x)))


def _nsa_cmp_kernel(kc_in_ref, vc_in_ref, cos_ref, sin_ref, kn_ref, pe_ref, w1_ref, b1_ref, w2_ref,
                    kc_ref, vc_ref):
    n_cmp = (kc_in_ref.shape[1] - CMP_LEN) // CMP_STRIDE + 1
    n_pad = n_cmp + 1
    zero_row = jnp.zeros((1, NSA_KV_WIDTH), F32)
    outs = []
    for j, src in enumerate((kc_in_ref, vc_in_ref)):
        acc = jnp.zeros((NSA_KV_HEADS * n_pad, CMP_HIDDEN), F32)
        for l in range(CMP_LEN):
            x = src[0, pl.ds(l, n_cmp, stride=CMP_STRIDE), :]
            x = jnp.concatenate([x, zero_row], axis=0)
            xg = jnp.concatenate(_split_groups(x), axis=0) + pe_ref[j, l:l + 1, :]
            acc = acc + _dot(xg.astype(BF16), w1_ref[j, l * HEAD_DIM:(l + 1) * HEAD_DIM, :])
        hid = _gelu_tanh(acc + b1_ref[j])
        out = _dot(hid.astype(BF16), w2_ref[j])
        outs.append(jnp.concatenate([out[g * n_pad:(g + 1) * n_pad] for g in range(NSA_KV_HEADS)], axis=1))
    kc, vc = outs
    seg_k = _seg_matrix(NSA_KV_WIDTH, HEAD_DIM)
    kc = _rope(_head_rmsnorm(kc, kn_ref[0:1, :], seg_k), cos_ref[...], sin_ref[...])
    vc_t = vc.T
    for g, a in enumerate(_split_groups(kc)):
        kc_ref[0, g] = a.astype(BF16)
        vc_ref[0, g] = vc_t[g * HEAD_DIM:(g + 1) * HEAD_DIM].astype(BF16)


def _nsa_cmp(proj3, cos_c, sin_c, k_norm, pe, w1, b1, w2):
    b, s, _ = proj3.shape
    kvw = NSA_KV_WIDTH
    n_pad = (s - CMP_LEN) // CMP_STRIDE + 2
    full = lambda shape: pl.BlockSpec(shape, lambda i: (0,) * len(shape))
    k_spec = pl.BlockSpec((1, NSA_KV_HEADS, n_pad, HEAD_DIM), lambda i: (i, 0, 0, 0))
    vt_spec = pl.BlockSpec((1, NSA_KV_HEADS, HEAD_DIM, n_pad), lambda i: (i, 0, 0, 0))
    k_shape = jax.ShapeDtypeStruct((b, NSA_KV_HEADS, n_pad, HEAD_DIM), BF16)
    vt_shape = jax.ShapeDtypeStruct((b, NSA_KV_HEADS, HEAD_DIM, n_pad), BF16)
    return pl.pallas_call(
        _nsa_cmp_kernel,
        grid=(b,),
        in_specs=[
            pl.BlockSpec((1, s, kvw), lambda i: (i, 0, COL_KV // kvw)),
            pl.BlockSpec((1, s, kvw), lambda i: (i, 0, COL_KV // kvw + 1)),
            full((n_pad, kvw)), full((n_pad, kvw)), full((3, kvw)),
            full((2, CMP_LEN, HEAD_DIM)), full((2, CMP_LEN * HEAD_DIM, CMP_HIDDEN)),
            full((2, 1, CMP_HIDDEN)), full((2, CMP_HIDDEN, HEAD_DIM)),
        ],
        out_specs=[k_spec, vt_spec],
        out_shape=[k_shape, vt_shape],
        compiler_params=_params(("parallel",)),
        name="nsa_compress",
    )(proj3, proj3, cos_c, sin_c, k_norm, pe, w1, b1, w2)


def _nsa_attn_kernel(q_ref, gate_ref, kc_ref, vc_ref, ks_ref, vs_ref, kw_ref, vw_ref, o_ref):
    qb = Q_BLOCK
    hg = NSA_GROUP
    g = pl.program_id(1)
    c = pl.program_id(2)
    cols = hg * qb
    qt = jnp.concatenate([q_ref[0, h] for h in range(hg)], axis=1)
    tile_heads = lambda x: jnp.concatenate([x] * hg, axis=1)

    n_cp = kc_ref.shape[2]
    s_c = _dot(kc_ref[0, 0], qt)
    cend = lax.broadcasted_iota(jnp.int32, (n_cp, qb), 0) * CMP_STRIDE + CMP_LEN - 1
    cmask = tile_heads(jnp.where(cend <= c * qb + lax.broadcasted_iota(jnp.int32, (n_cp, qb), 1), 1, 0)) > 0
    s_m = jnp.where(cmask, s_c, -jnp.inf)
    m = jnp.max(s_m, axis=0, keepdims=True)
    m = jnp.where(m == -jnp.inf, 0.0, m)
    e = jnp.where(cmask, jnp.exp2(s_m - m), 0.0)
    p_c = e / jnp.maximum(jnp.sum(e, axis=0, keepdims=True), 1e-30)
    o_cmp = _dot(vc_ref[0, 0], p_c.astype(BF16))

    n_blk = ks_ref.shape[2] // SEL_BLOCK
    p_sum = p_c[:, 0:qb]
    for h in range(1, hg):
        p_sum = p_sum + p_c[:, h * qb:(h + 1) * qb]
    bi = lax.broadcasted_iota(jnp.int32, (n_blk, n_cp), 0) * SEL_BLOCK
    ci = lax.broadcasted_iota(jnp.int32, (n_blk, n_cp), 1) * CMP_STRIDE
    cover_t = ((ci < bi + SEL_BLOCK) & (ci + CMP_LEN > bi)).astype(F32)
    imp = _dot(cover_t, p_sum, HI)
    blk = lax.broadcasted_iota(jnp.int32, (n_blk, qb), 0)
    qpos = c * qb + lax.broadcasted_iota(jnp.int32, (n_blk, qb), 1)
    cur = qpos // SEL_BLOCK
    forced = (blk == 0) | (blk == cur) | (blk == cur - 1)
    imp = jnp.where(forced, imp + FORCE_BONUS, imp)
    imp = jnp.where(blk <= cur, imp, -jnp.inf)
    rank = jnp.zeros((n_blk, qb), F32)
    for j in range(n_blk):
        other = imp[j:j + 1, :]
        rank = rank + jnp.where(blk > j, jnp.where(other >= imp, 1.0, 0.0), jnp.where(other > imp, 1.0, 0.0))
    sel_t = jnp.where(rank < min(TOP_N, n_blk), 1.0, 0.0).astype(BF16)

    kt_w = 4 * LANE
    kq = c * qb + lax.broadcasted_iota(jnp.int32, (kt_w, qb), 1)
    krel = lax.broadcasted_iota(jnp.int32, (kt_w, qb), 0)
    er = lax.broadcasted_iota(jnp.int32, (kt_w, n_blk), 0) // SEL_BLOCK
    ec = lax.broadcasted_iota(jnp.int32, (kt_w, n_blk), 1)

    def sel_body(kt, carry):
        m_i, l_i, acc = carry
        off = pl.multiple_of(kt * kt_w, kt_w)
        expand = jnp.where(ec == er + kt * (kt_w // SEL_BLOCK), 1.0, 0.0).astype(BF16)
        in_sel = _dot(expand, sel_t)
        bias = jnp.where((in_sel > 0.5) & (krel + off <= kq), 0.0, NEG_BIG)
        s = _dot(ks_ref[0, 0, pl.ds(off, kt_w), :], qt) + tile_heads(bias)
        m_new = jnp.maximum(m_i, jnp.max(s, axis=0, keepdims=True))
        alpha = jnp.exp2(m_i - m_new)
        p = jnp.exp2(s - m_new)
        l_new = alpha * l_i + jnp.sum(p, axis=0, keepdims=True)
        acc = alpha * acc + _dot(vs_ref[0, 0, :, pl.ds(off, kt_w)], p.astype(BF16))
        return m_new, l_new, acc

    init = (jnp.full((1, cols), NEG_BIG, F32), jnp.zeros((1, cols), F32), jnp.zeros((HEAD_DIM, cols), F32))
    _, l_s, acc_s = lax.fori_loop(0, (c * qb + qb + kt_w - 1) // kt_w, sel_body, init)
    o_slc = acc_s / l_s

    span = WINDOW + 2 * qb
    w0 = pl.multiple_of(jnp.clip(c * qb - WINDOW, 0, kw_ref.shape[2] - span), qb)
    kpos = w0 + lax.broadcasted_iota(jnp.int32, (span, qb), 0)
    wq = c * qb + lax.broadcasted_iota(jnp.int32, (span, qb), 1)
    wbias = jnp.where((kpos <= wq) & (kpos > wq - WINDOW), 0.0, NEG_BIG)
    s_w = _dot(kw_ref[0, 0, pl.ds(w0, span), :], qt) + tile_heads(wbias)
    e_w = jnp.exp2(s_w - jnp.max(s_w, axis=0, keepdims=True))
    o_win = _dot(vw_ref[0, 0, :, pl.ds(w0, span)], e_w.astype(BF16)) / jnp.sum(e_w, axis=0, keepdims=True)

    gates_t = jax.nn.sigmoid(gate_ref[0]).T
    grow = lax.broadcasted_iota(jnp.int32, gates_t.shape, 0)
    outs = []
    for h in range(hg):
        hc = slice(h * qb, (h + 1) * qb)
        first = (g * hg + h) * 3
        pick = lambda j: jnp.sum(jnp.where(grow == first + j, gates_t, 0.0), axis=0, keepdims=True)
        outs.append(pick(0) * o_cmp[:, hc] + pick(1) * o_slc[:, hc] + pick(2) * o_win[:, hc])
    o_ref[0] = jnp.concatenate(outs, axis=0).T


def _nsa_attn(q, proj3, kc, vc, ks, vs, kw, vw):
    b, _, _, s = q.shape
    gw = NSA_GROUP * HEAD_DIM
    n_pad = kc.shape[2]
    per_group = lambda rows, width: pl.BlockSpec((1, 1, rows, width), lambda i, g, c: (i, g, 0, 0))
    return pl.pallas_call(
        _nsa_attn_kernel,
        grid=(b, NSA_KV_HEADS, s // Q_BLOCK),
        in_specs=[
            pl.BlockSpec((1, NSA_GROUP, HEAD_DIM, Q_BLOCK), lambda i, g, c: (i, g, 0, c)),
            pl.BlockSpec((1, Q_BLOCK, LANE), lambda i, g, c: (i, c, COL_GATE // LANE)),
            per_group(n_pad, HEAD_DIM), per_group(HEAD_DIM, n_pad),
            per_group(s, HEAD_DIM), per_group(HEAD_DIM, s), per_group(s, HEAD_DIM), per_group(HEAD_DIM, s),
        ],
        out_specs=pl.BlockSpec((1, Q_BLOCK, gw), lambda i, g, c: (i, c, g)),
        out_shape=jax.ShapeDtypeStruct((b, s, NSA_WIDTH), F32),
        compiler_params=_params(("parallel", "parallel", "arbitrary")),
        name="nsa_attn",
    )(q, proj3, kc, vc, ks, vs, kw, vw)


def _mem_kv_kernel(mem_ref, g_ref, w_ref, kn_ref, k_ref, v_ref):
    x = mem_ref[0]
    ms = jnp.mean(x * x, axis=-1, keepdims=True)
    xn = (x * lax.rsqrt(ms + RMS_EPS) * g_ref[...]).astype(BF16)
    kv = _dot(xn, w_ref[...])
    for h in range(MEM_HEADS):
        sl = slice(h * MEM_HEAD_DIM, (h + 1) * MEM_HEAD_DIM)
        kh = kv[:, sl]
        kms = jnp.mean(kh * kh, axis=-1, keepdims=True)
        k_ref[0, :, sl] = (kh * lax.rsqrt(kms + RMS_EPS) * kn_ref[...]).astype(BF16)
    v_ref[0] = kv[:, MEM_WIDTH:].astype(BF16)


def _mem_kv(mem, g_mem, w_kv, k_norm):
    b, m, d = mem.shape
    spec = pl.BlockSpec((1, m, MEM_WIDTH), lambda i: (i, 0, 0))
    shape = jax.ShapeDtypeStruct((b, m, MEM_WIDTH), BF16)
    return pl.pallas_call(
        _mem_kv_kernel,
        grid=(b,),
        in_specs=[
            pl.BlockSpec((1, m, d), lambda i: (i, 0, 0)),
            pl.BlockSpec((1, d), lambda i: (0, 0)),
            pl.BlockSpec((d, 2 * MEM_WIDTH), lambda i: (0, 0)),
            pl.BlockSpec((1, MEM_HEAD_DIM), lambda i: (0, 0)),
        ],
        out_specs=[spec, spec],
        out_shape=[shape, shape],
        compiler_params=_params(("parallel",)),
        name="mem_kv",
    )(mem, g_mem, w_kv, k_norm)


def _mem_attn_kernel(q_ref, qn_ref, k_ref, v_ref, o_ref):
    q = q_ref[0]
    for h in range(MEM_HEADS):
        sl = slice(h * MEM_HEAD_DIM, (h + 1) * MEM_HEAD_DIM)
        qh = q[:, sl]
        ms = jnp.mean(qh * qh, axis=-1, keepdims=True)
        qh = (qh * lax.rsqrt(ms + RMS_EPS) * qn_ref[...]).astype(BF16)
        s = _dot_nt(qh, k_ref[0, :, sl]) * (MEM_HEAD_DIM ** -0.5)
        m = jnp.max(s, axis=-1, keepdims=True)
        e = jnp.exp(s - m)
        p = e / jnp.sum(e, axis=-1, keepdims=True)
        o_ref[0, :, sl] = _dot(p.astype(BF16), v_ref[0, :, sl])


def _mem_attn(proj3, q_norm, k, v, tq=512):
    b, s, _ = proj3.shape
    m = k.shape[1]
    kv_spec = pl.BlockSpec((1, m, MEM_WIDTH), lambda i, t: (i, 0, 0))
    return pl.pallas_call(
        _mem_attn_kernel,
        grid=(b, s // tq),
        in_specs=[
            pl.BlockSpec((1, tq, MEM_WIDTH), lambda i, t: (i, t, COL_QMEM // MEM_WIDTH)),
            pl.BlockSpec((1, MEM_HEAD_DIM), lambda i, t: (0, 0)),
            kv_spec, kv_spec,
        ],
        out_specs=pl.BlockSpec((1, tq, MEM_WIDTH), lambda i, t: (i, t, 0)),
        out_shape=jax.ShapeDtypeStruct((b, s, MEM_WIDTH), F32),
        compiler_params=_params(("parallel", "parallel")),
        name="mem_attn",
    )(proj3, q_norm, k, v)


def _merge_kernel(oa_ref, ob_ref, om_ref, l0_ref, l1_ref, l2_ref, x_ref, bm_ref, wb_ref, wo_ref,
                  gf_ref, rw_ref, rb_ref, x1_ref, h_ref, route_ref):
    mixed = None
    for n, (o_ref, l_ref) in enumerate(((oa_ref, l0_ref), (ob_ref, l1_ref), (om_ref, l2_ref))):
        gate = jax.nn.sigmoid(l_ref[...] + bm_ref[:, n * D_MODEL:(n + 1) * D_MODEL])
        term = gate * _dot(o_ref[...].astype(BF16), wb_ref[n])
        mixed = term if mixed is None else mixed + term
    x1 = x_ref[...] + _dot(mixed.astype(BF16), wo_ref[...])
    x1_ref[...] = x1
    ms = jnp.mean(x1 * x1, axis=-1, keepdims=True)
    h = x1 * lax.rsqrt(ms + RMS_EPS) * gf_ref[...]
    _store_slabs(h_ref, h)

    logits = _dot(h, rw_ref[...], HI) + rb_ref[...]
    lane_f = lax.broadcasted_iota(jnp.int32, logits.shape, 1).astype(F32)
    work = logits
    picks = []
    for _ in range(TOP_K):
        mx = jnp.max(work, axis=-1, keepdims=True)
        idx = jnp.min(jnp.where(work == mx, lane_f, 2.0 * LANE), axis=-1, keepdims=True)
        picks.append((idx, mx))
        work = jnp.where(lane_f == idx, -jnp.inf, work)
    exps = [jnp.exp(mx - picks[0][1]) for _, mx in picks]
    denom = functools.reduce(lambda a, b: a + b, exps)
    route = jnp.zeros(logits.shape, F32)
    for kk, ((idx, _), ex) in enumerate(zip(picks, exps)):
        route = jnp.where(lane_f == kk, idx, route)
        route = jnp.where(lane_f == TOP_K + kk, ex / denom, route)
    route_ref[...] = route


def _merge(o_a, o_b, o_m, proj2, x2, b_merge, w_branch, w_out, g_ffn, router_w, router_b, tm=512):
    t, d = x2.shape
    row = lambda width: pl.BlockSpec((tm, width), lambda i: (i, 0))
    logit = lambda n: pl.BlockSpec((tm, d), lambda i: (i, COL_MERGE // d + n))
    full = lambda shape: pl.BlockSpec(shape, lambda i: (0,) * len(shape))
    return pl.pallas_call(
        _merge_kernel,
        grid=(t // tm,),
        in_specs=[
            row(BRANCH_WIDTH), row(BRANCH_WIDTH), row(BRANCH_WIDTH), logit(0), logit(1), logit(2), row(d),
            full((1, N_BRANCHES * d)), full((N_BRANCHES, BRANCH_WIDTH, d)), full((d, d)), full((1, d)),
            full((d, LANE)), full((1, LANE)),
        ],
        out_specs=[row(d), pl.BlockSpec((tm * SUBLANES, LANE), lambda i: (i, 0)), row(LANE)],
        out_shape=[jax.ShapeDtypeStruct((t, d), F32), jax.ShapeDtypeStruct((t * SUBLANES, LANE), F32),
                   jax.ShapeDtypeStruct((t, LANE), F32)],
        compiler_params=_params(("parallel",)),
        name="merge_router",
    )(o_a, o_b, o_m, proj2, proj2, proj2, x2, b_merge, w_branch, w_out, g_ffn, router_w, router_b)


def _deinterleave(x):
    rows, width = x.shape
    lane = lax.broadcasted_iota(jnp.int32, (rows, LANE), 1)
    half = LANE // 2
    low = lane < half
    idx = jnp.where(low, 2 * lane, 2 * (lane - half) + 1)
    evens, odds = [], []
    for j in range(0, width, 2 * LANE):
        a = jnp.take_along_axis(x[:, j:j + LANE], idx, axis=1)
        b = jnp.take_along_axis(x[:, j + LANE:j + 2 * LANE], idx, axis=1)
        evens.append(jnp.where(low, a, pltpu.roll(b, half, axis=1)))
        odds.append(jnp.where(low, pltpu.roll(a, half, axis=1), b))
    return jnp.concatenate(evens, axis=1), jnp.concatenate(odds, axis=1)


MOE_TILE = 512


def _route_kernel(route_ref, dest_ref, counts_ref, carry_ref, start_ref):
    phase = pl.program_id(0)
    i = pl.program_id(1)
    tr = route_ref.shape[0]
    route = route_ref[...]
    lane = lax.broadcasted_iota(jnp.int32, route.shape, 1)
    lane_f = lane.astype(F32)
    hits = [lane_f == route[:, kk:kk + 1] for kk in range(TOP_K)]
    sel = jnp.zeros(route.shape, F32)
    for hit in hits:
        sel = jnp.where(hit, 1.0, sel)

    @pl.when((phase == 0) & (i == 0))
    def _():
        carry_ref[...] = jnp.zeros_like(carry_ref)

    @pl.when((phase == 1) & (i == 0))
    def _():
        counts = carry_ref[...]
        counts_ref[...] = counts
        padded = jnp.ceil(counts * (1.0 / MOE_TILE)) * MOE_TILE
        ui = lax.broadcasted_iota(jnp.int32, (LANE, LANE), 0)
        uj = lax.broadcasted_iota(jnp.int32, (LANE, LANE), 1)
        start_ref[...] = _dot01_right(jnp.broadcast_to(padded, (8, LANE)), (ui < uj).astype(BF16))[0:1]
        carry_ref[...] = jnp.zeros_like(carry_ref)

    @pl.when(phase == 1)
    def _():
        ri = lax.broadcasted_iota(jnp.int32, (tr, tr), 0)
        rj = lax.broadcasted_iota(jnp.int32, (tr, tr), 1)
        rank = carry_ref[...] + _dot((rj < ri).astype(BF16), sel.astype(BF16))
        row = start_ref[...] + rank
        dest = jnp.zeros(route.shape, F32)
        for kk, hit in enumerate(hits):
            dest = jnp.where(lane == kk, jnp.sum(jnp.where(hit, row, 0.0), axis=-1, keepdims=True), dest)
        dest_ref[...] = dest.astype(jnp.int32)

    carry_ref[...] += jnp.sum(sel, axis=0, keepdims=True)


def _route(route, tr=512):
    t = route.shape[0]
    return pl.pallas_call(
        _route_kernel,
        grid=(2, t // tr),
        in_specs=[pl.BlockSpec((tr, LANE), lambda p, i: (i, 0))],
        out_specs=[pl.BlockSpec((tr, LANE), lambda p, i: (i * p, 0)), pl.BlockSpec((1, LANE), lambda p, i: (0, 0))],
        out_shape=[jax.ShapeDtypeStruct((t, LANE), jnp.int32), jax.ShapeDtypeStruct((1, LANE), F32)],
        scratch_shapes=[pltpu.VMEM((1, LANE), F32), pltpu.VMEM((1, LANE), F32)],
        compiler_params=_params(("arbitrary", "arbitrary")),
        name="moe_route",
    )(route)


def _row_copy(src, src_row, dst, dst_row, sem):
    src_at = pl.ds(pl.multiple_of(src_row * SUBLANES, SUBLANES), SUBLANES)
    dst_at = pl.ds(pl.multiple_of(dst_row * SUBLANES, SUBLANES), SUBLANES)
    return pltpu.make_async_copy(src.at[src_at], dst.at[dst_at], sem)


def _dispatch_kernel(dest_ref, h_ref, xs_in_ref, xs_ref, sem):
    del xs_in_ref
    i = pl.program_id(0)
    td = dest_ref.shape[0] // TOP_K

    def issue(r, carry):
        for kk in range(TOP_K):
            _row_copy(h_ref, i * td + r, xs_ref, dest_ref[r * TOP_K + kk], sem).start(priority=kk % 2)
        return carry

    def drain(r, carry):
        for kk in range(TOP_K):
            _row_copy(h_ref, 0, xs_ref, 0, sem).wait()
        return carry

    lax.fori_loop(0, td, issue, 0, unroll=8)

    @pl.when(i > 0)
    def _():
        lax.fori_loop(0, td, drain, 0, unroll=8)

    @pl.when(i == pl.num_programs(0) - 1)
    def _():
        lax.fori_loop(0, td, drain, 0, unroll=8)


def _dispatch(dest_flat, h, xs_zero, td=256):
    t = h.shape[0] // SUBLANES
    return pl.pallas_call(
        _dispatch_kernel,
        grid=(t // td,),
        in_specs=[
            pl.BlockSpec((td * TOP_K,), lambda i: (i,), memory_space=pltpu.SMEM),
            pl.BlockSpec(memory_space=pl.ANY),
            pl.BlockSpec(memory_space=pl.ANY),
        ],
        out_specs=pl.BlockSpec(memory_space=pl.ANY),
        out_shape=jax.ShapeDtypeStruct(xs_zero.shape, xs_zero.dtype),
        scratch_shapes=[pltpu.SemaphoreType.DMA],
        input_output_aliases={2: 0},
        compiler_params=_params(("arbitrary",)),
        name="moe_dispatch",
    )(dest_flat, h, xs_zero)


def _experts_kernel(te_ref, nu_ref, xs_ref, w1_ref, b1g_ref, b1l_ref, w2_ref, b2_ref, y_ref, w1b_ref, w2b_ref):
    j = pl.program_id(0)

    @pl.when(j < nu_ref[0])
    def _():
        @pl.when((j == 0) | (te_ref[j] != te_ref[jnp.maximum(j - 1, 0)]))
        def _():
            w1b_ref[...] = w1_ref[0].astype(BF16)
            w2b_ref[...] = w2_ref[0].astype(BF16)

        x = _load_slabs(xs_ref, MOE_TILE).astype(BF16)
        hid_g, hid_l = _deinterleave(_dot(x, w1b_ref[...]))
        x_glu = jnp.minimum(hid_g + b1g_ref[0], SWIGLU_LIMIT)
        x_lin = jnp.clip(hid_l + b1l_ref[0], -SWIGLU_LIMIT, SWIGLU_LIMIT)
        act = x_glu * jax.nn.sigmoid(SWIGLU_ALPHA * x_glu) * (x_lin + 1.0)
        _store_slabs(y_ref, _dot(act.astype(BF16), w2b_ref[...]) + b2_ref[0])

    @pl.when(j >= nu_ref[0])
    def _():
        y_ref[...] = jnp.zeros_like(y_ref)


def _experts(tile_expert, n_used, xs, w1, b1g, b1l, w2, b2):
    rows = xs.shape[0] // SUBLANES
    n_e, ff, d = w2.shape
    tg = MOE_TILE
    used = lambda j, te, nu: jnp.minimum(j, nu[0] - 1)
    exp_of = lambda j, te, nu: te[used(j, te, nu)]
    return pl.pallas_call(
        _experts_kernel,
        grid_spec=pltpu.PrefetchScalarGridSpec(
            num_scalar_prefetch=2,
            grid=(rows // tg,),
            in_specs=[
                pl.BlockSpec((tg * SUBLANES, LANE), lambda j, te, nu: (used(j, te, nu), 0)),
                pl.BlockSpec((1, d, 2 * ff), lambda j, te, nu: (exp_of(j, te, nu), 0, 0)),
                pl.BlockSpec((1, 1, ff), lambda j, te, nu: (exp_of(j, te, nu), 0, 0)),
                pl.BlockSpec((1, 1, ff), lambda j, te, nu: (exp_of(j, te, nu), 0, 0)),
                pl.BlockSpec((1, ff, d), lambda j, te, nu: (exp_of(j, te, nu), 0, 0)),
                pl.BlockSpec((1, 1, d), lambda j, te, nu: (exp_of(j, te, nu), 0, 0)),
            ],
            out_specs=pl.BlockSpec((tg * SUBLANES, LANE), lambda j, te, nu: (j, 0)),
            scratch_shapes=[pltpu.VMEM((d, 2 * ff), BF16), pltpu.VMEM((ff, d), BF16)],
        ),
        out_shape=jax.ShapeDtypeStruct((rows * SUBLANES, LANE), F32),
        compiler_params=_params(("arbitrary",)),
        name="moe_experts",
    )(tile_expert, n_used, xs, w1, b1g, b1l, w2, b2)


def _combine_kernel(dest_ref, dest_next_ref, route_ref, x1_ref, y_ref, o_ref, buf_ref, sems):
    i = pl.program_id(0)
    tc = x1_ref.shape[0]
    slot = i % 2

    def gather(dref, to_slot):
        def issue(r, carry):
            for kk in range(TOP_K):
                _row_copy(y_ref, dref[r * TOP_K + kk], buf_ref.at[to_slot, kk], r, sems.at[to_slot]).start(
                    priority=kk % 2)
            return carry

        lax.fori_loop(0, tc, issue, 0, unroll=8)

    @pl.when(i == 0)
    def _():
        gather(dest_ref, slot)

    @pl.when(i + 1 < pl.num_programs(0))
    def _():
        gather(dest_next_ref, 1 - slot)

    def drain(r, carry):
        for kk in range(TOP_K):
            _row_copy(y_ref, 0, buf_ref.at[slot, kk], 0, sems.at[slot]).wait()
        return carry

    lax.fori_loop(0, tc, drain, 0, unroll=8)

    route = route_ref[...]
    acc = x1_ref[...]
    for kk in range(TOP_K):
        acc = acc + route[:, TOP_K + kk:TOP_K + kk + 1] * _load_slabs(buf_ref, tc, lead=(slot, kk))
    o_ref[...] = acc


def _combine(dest_flat, route, x1, y, tc=256):
    t, d = x1.shape
    last = t // tc - 1
    return pl.pallas_call(
        _combine_kernel,
        grid=(t // tc,),
        in_specs=[
            pl.BlockSpec((tc * TOP_K,), lambda i: (i,), memory_space=pltpu.SMEM),
            pl.BlockSpec((tc * TOP_K,), lambda i: (jnp.minimum(i + 1, last),), memory_space=pltpu.SMEM),
            pl.BlockSpec((tc, LANE), lambda i: (i, 0)),
            pl.BlockSpec((tc, d), lambda i: (i, 0)),
            pl.BlockSpec(memory_space=pl.ANY),
        ],
        out_specs=pl.BlockSpec((tc, d), lambda i: (i, 0)),
        out_shape=jax.ShapeDtypeStruct((t, d), F32),
        scratch_shapes=[pltpu.VMEM((2, TOP_K, tc * SUBLANES, LANE), F32), pltpu.SemaphoreType.DMA((2,))],
        compiler_params=_params(("arbitrary",)),
        name="moe_combine",
    )(dest_flat, dest_flat, route, x1, y)


def _moe(h, route, x1, w1, b1g, b1l, w2, b2):
    t, d = x1.shape
    assert d == SUBLANES * LANE
    n_e = w2.shape[0]
    n_tiles = (t * TOP_K) // MOE_TILE + n_e
    dest, counts = _route(route)
    tiles_per = jnp.ceil(counts[0, :n_e] * (1.0 / MOE_TILE)).astype(jnp.int32)
    tile_end = jnp.cumsum(tiles_per)
    past = (tile_end[None, :] <= jnp.arange(n_tiles, dtype=jnp.int32)[:, None]).astype(jnp.int32)
    tile_expert = jnp.minimum(jnp.sum(past, axis=1), n_e - 1).astype(jnp.int32)
    n_used = tile_end[-1:].astype(jnp.int32)
    dest_flat = dest[:, :TOP_K].reshape(-1)
    xs = _dispatch(dest_flat, h, jnp.zeros((n_tiles * MOE_TILE * SUBLANES, LANE), F32))
    y = _experts(tile_expert, n_used, xs, w1, b1g, b1l, w2, b2)
    return _combine(dest_flat, route, x1, y)


def _pad_cols(w, width):
    return jnp.pad(w, ((0, 0), (0, width - w.shape[1])))


def _rope_tables(pos, reps):
    inv = ROPE_THETA ** (-jnp.arange(0, HEAD_DIM, 2, dtype=F32) / HEAD_DIM)
    ang = pos.astype(F32)[:, None] * inv[None, :]
    ang = jnp.concatenate([ang, ang], axis=-1)
    sign = jnp.concatenate([-jnp.ones((HEAD_DIM // 2,), F32), jnp.ones((HEAD_DIM // 2,), F32)])
    return jnp.tile(jnp.cos(ang), (1, reps)), jnp.tile(jnp.sin(ang) * sign, (1, reps))


def _layer(x, mem, g_mix, g_mem, w_in, b_merge, nsa_q_norm, nsa_k_norm, cmp_pe, cmp_w1, cmp_b1, cmp_w2,
           rwkv_shift_mix, rwkv_w0, rwkv_w_up, rwkv_a0, rwkv_a_up, rwkv_g_up, rwkv_k_k, rwkv_k_a, rwkv_r_k,
           rwkv_ln_w, rwkv_ln_b, mem_w_kv, mem_q_norm, mem_k_norm, w_branch, w_out, g_ffn,
           router_w, router_b, exp_w1, exp_b1, exp_w2, exp_b2, cos, sin, cos_c, sin_c):
    b, s, d = x.shape
    t = b * s
    x2 = x.reshape(t, d)

    o = 0
    parts = []
    for width in (NSA_WIDTH, 6 * NSA_KV_WIDTH, 3 * NSA_HEADS,
                  3 * RWKV_WIDTH + DECAY_RANK + AAA_RANK + GATE_RANK, MEM_WIDTH, N_BRANCHES * D_MODEL):
        parts.append(w_in[:, o:o + width])
        o += width
    w_q, w_kv, w_gate, w_rwkv, w_qm, w_merge = parts

    def rwkv_layout(m):
        r3 = m[:, :3 * RWKV_WIDTH]
        xw = m[:, 3 * RWKV_WIDTH:3 * RWKV_WIDTH + DECAY_RANK]
        xa = m[:, 3 * RWKV_WIDTH + DECAY_RANK:3 * RWKV_WIDTH + DECAY_RANK + AAA_RANK]
        xg = m[:, 3 * RWKV_WIDTH + DECAY_RANK + AAA_RANK:]
        return _pad_cols(jnp.concatenate([r3, _pad_cols(xw, LANE), _pad_cols(xa, LANE), xg], axis=1), RWKV_PAD)

    w_all = jnp.concatenate([rwkv_layout(w_rwkv), w_merge, w_q, w_qm, w_kv, w_gate], axis=1)
    w_all = _pad_cols(w_all, IN_PAD).astype(BF16)
    proj2 = _inproj(x2, g_mix.reshape(1, d), w_all)
    proj3 = proj2.reshape(b, s, IN_PAD)

    rowv = lambda a: a.reshape(1, -1)
    pad_rows = lambda m: jnp.pad(m, ((0, LANE - m.shape[0]), (0, 0)))
    o_b = _rwkv(proj3, rwkv_layout(rowv(rwkv_shift_mix)), rowv(rwkv_w0), pad_rows(rwkv_w_up), rowv(rwkv_a0),
                pad_rows(rwkv_a_up), rwkv_g_up, rowv(rwkv_k_k), rowv(rwkv_k_a), rowv(rwkv_r_k),
                rowv(rwkv_ln_w), rowv(rwkv_ln_b))

    q_gain = jnp.tile(nsa_q_norm.reshape(1, HEAD_DIM), (1, NSA_HEADS))
    k_gain = jnp.tile(nsa_k_norm, (1, NSA_KV_HEADS))
    qn, ks, vs, kw, vw = _nsa_prep(proj3, cos, sin, q_gain, k_gain)
    kc, vc = _nsa_cmp(proj3, cos_c, sin_c, k_gain, cmp_pe, cmp_w1.astype(BF16),
                      cmp_b1.reshape(2, 1, CMP_HIDDEN), cmp_w2.astype(BF16))
    o_a = _nsa_attn(qn, proj3, kc, vc, ks, vs, kw, vw)

    mk, mv = _mem_kv(mem, g_mem.reshape(1, d), mem_w_kv.astype(BF16), mem_k_norm.reshape(1, MEM_HEAD_DIM))
    o_m = _mem_attn(proj3, mem_q_norm.reshape(1, MEM_HEAD_DIM), mk, mv)

    rw = _pad_cols(router_w, LANE)
    rb = jnp.concatenate([router_b, jnp.full((LANE - N_EXPERTS,), NEG_BIG, F32)]).reshape(1, LANE)
    x1, h2, route = _merge(o_a.reshape(t, NSA_WIDTH), o_b.reshape(t, RWKV_WIDTH), o_m.reshape(t, MEM_WIDTH),
                          proj2, x2, b_merge.reshape(1, -1), w_branch.astype(BF16), w_out.astype(BF16),
                          g_ffn.reshape(1, d), rw, rb)

    b1g = exp_b1[:, None, 0::2]
    b1l = exp_b1[:, None, 1::2]
    out = _moe(h2, route, x1, exp_w1, b1g, b1l, exp_w2, exp_b2[:, None, :])
    return out.reshape(b, s, d)


def kernel(x, mem, g_mix, g_mem, w_in, b_merge, nsa_q_norm, nsa_k_norm, cmp_pe, cmp_w1, cmp_b1, cmp_w2,
           rwkv_shift_mix, rwkv_w0, rwkv_w_up, rwkv_a0, rwkv_a_up, rwkv_g_up, rwkv_k_k, rwkv_k_a, rwkv_r_k,
           rwkv_ln_w, rwkv_ln_b, mem_w_kv, mem_q_norm, mem_k_norm, w_branch, w_out, g_ffn,
           router_w, router_b, exp_w1, exp_b1, exp_w2, exp_b2):
    s = x.shape[1]
    cos, sin = _rope_tables(jnp.arange(s), NSA_HEADS)
    n_cmp = (s - CMP_LEN) // CMP_STRIDE + 1
    cos_c, sin_c = _rope_tables(jnp.arange(n_cmp + 1) * CMP_STRIDE + CMP_LEN - 1, NSA_KV_HEADS)
    depth = g_mix.shape[0]
    for l in range(depth):
        x = _layer(x, mem, g_mix[l], g_mem[l], w_in[l], b_merge[l], nsa_q_norm[l], nsa_k_norm[l], cmp_pe[l],
                   cmp_w1[l], cmp_b1[l], cmp_w2[l], rwkv_shift_mix[l], rwkv_w0[l], rwkv_w_up[l], rwkv_a0[l],
                   rwkv_a_up[l], rwkv_g_up[l], rwkv_k_k[l], rwkv_k_a[l], rwkv_r_k[l], rwkv_ln_w[l], rwkv_ln_b[l],
                   mem_w_kv[l], mem_q_norm[l], mem_k_norm[l], w_branch[l], w_out[l], g_ffn[l], router_w[l],
                   router_b[l], exp_w1[l], exp_b1[l], exp_w2[l], exp_b2[l], cos, sin, cos_c, sin_c)
    return x
```

```python
import functools

import jax
import jax.numpy as jnp
from jax import lax
from jax.experimental import pallas as pl
from jax.experimental.pallas import tpu as pltpu

F32 = jnp.float32
BF16 = jnp.bfloat16
HI = lax.Precision.HIGHEST

D_MODEL = 1024
HEAD_DIM = 64
NSA_HEADS = 8
NSA_KV_HEADS = 2
NSA_GROUP = NSA_HEADS // NSA_KV_HEADS
NSA_WIDTH = NSA_HEADS * HEAD_DIM
NSA_KV_WIDTH = NSA_KV_HEADS * HEAD_DIM
CMP_LEN = 32
CMP_STRIDE = 16
CMP_HIDDEN = 128
SEL_BLOCK = 64
TOP_N = 8
WINDOW = 512
Q_BLOCK = 128
FORCE_BONUS = 1000.0
RWKV_HEADS = 8
RWKV_HEAD_DIM = 64
RWKV_WIDTH = RWKV_HEADS * RWKV_HEAD_DIM
DECAY_RANK = 64
AAA_RANK = 64
GATE_RANK = 128
GN_EPS = 64e-5
MEM_HEADS = 4
MEM_HEAD_DIM = 128
MEM_WIDTH = MEM_HEADS * MEM_HEAD_DIM
N_BRANCHES = 3
BRANCH_WIDTH = 512
N_EXPERTS = 32
TOP_K = 4
EXPERT_FF = 1024
SWIGLU_ALPHA = 1.702
SWIGLU_LIMIT = 7.0
ROPE_THETA = 10000.0
RMS_EPS = 1e-6

LANE = 128
LOG2_E = 1.4426950408889634
NEG_BIG = -1e30

RWKV_PAD = 2048
COL_RWKV = 0
COL_MERGE = COL_RWKV + RWKV_PAD
COL_QNSA = COL_MERGE + N_BRANCHES * D_MODEL
COL_QMEM = COL_QNSA + NSA_WIDTH
COL_KV = COL_QMEM + MEM_WIDTH
COL_GATE = COL_KV + 6 * NSA_KV_WIDTH
IN_PAD = 7168
RW_R, RW_K, RW_V, RW_XW, RW_XA, RW_XG = 0, 512, 1024, 1536, 1664, 1792

RWKV_CHUNK = 64
VMEM_LIMIT = 56 * 1024 * 1024


def _dot(a, b, prec=None):
    return jnp.dot(a, b, preferred_element_type=F32, precision=prec)


def _dot_nt(a, b, prec=None):
    return lax.dot_general(a, b, (((1,), (1,)), ((), ())), preferred_element_type=F32, precision=prec)


def _dot_tn(a, b, prec=None):
    return lax.dot_general(a, b, (((0,), (0,)), ((), ())), preferred_element_type=F32, precision=prec)


def _split3(x):
    hi = x.astype(BF16)
    r1 = x - hi.astype(F32)
    mid = r1.astype(BF16)
    lo = (r1 - mid.astype(F32)).astype(BF16)
    return hi, mid, lo


def _dot01_left(m01, x):
    n = x.shape[1]
    out = _dot(m01, jnp.concatenate(_split3(x), axis=1))
    return out[:, :n] + out[:, n:2 * n] + out[:, 2 * n:]


def _dot01_right(x, m01):
    m = x.shape[0]
    out = _dot(jnp.concatenate(_split3(x), axis=0), m01)
    return out[:m] + out[m:2 * m] + out[2 * m:]


def _seg_matrix(width, seg):
    r = lax.broadcasted_iota(jnp.int32, (width, width), 0) // seg
    c = lax.broadcasted_iota(jnp.int32, (width, width), 1) // seg
    return (r == c).astype(F32)


SUBLANES = 8


def _store_slabs(ref, x, lead=()):
    rows = x.shape[0]
    for s in range(SUBLANES):
        ref[lead + (pl.ds(s, rows, stride=SUBLANES), slice(None))] = x[:, s * LANE:(s + 1) * LANE]


def _load_slabs(ref, rows, lead=()):
    return jnp.concatenate(
        [ref[lead + (pl.ds(s, rows, stride=SUBLANES), slice(None))] for s in range(SUBLANES)], axis=1)


def _params(sem):
    return pltpu.CompilerParams(dimension_semantics=sem, vmem_limit_bytes=VMEM_LIMIT)


def _inproj_kernel(x_ref, g_ref, w_ref, o_ref, hn_ref):
    @pl.when(pl.program_id(1) == 0)
    def _():
        x = x_ref[...]
        ms = jnp.mean(x * x, axis=-1, keepdims=True)
        hn_ref[...] = (x * lax.rsqrt(ms + RMS_EPS) * g_ref[...]).astype(BF16)

    o_ref[...] = _dot(hn_ref[...], w_ref[...])


def _inproj(x2, g, w, tm=1024, tn=1792):
    t, d = x2.shape
    n = w.shape[1]
    return pl.pallas_call(
        _inproj_kernel,
        grid=(t // tm, n // tn),
        in_specs=[
            pl.BlockSpec((tm, d), lambda i, j: (i, 0)),
            pl.BlockSpec((1, d), lambda i, j: (0, 0)),
            pl.BlockSpec((d, tn), lambda i, j: (0, j)),
        ],
        out_specs=pl.BlockSpec((tm, tn), lambda i, j: (i, j)),
        out_shape=jax.ShapeDtypeStruct((t, n), F32),
        scratch_shapes=[pltpu.VMEM((tm, d), BF16)],
        compiler_params=_params(("parallel", "arbitrary")),
        name="inproj",
    )(x2, g, w)


def _rwkv_chunk_kernel(p_ref, pprev_ref, mix_ref, w0_ref, wup_ref, a0_ref, aup_ref, gup_ref, kk_ref, ka_ref,
                       rk_ref, rm_ref, y0_ref, bonus_ref, g_ref, gam_ref, m_ref, d0_ref):
    c = RWKV_CHUNK
    n = RWKV_HEAD_DIM
    rows = p_ref.shape[1]

    p = p_ref[0]
    row = lax.broadcasted_iota(jnp.int32, p.shape, 0)
    last_prev = jnp.where(pl.program_id(1) == 0, 0.0, 1.0) * pprev_ref[0, 7:8, :]
    prev = jnp.where(row == 0, last_prev, pltpu.roll(p, 1, axis=0))
    ps = p + (prev - p) * mix_ref[...]
    r = ps[:, RW_R:RW_R + RWKV_WIDTH]
    k = ps[:, RW_K:RW_K + RWKV_WIDTH]
    v = ps[:, RW_V:RW_V + RWKV_WIDTH]
    xw = ps[:, RW_XW:RW_XW + LANE]
    xa = ps[:, RW_XA:RW_XA + LANE]
    xg = ps[:, RW_XG:RW_XG + LANE]

    z = -(w0_ref[...] + _dot(jnp.tanh(xw), wup_ref[...], HI))
    softplus = jnp.maximum(z, 0.0) + jnp.log1p(jnp.exp(-jnp.abs(z)))
    w = -softplus - 0.5
    logw = -jnp.exp(w)
    a = jax.nn.sigmoid(a0_ref[...] + _dot(xa, aup_ref[...], HI))
    g_ref[0] = _dot(jax.nn.sigmoid(xg), gup_ref[...], HI)

    seg = _seg_matrix(RWKV_WIDTH, n).astype(BF16)
    kk = k * kk_ref[...]
    k = k * (1.0 + (a - 1.0) * ka_ref[...])
    sums = _dot01_right(jnp.concatenate([kk * kk, r * k * rk_ref[...]], axis=0), seg)
    kk = kk / jnp.maximum(jnp.sqrt(sums[:rows]), 1e-12)
    bonus_ref[0] = sums[rows:] * v
    kka = kk * a

    ti = lax.broadcasted_iota(jnp.int32, (2 * c, 2 * c), 0)
    tj = lax.broadcasted_iota(jnp.int32, (2 * c, 2 * c), 1)
    keep = (tj % c) < jnp.where(ti < c, ti, ti - c + 1)
    ci = lax.broadcasted_iota(jnp.int32, (c, c), 0)
    cj = lax.broadcasted_iota(jnp.int32, (c, c), 1)
    eye = (ci == cj).astype(F32)
    ltri = (cj <= ci).astype(BF16)
    zeros_cn = jnp.zeros((c, n), BF16)

    chains = []
    for j in range(rows // c):
        rs = slice(j * c, (j + 1) * c)
        lw = logw[rs]
        cum = _dot01_left(ltri, lw)
        tot = cum[c - 1:c, :]
        einv = jnp.exp(-cum)
        dec_end = jnp.exp(tot - cum)
        r_f = r[rs] * jnp.exp(cum)
        a_t = (-kk[rs] * jnp.exp(cum - lw)).astype(BF16)
        b_t = (kka[rs] * einv).astype(BF16)
        k_t = (k[rs] * einv).astype(BF16)
        r_t = r_f.astype(BF16)
        b_e = (kka[rs] * dec_end).astype(BF16)
        k_e = (k[rs] * dec_end).astype(BF16)
        v_b = v[rs].astype(BF16)
        gam_ref[0, j] = jnp.exp(tot)
        for h in range(RWKV_HEADS):
            sl = slice(h * n, (h + 1) * n)
            chains.append(dict(j=j, h=h, rs=rs, sl=sl, a=a_t[:, sl], r=r_t[:, sl], rf=r_f[:, sl], v=v_b[:, sl],
                               rhs=jnp.concatenate([b_t[:, sl], k_t[:, sl]], axis=0),
                               bke=jnp.concatenate([b_e[:, sl], k_e[:, sl]], axis=0)))

    for ch in chains:
        lhs = jnp.concatenate([ch["a"], ch["r"]], axis=0)
        ch["amat"] = jnp.where(keep, _dot_nt(lhs, ch["rhs"]), 0.0)
        ch["pw"] = ch["amat"][:c, :c]
        ch["tinv"] = eye + ch["pw"]
    for _ in range(5):
        for ch in chains:
            pw_b = ch["pw"].astype(BF16)
            ch["pw"] = _dot(pw_b, pw_b)
        for ch in chains:
            ch["tinv"] = ch["tinv"] + _dot(ch["tinv"].astype(BF16), ch["pw"].astype(BF16))
    for ch in chains:
        ch["akv"] = _dot(ch["amat"][:c, c:].astype(BF16), ch["v"])
    for ch in chains:
        wu = _dot(ch["tinv"].astype(BF16), jnp.concatenate([ch["a"], ch["akv"].astype(BF16)], axis=1)).astype(BF16)
        ch["x"] = jnp.concatenate([wu, jnp.concatenate([zeros_cn, ch["v"]], axis=1)], axis=0)
    for ch in chains:
        ry = _dot(ch["amat"][c:, :].astype(BF16), ch["x"])
        rm_ref[0, ch["rs"], ch["sl"]] = (ch["rf"] + ry[:, :n]).astype(BF16)
        y0_ref[0, ch["rs"], ch["sl"]] = ry[:, n:]
    for ch in chains:
        md = _dot_tn(ch["x"], ch["bke"])
        m_ref[0, ch["j"], ch["h"]] = md[:n].astype(BF16)
        d0_ref[0, ch["j"], ch["h"]] = md[n:]


def _rwkv_chunks(proj3, mix, w0, wup, a0, aup, gup, k_k, k_a, r_k, rows=256):
    b, s, _ = proj3.shape
    c = RWKV_CHUNK
    nc = s // c
    cps = rows // c
    vec = lambda width: pl.BlockSpec((1, width), lambda i, t: (0, 0))
    mat = lambda nrows: pl.BlockSpec((nrows, RWKV_WIDTH), lambda i, t: (0, 0))
    tok = pl.BlockSpec((1, rows, RWKV_WIDTH), lambda i, t: (i, t, 0))
    sq = pl.BlockSpec((1, cps, RWKV_HEADS, RWKV_HEAD_DIM, RWKV_HEAD_DIM), lambda i, t: (i, t, 0, 0, 0))
    tok_shape = lambda dt: jax.ShapeDtypeStruct((b, s, RWKV_WIDTH), dt)
    sq_shape = lambda dt: jax.ShapeDtypeStruct((b, nc, RWKV_HEADS, RWKV_HEAD_DIM, RWKV_HEAD_DIM), dt)
    return pl.pallas_call(
        _rwkv_chunk_kernel,
        grid=(b, s // rows),
        in_specs=[
            pl.BlockSpec((1, rows, RWKV_PAD), lambda i, t: (i, t, COL_RWKV // RWKV_PAD)),
            pl.BlockSpec((1, 8, RWKV_PAD), lambda i, t: (i, jnp.maximum(t * (rows // 8) - 1, 0), COL_RWKV // RWKV_PAD)),
            vec(RWKV_PAD), vec(RWKV_WIDTH), mat(LANE), vec(RWKV_WIDTH), mat(LANE), mat(LANE),
            vec(RWKV_WIDTH), vec(RWKV_WIDTH), vec(RWKV_WIDTH),
        ],
        out_specs=[tok, tok, tok, tok,
                   pl.BlockSpec((1, cps, 1, RWKV_WIDTH), lambda i, t: (i, t, 0, 0)), sq, sq],
        out_shape=[tok_shape(BF16), tok_shape(F32), tok_shape(F32), tok_shape(F32),
                   jax.ShapeDtypeStruct((b, nc, 1, RWKV_WIDTH), F32), sq_shape(BF16), sq_shape(F32)],
        compiler_params=_params(("parallel", "parallel")),
        name="rwkv7_chunks",
    )(proj3, proj3, mix, w0, wup, a0, aup, gup, k_k, k_a, r_k)


def _rwkv_scan_kernel(rm_ref, y0_ref, bonus_ref, g_ref, gam_ref, m_ref, d0_ref, lnw_ref, lnb_ref,
                      o_ref, state_ref, y_ref):
    c = RWKV_CHUNK
    n = RWKV_HEAD_DIM

    @pl.when(pl.program_id(1) == 0)
    def _():
        state_ref[...] = jnp.zeros_like(state_ref)

    for j in range(gam_ref.shape[1]):
        rs = slice(j * c, (j + 1) * c)
        gam = gam_ref[0, j]
        for h in range(RWKV_HEADS):
            sl = slice(h * n, (h + 1) * n)
            s = state_ref[h]
            s_b = s.astype(BF16)
            y_ref[rs, sl] = _dot_nt(rm_ref[0, rs, sl], s_b) + y0_ref[0, rs, sl]
            state_ref[h] = s * gam[:, sl] + _dot(s_b, m_ref[0, j, h]) + d0_ref[0, j, h]

    seg = _seg_matrix(RWKV_WIDTH, n).astype(BF16)
    y = y_ref[...]
    mu = _dot01_right(y, seg) * (1.0 / n)
    d = y - mu
    var = _dot01_right(d * d, seg) * (1.0 / n)
    yn = d * lax.rsqrt(var + GN_EPS) * lnw_ref[...] + lnb_ref[...]
    o_ref[0] = (yn + bonus_ref[0]) * g_ref[0]


def _rwkv_scan(rm, y0, bonus, g, gam, m, d0, ln_w, ln_b, rows=256):
    b, s, _ = rm.shape
    cps = rows // RWKV_CHUNK
    vec = pl.BlockSpec((1, RWKV_WIDTH), lambda i, t: (0, 0))
    tok = pl.BlockSpec((1, rows, RWKV_WIDTH), lambda i, t: (i, t, 0))
    sq = pl.BlockSpec((1, cps, RWKV_HEADS, RWKV_HEAD_DIM, RWKV_HEAD_DIM), lambda i, t: (i, t, 0, 0, 0))
    return pl.pallas_call(
        _rwkv_scan_kernel,
        grid=(b, s // rows),
        in_specs=[tok, tok, tok, tok, pl.BlockSpec((1, cps, 1, RWKV_WIDTH), lambda i, t: (i, t, 0, 0)), sq, sq,
                  vec, vec],
        out_specs=tok,
        out_shape=jax.ShapeDtypeStruct((b, s, RWKV_WIDTH), F32),
        scratch_shapes=[
            pltpu.VMEM((RWKV_HEADS, RWKV_HEAD_DIM, RWKV_HEAD_DIM), F32),
            pltpu.VMEM((rows, RWKV_WIDTH), F32),
        ],
        compiler_params=_params(("parallel", "arbitrary")),
        name="rwkv7_scan",
    )(rm, y0, bonus, g, gam, m, d0, ln_w, ln_b)


def _rwkv(proj3, mix, w0, wup, a0, aup, gup, k_k, k_a, r_k, ln_w, ln_b):
    rm, y0, bonus, g, gam, m, d0 = _rwkv_chunks(proj3, mix, w0, wup, a0, aup, gup, k_k, k_a, r_k)
    return _rwkv_scan(rm, y0, bonus, g, gam, m, d0, ln_w, ln_b)


def _rope(x, cos, sin_signed):
    w = x.shape[-1]
    first_half = (lax.broadcasted_iota(jnp.int32, x.shape, 1) % HEAD_DIM) < (HEAD_DIM // 2)
    rot = jnp.where(first_half, pltpu.roll(x, w - HEAD_DIM // 2, axis=1), pltpu.roll(x, HEAD_DIM // 2, axis=1))
    return x * cos + rot * sin_signed


def _head_rmsnorm(x, gain, seg):
    ms = _dot(x * x, seg, HI) * (1.0 / HEAD_DIM)
    return x * lax.rsqrt(ms + RMS_EPS) * gain


def _split_groups(x):
    return [x[:, g * HEAD_DIM:(g + 1) * HEAD_DIM] for g in range(NSA_KV_HEADS)]


def _nsa_prep_kernel(q_ref, ksl_ref, vsl_ref, kwn_ref, vwn_ref, cos_ref, sin_ref, qn_ref, kn_ref,
                     qo_ref, ks_ref, vs_ref, kw_ref, vw_ref):
    seg_q = _seg_matrix(NSA_WIDTH, HEAD_DIM)
    seg_k = _seg_matrix(NSA_KV_WIDTH, HEAD_DIM)
    cos_k = cos_ref[:, :NSA_KV_WIDTH]
    sin_k = sin_ref[:, :NSA_KV_WIDTH]

    q = _rope(_head_rmsnorm(q_ref[0], qn_ref[...], seg_q), cos_ref[...], sin_ref[...])
    q_t = (q * (HEAD_DIM ** -0.5 * LOG2_E)).T
    for h in range(NSA_HEADS):
        qo_ref[0, h] = q_t[h * HEAD_DIM:(h + 1) * HEAD_DIM].astype(BF16)

    ks = _rope(_head_rmsnorm(ksl_ref[0], kn_ref[1:2, :], seg_k), cos_k, sin_k)
    kw = _rope(_head_rmsnorm(kwn_ref[0], kn_ref[2:3, :], seg_k), cos_k, sin_k)
    vs_t = vsl_ref[0].T
    vw_t = vwn_ref[0].T
    for g, (a, c_) in enumerate(zip(_split_groups(ks), _split_groups(kw))):
        ks_ref[0, g] = a.astype(BF16)
        kw_ref[0, g] = c_.astype(BF16)
        vs_ref[0, g] = vs_t[g * HEAD_DIM:(g + 1) * HEAD_DIM].astype(BF16)
        vw_ref[0, g] = vw_t[g * HEAD_DIM:(g + 1) * HEAD_DIM].astype(BF16)


def _nsa_prep(proj3, cos, sin_signed, q_norm, k_norm, tq=512):
    b, s, _ = proj3.shape
    kvw = NSA_KV_WIDTH
    kv_spec = lambda j: pl.BlockSpec((1, tq, kvw), lambda i, t: (i, t, COL_KV // kvw + j))
    out_k = pl.BlockSpec((1, NSA_KV_HEADS, tq, HEAD_DIM), lambda i, t: (i, 0, t, 0))
    out_vt = pl.BlockSpec((1, NSA_KV_HEADS, HEAD_DIM, tq), lambda i, t: (i, 0, 0, t))
    k_shape = jax.ShapeDtypeStruct((b, NSA_KV_HEADS, s, HEAD_DIM), BF16)
    vt_shape = jax.ShapeDtypeStruct((b, NSA_KV_HEADS, HEAD_DIM, s), BF16)
    return pl.pallas_call(
        _nsa_prep_kernel,
        grid=(b, s // tq),
        in_specs=[
            pl.BlockSpec((1, tq, NSA_WIDTH), lambda i, t: (i, t, COL_QNSA // NSA_WIDTH)),
            kv_spec(2), kv_spec(3), kv_spec(4), kv_spec(5),
            pl.BlockSpec((tq, NSA_WIDTH), lambda i, t: (t, 0)),
            pl.BlockSpec((tq, NSA_WIDTH), lambda i, t: (t, 0)),
            pl.BlockSpec((1, NSA_WIDTH), lambda i, t: (0, 0)),
            pl.BlockSpec((3, kvw), lambda i, t: (0, 0)),
        ],
        out_specs=[
            pl.BlockSpec((1, NSA_HEADS, HEAD_DIM, tq), lambda i, t: (i, 0, 0, t)),
            out_k, out_vt, out_k, out_vt,
        ],
        out_shape=[jax.ShapeDtypeStruct((b, NSA_HEADS, HEAD_DIM, s), BF16), k_shape, vt_shape, k_shape, vt_shape],
        compiler_params=_params(("parallel", "parallel")),
        name="nsa_prep",
    )(proj3, proj3, proj3, proj3, proj3, cos, sin_signed, q_norm, k_norm)


def _gelu_tanh(x):
    return 0.5 * x * (1.0 + jnp.tanh(0.7978845608028654 * (x + 0.044715 * x * x * x)))


def _nsa_cmp_kernel(kc_in_ref, vc_in_ref, cos_ref, sin_ref, kn_ref, pe_ref, w1_ref, b1_ref, w2_ref,
                    kc_ref, vc_ref):
    n_cmp = (kc_in_ref.shape[1] - CMP_LEN) // CMP_STRIDE + 1
    n_pad = n_cmp + 1
    zero_row = jnp.zeros((1, NSA_KV_WIDTH), F32)
    outs = []
    for j, src in enumerate((kc_in_ref, vc_in_ref)):
        acc = jnp.zeros((NSA_KV_HEADS * n_pad, CMP_HIDDEN), F32)
        for l in range(CMP_LEN):
            x = src[0, pl.ds(l, n_cmp, stride=CMP_STRIDE), :]
            x = jnp.concatenate([x, zero_row], axis=0)
            xg = jnp.concatenate(_split_groups(x), axis=0) + pe_ref[j, l:l + 1, :]
            acc = acc + _dot(xg.astype(BF16), w1_ref[j, l * HEAD_DIM:(l + 1) * HEAD_DIM, :])
        hid = _gelu_tanh(acc + b1_ref[j])
        out = _dot(hid.astype(BF16), w2_ref[j])
        outs.append(jnp.concatenate([out[g * n_pad:(g + 1) * n_pad] for g in range(NSA_KV_HEADS)], axis=1))
    kc, vc = outs
    seg_k = _seg_matrix(NSA_KV_WIDTH, HEAD_DIM)
    kc = _rope(_head_rmsnorm(kc, kn_ref[0:1, :], seg_k), cos_ref[...], sin_ref[...])
    vc_t = vc.T
    for g, a in enumerate(_split_groups(kc)):
        kc_ref[0, g] = a.astype(BF16)
        vc_ref[0, g] = vc_t[g * HEAD_DIM:(g + 1) * HEAD_DIM].astype(BF16)


def _nsa_cmp(proj3, cos_c, sin_c, k_norm, pe, w1, b1, w2):
    b, s, _ = proj3.shape
    kvw = NSA_KV_WIDTH
    n_pad = (s - CMP_LEN) // CMP_STRIDE + 2
    full = lambda shape: pl.BlockSpec(shape, lambda i: (0,) * len(shape))
    k_spec = pl.BlockSpec((1, NSA_KV_HEADS, n_pad, HEAD_DIM), lambda i: (i, 0, 0, 0))
    vt_spec = pl.BlockSpec((1, NSA_KV_HEADS, HEAD_DIM, n_pad), lambda i: (i, 0, 0, 0))
    k_shape = jax.ShapeDtypeStruct((b, NSA_KV_HEADS, n_pad, HEAD_DIM), BF16)
    vt_shape = jax.ShapeDtypeStruct((b, NSA_KV_HEADS, HEAD_DIM, n_pad), BF16)
    return pl.pallas_call(
        _nsa_cmp_kernel,
        grid=(b,),
        in_specs=[
            pl.BlockSpec((1, s, kvw), lambda i: (i, 0, COL_KV // kvw)),
            pl.BlockSpec((1, s, kvw), lambda i: (i, 0, COL_KV // kvw + 1)),
            full((n_pad, kvw)), full((n_pad, kvw)), full((3, kvw)),
            full((2, CMP_LEN, HEAD_DIM)), full((2, CMP_LEN * HEAD_DIM, CMP_HIDDEN)),
            full((2, 1, CMP_HIDDEN)), full((2, CMP_HIDDEN, HEAD_DIM)),
        ],
        out_specs=[k_spec, vt_spec],
        out_shape=[k_shape, vt_shape],
        compiler_params=_params(("parallel",)),
        name="nsa_compress",
    )(proj3, proj3, cos_c, sin_c, k_norm, pe, w1, b1, w2)


def _nsa_attn_kernel(q_ref, gate_ref, kc_ref, vc_ref, ks_ref, vs_ref, kw_ref, vw_ref, o_ref):
    qb = Q_BLOCK
    hg = NSA_GROUP
    g = pl.program_id(1)
    c = pl.program_id(2)
    cols = hg * qb
    qt = jnp.concatenate([q_ref[0, h] for h in range(hg)], axis=1)
    tile_heads = lambda x: jnp.concatenate([x] * hg, axis=1)

    n_cp = kc_ref.shape[2]
    s_c = _dot(kc_ref[0, 0], qt)
    cend = lax.broadcasted_iota(jnp.int32, (n_cp, qb), 0) * CMP_STRIDE + CMP_LEN - 1
    cmask = tile_heads(jnp.where(cend <= c * qb + lax.broadcasted_iota(jnp.int32, (n_cp, qb), 1), 1, 0)) > 0
    s_m = jnp.where(cmask, s_c, -jnp.inf)
    m = jnp.max(s_m, axis=0, keepdims=True)
    m = jnp.where(m == -jnp.inf, 0.0, m)
    e = jnp.where(cmask, jnp.exp2(s_m - m), 0.0)
    p_c = e / jnp.maximum(jnp.sum(e, axis=0, keepdims=True), 1e-30)
    o_cmp = _dot(vc_ref[0, 0], p_c.astype(BF16))

    n_blk = ks_ref.shape[2] // SEL_BLOCK
    p_sum = p_c[:, 0:qb]
    for h in range(1, hg):
        p_sum = p_sum + p_c[:, h * qb:(h + 1) * qb]
    bi = lax.broadcasted_iota(jnp.int32, (n_blk, n_cp), 0) * SEL_BLOCK
    ci = lax.broadcasted_iota(jnp.int32, (n_blk, n_cp), 1) * CMP_STRIDE
    cover_t = ((ci < bi + SEL_BLOCK) & (ci + CMP_LEN > bi)).astype(F32)
    imp = _dot(cover_t, p_sum, HI)
    blk = lax.broadcasted_iota(jnp.int32, (n_blk, qb), 0)
    qpos = c * qb + lax.broadcasted_iota(jnp.int32, (n_blk, qb), 1)
    cur = qpos // SEL_BLOCK
    forced = (blk == 0) | (blk == cur) | (blk == cur - 1)
    imp = jnp.where(forced, imp + FORCE_BONUS, imp)
    imp = jnp.where(blk <= cur, imp, -jnp.inf)
    rank = jnp.zeros((n_blk, qb), F32)
    for j in range(n_blk):
        other = imp[j:j + 1, :]
        rank = rank + jnp.where(blk > j, jnp.where(other >= imp, 1.0, 0.0), jnp.where(other > imp, 1.0, 0.0))
    sel_t = jnp.where(rank < min(TOP_N, n_blk), 1.0, 0.0).astype(BF16)

    kt_w = 4 * LANE
    kq = c * qb + lax.broadcasted_iota(jnp.int32, (kt_w, qb), 1)
    krel = lax.broadcasted_iota(jnp.int32, (kt_w, qb), 0)
    er = lax.broadcasted_iota(jnp.int32, (kt_w, n_blk), 0) // SEL_BLOCK
    ec = lax.broadcasted_iota(jnp.int32, (kt_w, n_blk), 1)

    def sel_body(kt, carry):
        m_i, l_i, acc = carry
        off = pl.multiple_of(kt * kt_w, kt_w)
        expand = jnp.where(ec == er + kt * (kt_w // SEL_BLOCK), 1.0, 0.0).astype(BF16)
        in_sel = _dot(expand, sel_t)
        bias = jnp.where((in_sel > 0.5) & (krel + off <= kq), 0.0, NEG_BIG)
        s = _dot(ks_ref[0, 0, pl.ds(off, kt_w), :], qt) + tile_heads(bias)
        m_new = jnp.maximum(m_i, jnp.max(s, axis=0, keepdims=True))
        alpha = jnp.exp2(m_i - m_new)
        p = jnp.exp2(s - m_new)
        l_new = alpha * l_i + jnp.sum(p, axis=0, keepdims=True)
        acc = alpha * acc + _dot(vs_ref[0, 0, :, pl.ds(off, kt_w)], p.astype(BF16))
        return m_new, l_new, acc

    init = (jnp.full((1, cols), NEG_BIG, F32), jnp.zeros((1, cols), F32), jnp.zeros((HEAD_DIM, cols), F32))
    _, l_s, acc_s = lax.fori_loop(0, (c * qb + qb + kt_w - 1) // kt_w, sel_body, init)
    o_slc = acc_s / l_s

    span = WINDOW + 2 * qb
    w0 = pl.multiple_of(jnp.clip(c * qb - WINDOW, 0, kw_ref.shape[2] - span), qb)
    kpos = w0 + lax.broadcasted_iota(jnp.int32, (span, qb), 0)
    wq = c * qb + lax.broadcasted_iota(jnp.int32, (span, qb), 1)
    wbias = jnp.where((kpos <= wq) & (kpos > wq - WINDOW), 0.0, NEG_BIG)
    s_w = _dot(kw_ref[0, 0, pl.ds(w0, span), :], qt) + tile_heads(wbias)
    e_w = jnp.exp2(s_w - jnp.max(s_w, axis=0, keepdims=True))
    o_win = _dot(vw_ref[0, 0, :, pl.ds(w0, span)], e_w.astype(BF16)) / jnp.sum(e_w, axis=0, keepdims=True)

    gates_t = jax.nn.sigmoid(gate_ref[0]).T
    grow = lax.broadcasted_iota(jnp.int32, gates_t.shape, 0)
    outs = []
    for h in range(hg):
        hc = slice(h * qb, (h + 1) * qb)
        first = (g * hg + h) * 3
        pick = lambda j: jnp.sum(jnp.where(grow == first + j, gates_t, 0.0), axis=0, keepdims=True)
        outs.append(pick(0) * o_cmp[:, hc] + pick(1) * o_slc[:, hc] + pick(2) * o_win[:, hc])
    o_ref[0] = jnp.concatenate(outs, axis=0).T


def _nsa_attn(q, proj3, kc, vc, ks, vs, kw, vw):
    b, _, _, s = q.shape
    gw = NSA_GROUP * HEAD_DIM
    n_pad = kc.shape[2]
    per_group = lambda rows, width: pl.BlockSpec((1, 1, rows, width), lambda i, g, c: (i, g, 0, 0))
    return pl.pallas_call(
        _nsa_attn_kernel,
        grid=(b, NSA_KV_HEADS, s // Q_BLOCK),
        in_specs=[
            pl.BlockSpec((1, NSA_GROUP, HEAD_DIM, Q_BLOCK), lambda i, g, c: (i, g, 0, c)),
            pl.BlockSpec((1, Q_BLOCK, LANE), lambda i, g, c: (i, c, COL_GATE // LANE)),
            per_group(n_pad, HEAD_DIM), per_group(HEAD_DIM, n_pad),
            per_group(s, HEAD_DIM), per_group(HEAD_DIM, s), per_group(s, HEAD_DIM), per_group(HEAD_DIM, s),
        ],
        out_specs=pl.BlockSpec((1, Q_BLOCK, gw), lambda i, g, c: (i, c, g)),
        out_shape=jax.ShapeDtypeStruct((b, s, NSA_WIDTH), F32),
        compiler_params=_params(("parallel", "parallel", "arbitrary")),
        name="nsa_attn",
    )(q, proj3, kc, vc, ks, vs, kw, vw)


def _mem_kv_kernel(mem_ref, g_ref, w_ref, kn_ref, k_ref, v_ref):
    x = mem_ref[0]
    ms = jnp.mean(x * x, axis=-1, keepdims=True)
    xn = (x * lax.rsqrt(ms + RMS_EPS) * g_ref[...]).astype(BF16)
    kv = _dot(xn, w_ref[...])
    for h in range(MEM_HEADS):
        sl = slice(h * MEM_HEAD_DIM, (h + 1) * MEM_HEAD_DIM)
        kh = kv[:, sl]
        kms = jnp.mean(kh * kh, axis=-1, keepdims=True)
        k_ref[0, :, sl] = (kh * lax.rsqrt(kms + RMS_EPS) * kn_ref[...]).astype(BF16)
    v_ref[0] = kv[:, MEM_WIDTH:].astype(BF16)


def _mem_kv(mem, g_mem, w_kv, k_norm):
    b, m, d = mem.shape
    spec = pl.BlockSpec((1, m, MEM_WIDTH), lambda i: (i, 0, 0))
    shape = jax.ShapeDtypeStruct((b, m, MEM_WIDTH), BF16)
    return pl.pallas_call(
        _mem_kv_kernel,
        grid=(b,),
        in_specs=[
            pl.BlockSpec((1, m, d), lambda i: (i, 0, 0)),
            pl.BlockSpec((1, d), lambda i: (0, 0)),
            pl.BlockSpec((d, 2 * MEM_WIDTH), lambda i: (0, 0)),
            pl.BlockSpec((1, MEM_HEAD_DIM), lambda i: (0, 0)),
        ],
        out_specs=[spec, spec],
        out_shape=[shape, shape],
        compiler_params=_params(("parallel",)),
        name="mem_kv",
    )(mem, g_mem, w_kv, k_norm)


def _mem_attn_kernel(q_ref, qn_ref, k_ref, v_ref, o_ref):
    q = q_ref[0]
    for h in range(MEM_HEADS):
        sl = slice(h * MEM_HEAD_DIM, (h + 1) * MEM_HEAD_DIM)
        qh = q[:, sl]
        ms = jnp.mean(qh * qh, axis=-1, keepdims=True)
        qh = (qh * lax.rsqrt(ms + RMS_EPS) * qn_ref[...]).astype(BF16)
        s = _dot_nt(qh, k_ref[0, :, sl]) * (MEM_HEAD_DIM ** -0.5)
        m = jnp.max(s, axis=-1, keepdims=True)
        e = jnp.exp(s - m)
        p = e / jnp.sum(e, axis=-1, keepdims=True)
        o_ref[0, :, sl] = _dot(p.astype(BF16), v_ref[0, :, sl])


def _mem_attn(proj3, q_norm, k, v, tq=512):
    b, s, _ = proj3.shape
    m = k.shape[1]
    kv_spec = pl.BlockSpec((1, m, MEM_WIDTH), lambda i, t: (i, 0, 0))
    return pl.pallas_call(
        _mem_attn_kernel,
        grid=(b, s // tq),
        in_specs=[
            pl.BlockSpec((1, tq, MEM_WIDTH), lambda i, t: (i, t, COL_QMEM // MEM_WIDTH)),
            pl.BlockSpec((1, MEM_HEAD_DIM), lambda i, t: (0, 0)),
            kv_spec, kv_spec,
        ],
        out_specs=pl.BlockSpec((1, tq, MEM_WIDTH), lambda i, t: (i, t, 0)),
        out_shape=jax.ShapeDtypeStruct((b, s, MEM_WIDTH), F32),
        compiler_params=_params(("parallel", "parallel")),
        name="mem_attn",
    )(proj3, q_norm, k, v)


def _merge_kernel(oa_ref, ob_ref, om_ref, l0_ref, l1_ref, l2_ref, x_ref, bm_ref, wb_ref, wo_ref,
                  gf_ref, rw_ref, rb_ref, x1_ref, h_ref, route_ref):
    mixed = None
    for n, (o_ref, l_ref) in enumerate(((oa_ref, l0_ref), (ob_ref, l1_ref), (om_ref, l2_ref))):
        gate = jax.nn.sigmoid(l_ref[...] + bm_ref[:, n * D_MODEL:(n + 1) * D_MODEL])
        term = gate * _dot(o_ref[...].astype(BF16), wb_ref[n])
        mixed = term if mixed is None else mixed + term
    x1 = x_ref[...] + _dot(mixed.astype(BF16), wo_ref[...])
    x1_ref[...] = x1
    ms = jnp.mean(x1 * x1, axis=-1, keepdims=True)
    h = x1 * lax.rsqrt(ms + RMS_EPS) * gf_ref[...]
    _store_slabs(h_ref, h)

    logits = _dot(h, rw_ref[...], HI) + rb_ref[...]
    lane_f = lax.broadcasted_iota(jnp.int32, logits.shape, 1).astype(F32)
    work = logits
    picks = []
    for _ in range(TOP_K):
        mx = jnp.max(work, axis=-1, keepdims=True)
        idx = jnp.min(jnp.where(work == mx, lane_f, 2.0 * LANE), axis=-1, keepdims=True)
        picks.append((idx, mx))
        work = jnp.where(lane_f == idx, -jnp.inf, work)
    exps = [jnp.exp(mx - picks[0][1]) for _, mx in picks]
    denom = functools.reduce(lambda a, b: a + b, exps)
    route = jnp.zeros(logits.shape, F32)
    for kk, ((idx, _), ex) in enumerate(zip(picks, exps)):
        route = jnp.where(lane_f == kk, idx, route)
        route = jnp.where(lane_f == TOP_K + kk, ex / denom, route)
    route_ref[...] = route


def _merge(o_a, o_b, o_m, proj2, x2, b_merge, w_branch, w_out, g_ffn, router_w, router_b, tm=512):
    t, d = x2.shape
    row = lambda width: pl.BlockSpec((tm, width), lambda i: (i, 0))
    logit = lambda n: pl.BlockSpec((tm, d), lambda i: (i, COL_MERGE // d + n))
    full = lambda shape: pl.BlockSpec(shape, lambda i: (0,) * len(shape))
    return pl.pallas_call(
        _merge_kernel,
        grid=(t // tm,),
        in_specs=[
            row(BRANCH_WIDTH), row(BRANCH_WIDTH), row(BRANCH_WIDTH), logit(0), logit(1), logit(2), row(d),
            full((1, N_BRANCHES * d)), full((N_BRANCHES, BRANCH_WIDTH, d)), full((d, d)), full((1, d)),
            full((d, LANE)), full((1, LANE)),
        ],
        out_specs=[row(d), pl.BlockSpec((tm * SUBLANES, LANE), lambda i: (i, 0)), row(LANE)],
        out_shape=[jax.ShapeDtypeStruct((t, d), F32), jax.ShapeDtypeStruct((t * SUBLANES, LANE), F32),
                   jax.ShapeDtypeStruct((t, LANE), F32)],
        compiler_params=_params(("parallel",)),
        name="merge_router",
    )(o_a, o_b, o_m, proj2, proj2, proj2, x2, b_merge, w_branch, w_out, g_ffn, router_w, router_b)


def _deinterleave(x):
    rows, width = x.shape
    lane = lax.broadcasted_iota(jnp.int32, (rows, LANE), 1)
    half = LANE // 2
    low = lane < half
    idx = jnp.where(low, 2 * lane, 2 * (lane - half) + 1)
    evens, odds = [], []
    for j in range(0, width, 2 * LANE):
        a = jnp.take_along_axis(x[:, j:j + LANE], idx, axis=1)
        b = jnp.take_along_axis(x[:, j + LANE:j + 2 * LANE], idx, axis=1)
        evens.append(jnp.where(low, a, pltpu.roll(b, half, axis=1)))
        odds.append(jnp.where(low, pltpu.roll(a, half, axis=1), b))
    return jnp.concatenate(evens, axis=1), jnp.concatenate(odds, axis=1)


MOE_TILE = 512


def _route_kernel(route_ref, dest_ref, counts_ref, carry_ref, start_ref):
    phase = pl.program_id(0)
    i = pl.program_id(1)
    tr = route_ref.shape[0]
    route = route_ref[...]
    lane = lax.broadcasted_iota(jnp.int32, route.shape, 1)
    lane_f = lane.astype(F32)
    hits = [lane_f == route[:, kk:kk + 1] for kk in range(TOP_K)]
    sel = jnp.zeros(route.shape, F32)
    for hit in hits:
        sel = jnp.where(hit, 1.0, sel)

    @pl.when((phase == 0) & (i == 0))
    def _():
        carry_ref[...] = jnp.zeros_like(carry_ref)

    @pl.when((phase == 1) & (i == 0))
    def _():
        counts = carry_ref[...]
        counts_ref[...] = counts
        padded = jnp.ceil(counts * (1.0 / MOE_TILE)) * MOE_TILE
        ui = lax.broadcasted_iota(jnp.int32, (LANE, LANE), 0)
        uj = lax.broadcasted_iota(jnp.int32, (LANE, LANE), 1)
        start_ref[...] = _dot01_right(jnp.broadcast_to(padded, (8, LANE)), (ui < uj).astype(BF16))[0:1]
        carry_ref[...] = jnp.zeros_like(carry_ref)

    @pl.when(phase == 1)
    def _():
        ri = lax.broadcasted_iota(jnp.int32, (tr, tr), 0)
        rj = lax.broadcasted_iota(jnp.int32, (tr, tr), 1)
        rank = carry_ref[...] + _dot((rj < ri).astype(BF16), sel.astype(BF16))
        row = start_ref[...] + rank
        dest = jnp.zeros(route.shape, F32)
        for kk, hit in enumerate(hits):
            dest = jnp.where(lane == kk, jnp.sum(jnp.where(hit, row, 0.0), axis=-1, keepdims=True), dest)
        dest_ref[...] = dest.astype(jnp.int32)

    carry_ref[...] += jnp.sum(sel, axis=0, keepdims=True)


def _route(route, tr=512):
    t = route.shape[0]
    return pl.pallas_call(
        _route_kernel,
        grid=(2, t // tr),
        in_specs=[pl.BlockSpec((tr, LANE), lambda p, i: (i, 0))],
        out_specs=[pl.BlockSpec((tr, LANE), lambda p, i: (i * p, 0)), pl.BlockSpec((1, LANE), lambda p, i: (0, 0))],
        out_shape=[jax.ShapeDtypeStruct((t, LANE), jnp.int32), jax.ShapeDtypeStruct((1, LANE), F32)],
        scratch_shapes=[pltpu.VMEM((1, LANE), F32), pltpu.VMEM((1, LANE), F32)],
        compiler_params=_params(("arbitrary", "arbitrary")),
        name="moe_route",
    )(route)


def _row_copy(src, src_row, dst, dst_row, sem):
    src_at = pl.ds(pl.multiple_of(src_row * SUBLANES, SUBLANES), SUBLANES)
    dst_at = pl.ds(pl.multiple_of(dst_row * SUBLANES, SUBLANES), SUBLANES)
    return pltpu.make_async_copy(src.at[src_at], dst.at[dst_at], sem)


def _dispatch_kernel(dest_ref, h_ref, xs_in_ref, xs_ref, sem):
    del xs_in_ref
    td = h_ref.shape[0] // SUBLANES

    def issue(r, carry):
        for kk in range(TOP_K):
            _row_copy(h_ref, r, xs_ref, dest_ref[r * TOP_K + kk], sem).start(priority=kk % 2)
        return carry

    lax.fori_loop(0, td, issue, 0, unroll=8)

    def drain(r, carry):
        for kk in range(TOP_K):
            _row_copy(h_ref, 0, xs_ref, 0, sem).wait()
        return carry

    lax.fori_loop(0, td, drain, 0, unroll=8)


def _dispatch(dest_flat, h, xs_zero, td=512):
    t = h.shape[0] // SUBLANES
    return pl.pallas_call(
        _dispatch_kernel,
        grid=(t // td,),
        in_specs=[
            pl.BlockSpec((td * TOP_K,), lambda i: (i,), memory_space=pltpu.SMEM),
            pl.BlockSpec((td * SUBLANES, LANE), lambda i: (i, 0)),
            pl.BlockSpec(memory_space=pl.ANY),
        ],
        out_specs=pl.BlockSpec(memory_space=pl.ANY),
        out_shape=jax.ShapeDtypeStruct(xs_zero.shape, xs_zero.dtype),
        scratch_shapes=[pltpu.SemaphoreType.DMA],
        input_output_aliases={2: 0},
        compiler_params=_params(("arbitrary",)),
        name="moe_dispatch",
    )(dest_flat, h, xs_zero)


def _experts_kernel(te_ref, nu_ref, xs_ref, w1_ref, b1g_ref, b1l_ref, w2_ref, b2_ref, y_ref, w1b_ref, w2b_ref):
    j = pl.program_id(0)

    @pl.when(j < nu_ref[0])
    def _():
        @pl.when((j == 0) | (te_ref[j] != te_ref[jnp.maximum(j - 1, 0)]))
        def _():
            w1b_ref[...] = w1_ref[0].astype(BF16)
            w2b_ref[...] = w2_ref[0].astype(BF16)

        x = _load_slabs(xs_ref, MOE_TILE).astype(BF16)
        hid_g, hid_l = _deinterleave(_dot(x, w1b_ref[...]))
        x_glu = jnp.minimum(hid_g + b1g_ref[0], SWIGLU_LIMIT)
        x_lin = jnp.clip(hid_l + b1l_ref[0], -SWIGLU_LIMIT, SWIGLU_LIMIT)
        act = x_glu * jax.nn.sigmoid(SWIGLU_ALPHA * x_glu) * (x_lin + 1.0)
        _store_slabs(y_ref, _dot(act.astype(BF16), w2b_ref[...]) + b2_ref[0])

    @pl.when(j >= nu_ref[0])
    def _():
        y_ref[...] = jnp.zeros_like(y_ref)


def _experts(tile_expert, n_used, xs, w1, b1g, b1l, w2, b2):
    rows = xs.shape[0] // SUBLANES
    n_e, ff, d = w2.shape
    tg = MOE_TILE
    used = lambda j, te, nu: jnp.minimum(j, nu[0] - 1)
    exp_of = lambda j, te, nu: te[used(j, te, nu)]
    return pl.pallas_call(
        _experts_kernel,
        grid_spec=pltpu.PrefetchScalarGridSpec(
            num_scalar_prefetch=2,
            grid=(rows // tg,),
            in_specs=[
                pl.BlockSpec((tg * SUBLANES, LANE), lambda j, te, nu: (used(j, te, nu), 0)),
                pl.BlockSpec((1, d, 2 * ff), lambda j, te, nu: (exp_of(j, te, nu), 0, 0)),
                pl.BlockSpec((1, 1, ff), lambda j, te, nu: (exp_of(j, te, nu), 0, 0)),
                pl.BlockSpec((1, 1, ff), lambda j, te, nu: (exp_of(j, te, nu), 0, 0)),
                pl.BlockSpec((1, ff, d), lambda j, te, nu: (exp_of(j, te, nu), 0, 0)),
                pl.BlockSpec((1, 1, d), lambda j, te, nu: (exp_of(j, te, nu), 0, 0)),
            ],
            out_specs=pl.BlockSpec((tg * SUBLANES, LANE), lambda j, te, nu: (j, 0)),
            scratch_shapes=[pltpu.VMEM((d, 2 * ff), BF16), pltpu.VMEM((ff, d), BF16)],
        ),
        out_shape=jax.ShapeDtypeStruct((rows * SUBLANES, LANE), F32),
        compiler_params=_params(("arbitrary",)),
        name="moe_experts",
    )(tile_expert, n_used, xs, w1, b1g, b1l, w2, b2)


def _combine_kernel(dest_ref, dest_next_ref, route_ref, x1_ref, y_ref, o_ref, buf_ref, sems):
    i = pl.program_id(0)
    tc = x1_ref.shape[0]
    slot = i % 2

    def gather(dref, to_slot):
        def issue(r, carry):
            for kk in range(TOP_K):
                _row_copy(y_ref, dref[r * TOP_K + kk], buf_ref.at[to_slot, kk], r, sems.at[to_slot]).start(
                    priority=kk % 2)
            return carry

        lax.fori_loop(0, tc, issue, 0, unroll=8)

    @pl.when(i == 0)
    def _():
        gather(dest_ref, slot)

    @pl.when(i + 1 < pl.num_programs(0))
    def _():
        gather(dest_next_ref, 1 - slot)

    def drain(r, carry):
        for kk in range(TOP_K):
            _row_copy(y_ref, 0, buf_ref.at[slot, kk], 0, sems.at[slot]).wait()
        return carry

    lax.fori_loop(0, tc, drain, 0, unroll=8)

    route = route_ref[...]
    acc = x1_ref[...]
    for kk in range(TOP_K):
        acc = acc + route[:, TOP_K + kk:TOP_K + kk + 1] * _load_slabs(buf_ref, tc, lead=(slot, kk))
    o_ref[...] = acc


def _combine(dest_flat, route, x1, y, tc=256):
    t, d = x1.shape
    last = t // tc - 1
    return pl.pallas_call(
        _combine_kernel,
        grid=(t // tc,),
        in_specs=[
            pl.BlockSpec((tc * TOP_K,), lambda i: (i,), memory_space=pltpu.SMEM),
            pl.BlockSpec((tc * TOP_K,), lambda i: (jnp.minimum(i + 1, last),), memory_space=pltpu.SMEM),
            pl.BlockSpec((tc, LANE), lambda i: (i, 0)),
            pl.BlockSpec((tc, d), lambda i: (i, 0)),
            pl.BlockSpec(memory_space=pl.ANY),
        ],
        out_specs=pl.BlockSpec((tc, d), lambda i: (i, 0)),
        out_shape=jax.ShapeDtypeStruct((t, d), F32),
        scratch_shapes=[pltpu.VMEM((2, TOP_K, tc * SUBLANES, LANE), F32), pltpu.SemaphoreType.DMA((2,))],
        compiler_params=_params(("arbitrary",)),
        name="moe_combine",
    )(dest_flat, dest_flat, route, x1, y)


def _moe(h, route, x1, w1, b1g, b1l, w2, b2):
    t, d = x1.shape
    assert d == SUBLANES * LANE
    n_e = w2.shape[0]
    n_tiles = (t * TOP_K) // MOE_TILE + n_e
    dest, counts = _route(route)
    tiles_per = jnp.ceil(counts[0, :n_e] * (1.0 / MOE_TILE)).astype(jnp.int32)
    tile_end = jnp.cumsum(tiles_per)
    past = (tile_end[None, :] <= jnp.arange(n_tiles, dtype=jnp.int32)[:, None]).astype(jnp.int32)
    tile_expert = jnp.minimum(jnp.sum(past, axis=1), n_e - 1).astype(jnp.int32)
    n_used = tile_end[-1:].astype(jnp.int32)
    dest_flat = dest[:, :TOP_K].reshape(-1)
    xs = _dispatch(dest_flat, h, jnp.zeros((n_tiles * MOE_TILE * SUBLANES, LANE), F32))
    y = _experts(tile_expert, n_used, xs, w1, b1g, b1l, w2, b2)
    return _combine(dest_flat, route, x1, y)


def _pad_cols(w, width):
    return jnp.pad(w, ((0, 0), (0, width - w.shape[1])))


def _rope_tables(pos, reps):
    inv = ROPE_THETA ** (-jnp.arange(0, HEAD_DIM, 2, dtype=F32) / HEAD_DIM)
    ang = pos.astype(F32)[:, None] * inv[None, :]
    ang = jnp.concatenate([ang, ang], axis=-1)
    sign = jnp.concatenate([-jnp.ones((HEAD_DIM // 2,), F32), jnp.ones((HEAD_DIM // 2,), F32)])
    return jnp.tile(jnp.cos(ang), (1, reps)), jnp.tile(jnp.sin(ang) * sign, (1, reps))


def _layer(x, mem, g_mix, g_mem, w_in, b_merge, nsa_q_norm, nsa_k_norm, cmp_pe, cmp_w1, cmp_b1, cmp_w2,
           rwkv_shift_mix, rwkv_w0, rwkv_w_up, rwkv_a0, rwkv_a_up, rwkv_g_up, rwkv_k_k, rwkv_k_a, rwkv_r_k,
           rwkv_ln_w, rwkv_ln_b, mem_w_kv, mem_q_norm, mem_k_norm, w_branch, w_out, g_ffn,
           router_w, router_b, exp_w1, exp_b1, exp_w2, exp_b2, cos, sin, cos_c, sin_c):
    b, s, d = x.shape
    t = b * s
    x2 = x.reshape(t, d)

    o = 0
    parts = []
    for width in (NSA_WIDTH, 6 * NSA_KV_WIDTH, 3 * NSA_HEADS,
                  3 * RWKV_WIDTH + DECAY_RANK + AAA_RANK + GATE_RANK, MEM_WIDTH, N_BRANCHES * D_MODEL):
        parts.append(w_in[:, o:o + width])
        o += width
    w_q, w_kv, w_gate, w_rwkv, w_qm, w_merge = parts

    def rwkv_layout(m):
        r3 = m[:, :3 * RWKV_WIDTH]
        xw = m[:, 3 * RWKV_WIDTH:3 * RWKV_WIDTH + DECAY_RANK]
        xa = m[:, 3 * RWKV_WIDTH + DECAY_RANK:3 * RWKV_WIDTH + DECAY_RANK + AAA_RANK]
        xg = m[:, 3 * RWKV_WIDTH + DECAY_RANK + AAA_RANK:]
        return _pad_cols(jnp.concatenate([r3, _pad_cols(xw, LANE), _pad_cols(xa, LANE), xg], axis=1), RWKV_PAD)

    w_all = jnp.concatenate([rwkv_layout(w_rwkv), w_merge, w_q, w_qm, w_kv, w_gate], axis=1)
    w_all = _pad_cols(w_all, IN_PAD).astype(BF16)
    proj2 = _inproj(x2, g_mix.reshape(1, d), w_all)
    proj3 = proj2.reshape(b, s, IN_PAD)

    rowv = lambda a: a.reshape(1, -1)
    pad_rows = lambda m: jnp.pad(m, ((0, LANE - m.shape[0]), (0, 0)))
    o_b = _rwkv(proj3, rwkv_layout(rowv(rwkv_shift_mix)), rowv(rwkv_w0), pad_rows(rwkv_w_up), rowv(rwkv_a0),
                pad_rows(rwkv_a_up), rwkv_g_up, rowv(rwkv_k_k), rowv(rwkv_k_a), rowv(rwkv_r_k),
                rowv(rwkv_ln_w), rowv(rwkv_ln_b))

    q_gain = jnp.tile(nsa_q_norm.reshape(1, HEAD_DIM), (1, NSA_HEADS))
    k_gain = jnp.tile(nsa_k_norm, (1, NSA_KV_HEADS))
    qn, ks, vs, kw, vw = _nsa_prep(proj3, cos, sin, q_gain, k_gain)
    kc, vc = _nsa_cmp(proj3, cos_c, sin_c, k_gain, cmp_pe, cmp_w1.astype(BF16),
                      cmp_b1.reshape(2, 1, CMP_HIDDEN), cmp_w2.astype(BF16))
    o_a = _nsa_attn(qn, proj3, kc, vc, ks, vs, kw, vw)

    mk, mv = _mem_kv(mem, g_mem.reshape(1, d), mem_w_kv.astype(BF16), mem_k_norm.reshape(1, MEM_HEAD_DIM))
    o_m = _mem_attn(proj3, mem_q_norm.reshape(1, MEM_HEAD_DIM), mk, mv)

    rw = _pad_cols(router_w, LANE)
    rb = jnp.concatenate([router_b, jnp.full((LANE - N_EXPERTS,), NEG_BIG, F32)]).reshape(1, LANE)
    x1, h2, route = _merge(o_a.reshape(t, NSA_WIDTH), o_b.reshape(t, RWKV_WIDTH), o_m.reshape(t, MEM_WIDTH),
                          proj2, x2, b_merge.reshape(1, -1), w_branch.astype(BF16), w_out.astype(BF16),
                          g_ffn.reshape(1, d), rw, rb)

    b1g = exp_b1[:, None, 0::2]
    b1l = exp_b1[:, None, 1::2]
    out = _moe(h2, route, x1, exp_w1, b1g, b1l, exp_w2, exp_b2[:, None, :])
    return out.reshape(b, s, d)


def kernel(x, mem, g_mix, g_mem, w_in, b_merge, nsa_q_norm, nsa_k_norm, cmp_pe, cmp_w1, cmp_b1, cmp_w2,
           rwkv_shift_mix, rwkv_w0, rwkv_w_up, rwkv_a0, rwkv_a_up, rwkv_g_up, rwkv_k_k, rwkv_k_a, rwkv_r_k,
           rwkv_ln_w, rwkv_ln_b, mem_w_kv, mem_q_norm, mem_k_norm, w_branch, w_out, g_ffn,
           router_w, router_b, exp_w1, exp_b1, exp_w2, exp_b2):
    s = x.shape[1]
    cos, sin = _rope_tables(jnp.arange(s), NSA_HEADS)
    n_cmp = (s - CMP_LEN) // CMP_STRIDE + 1
    cos_c, sin_c = _rope_tables(jnp.arange(n_cmp + 1) * CMP_STRIDE + CMP_LEN - 1, NSA_KV_HEADS)
    depth = g_mix.shape[0]
    for l in range(depth):
        x = _layer(x, mem, g_mix[l], g_mem[l], w_in[l], b_merge[l], nsa_q_norm[l], nsa_k_norm[l], cmp_pe[l],
                   cmp_w1[l], cmp_b1[l], cmp_w2[l], rwkv_shift_mix[l], rwkv_w0[l], rwkv_w_up[l], rwkv_a0[l],
                   rwkv_a_up[l], rwkv_g_up[l], rwkv_k_k[l], rwkv_k_a[l], rwkv_r_k[l], rwkv_ln_w[l], rwkv_ln_b[l],
                   mem_w_kv[l], mem_q_norm[l], mem_k_norm[l], w_branch[l], w_out[l], g_ffn[l], router_w[l],
                   router_b[l], exp_w1[l], exp_b1[l], exp_w2[l], exp_b2[l], cos, sin, cos_c, sin_c)
    return x
```

```python
import functools

import jax
import jax.numpy as jnp
from jax import lax
from jax.experimental import pallas as pl
from jax.experimental.pallas import tpu as pltpu

F32 = jnp.float32
BF16 = jnp.bfloat16
HI = lax.Precision.HIGHEST

D_MODEL = 1024
HEAD_DIM = 64
NSA_HEADS = 8
NSA_KV_HEADS = 2
NSA_GROUP = NSA_HEADS // NSA_KV_HEADS
NSA_WIDTH = NSA_HEADS * HEAD_DIM
NSA_KV_WIDTH = NSA_KV_HEADS * HEAD_DIM
CMP_LEN = 32
CMP_STRIDE = 16
CMP_HIDDEN = 128
SEL_BLOCK = 64
TOP_N = 8
WINDOW = 512
Q_BLOCK = 128
FORCE_BONUS = 1000.0
RWKV_HEADS = 8
RWKV_HEAD_DIM = 64
RWKV_WIDTH = RWKV_HEADS * RWKV_HEAD_DIM
DECAY_RANK = 64
AAA_RANK = 64
GATE_RANK = 128
GN_EPS = 64e-5
MEM_HEADS = 4
MEM_HEAD_DIM = 128
MEM_WIDTH = MEM_HEADS * MEM_HEAD_DIM
N_BRANCHES = 3
BRANCH_WIDTH = 512
N_EXPERTS = 32
TOP_K = 4
EXPERT_FF = 1024
SWIGLU_ALPHA = 1.702
SWIGLU_LIMIT = 7.0
ROPE_THETA = 10000.0
RMS_EPS = 1e-6

LANE = 128
LOG2_E = 1.4426950408889634
NEG_BIG = -1e30

RWKV_PAD = 2048
COL_RWKV = 0
COL_CMP = COL_RWKV + RWKV_PAD
PROJ_F32_WIDTH = COL_CMP + 2 * NSA_KV_WIDTH
COL_MERGE = 0
COL_QNSA = COL_MERGE + N_BRANCHES * D_MODEL
COL_QMEM = COL_QNSA + NSA_WIDTH
COL_KV = COL_QMEM + MEM_WIDTH
COL_GATE = COL_KV + 4 * NSA_KV_WIDTH
PROJ_BF16_WIDTH = 5120
RW_R, RW_K, RW_V, RW_XW, RW_XA, RW_XG = 0, 512, 1024, 1536, 1664, 1792

RWKV_CHUNK = 64
VMEM_LIMIT = 56 * 1024 * 1024


def _dot(a, b, prec=None):
    return jnp.dot(a, b, preferred_element_type=F32, precision=prec)


def _dot_nt(a, b, prec=None):
    return lax.dot_general(a, b, (((1,), (1,)), ((), ())), preferred_element_type=F32, precision=prec)


def _dot_tn(a, b, prec=None):
    return lax.dot_general(a, b, (((0,), (0,)), ((), ())), preferred_element_type=F32, precision=prec)


def _split3(x):
    hi = x.astype(BF16)
    r1 = x - hi.astype(F32)
    mid = r1.astype(BF16)
    lo = (r1 - mid.astype(F32)).astype(BF16)
    return hi, mid, lo


def _dot01_left(m01, x):
    n = x.shape[1]
    out = _dot(m01, jnp.concatenate(_split3(x), axis=1))
    return out[:, :n] + out[:, n:2 * n] + out[:, 2 * n:]


def _dot01_right(x, m01):
    m = x.shape[0]
    out = _dot(jnp.concatenate(_split3(x), axis=0), m01)
    return out[:m] + out[m:2 * m] + out[2 * m:]


def _seg_matrix(width, seg):
    r = lax.broadcasted_iota(jnp.int32, (width, width), 0) // seg
    c = lax.broadcasted_iota(jnp.int32, (width, width), 1) // seg
    return (r == c).astype(F32)


SUBLANES = 8


def _store_slabs(ref, x, lead=()):
    rows = x.shape[0]
    for s in range(SUBLANES):
        ref[lead + (pl.ds(s, rows, stride=SUBLANES), slice(None))] = x[:, s * LANE:(s + 1) * LANE]


def _load_slabs(ref, rows, lead=()):
    return jnp.concatenate(
        [ref[lead + (pl.ds(s, rows, stride=SUBLANES), slice(None))] for s in range(SUBLANES)], axis=1)


def _params(sem):
    return pltpu.CompilerParams(dimension_semantics=sem, vmem_limit_bytes=VMEM_LIMIT)


def _inproj_kernel(x_ref, g_ref, w_ref, o_ref, hn_ref):
    @pl.when(pl.program_id(1) == 0)
    def _():
        x = x_ref[...]
        ms = jnp.mean(x * x, axis=-1, keepdims=True)
        hn_ref[...] = (x * lax.rsqrt(ms + RMS_EPS) * g_ref[...]).astype(BF16)

    o_ref[...] = _dot(hn_ref[...], w_ref[...]).astype(o_ref.dtype)


def _inproj(x2, g, w, out_dtype, tn, tm=1024):
    t, d = x2.shape
    n = w.shape[1]
    return pl.pallas_call(
        _inproj_kernel,
        grid=(t // tm, n // tn),
        in_specs=[
            pl.BlockSpec((tm, d), lambda i, j: (i, 0)),
            pl.BlockSpec((1, d), lambda i, j: (0, 0)),
            pl.BlockSpec((d, tn), lambda i, j: (0, j)),
        ],
        out_specs=pl.BlockSpec((tm, tn), lambda i, j: (i, j)),
        out_shape=jax.ShapeDtypeStruct((t, n), out_dtype),
        scratch_shapes=[pltpu.VMEM((tm, d), BF16)],
        compiler_params=_params(("parallel", "arbitrary")),
        name="inproj",
    )(x2, g, w)


def _rwkv_chunk_kernel(p_ref, pprev_ref, mix_ref, w0_ref, wup_ref, a0_ref, aup_ref, gup_ref, kk_ref, ka_ref,
                       rk_ref, rm_ref, y0_ref, bonus_ref, g_ref, gam_ref, m_ref, d0_ref):
    c = RWKV_CHUNK
    n = RWKV_HEAD_DIM
    rows = p_ref.shape[1]

    p = p_ref[0]
    row = lax.broadcasted_iota(jnp.int32, p.shape, 0)
    last_prev = jnp.where(pl.program_id(1) == 0, 0.0, 1.0) * pprev_ref[0, 7:8, :]
    prev = jnp.where(row == 0, last_prev, pltpu.roll(p, 1, axis=0))
    ps = p + (prev - p) * mix_ref[...]
    r = ps[:, RW_R:RW_R + RWKV_WIDTH]
    k = ps[:, RW_K:RW_K + RWKV_WIDTH]
    v = ps[:, RW_V:RW_V + RWKV_WIDTH]
    xw = ps[:, RW_XW:RW_XW + LANE]
    xa = ps[:, RW_XA:RW_XA + LANE]
    xg = ps[:, RW_XG:RW_XG + LANE]

    z = -(w0_ref[...] + _dot(jnp.tanh(xw), wup_ref[...], HI))
    softplus = jnp.maximum(z, 0.0) + jnp.log1p(jnp.exp(-jnp.abs(z)))
    w = -softplus - 0.5
    logw = -jnp.exp(w)
    a = jax.nn.sigmoid(a0_ref[...] + _dot(xa, aup_ref[...], HI))
    g_ref[0] = _dot(jax.nn.sigmoid(xg), gup_ref[...], HI)

    seg = _seg_matrix(RWKV_WIDTH, n).astype(BF16)
    kk = k * kk_ref[...]
    k = k * (1.0 + (a - 1.0) * ka_ref[...])
    sums = _dot01_right(jnp.concatenate([kk * kk, r * k * rk_ref[...]], axis=0), seg)
    kk = kk / jnp.maximum(jnp.sqrt(sums[:rows]), 1e-12)
    bonus_ref[0] = sums[rows:] * v
    kka = kk * a

    ti = lax.broadcasted_iota(jnp.int32, (2 * c, 2 * c), 0)
    tj = lax.broadcasted_iota(jnp.int32, (2 * c, 2 * c), 1)
    keep = (tj % c) < jnp.where(ti < c, ti, ti - c + 1)
    ci = lax.broadcasted_iota(jnp.int32, (c, c), 0)
    cj = lax.broadcasted_iota(jnp.int32, (c, c), 1)
    eye = (ci == cj).astype(F32)
    ltri = (cj <= ci).astype(BF16)
    zeros_cn = jnp.zeros((c, n), BF16)

    chains = []
    for j in range(rows // c):
        rs = slice(j * c, (j + 1) * c)
        lw = logw[rs]
        cum = _dot01_left(ltri, lw)
        tot = cum[c - 1:c, :]
        einv = jnp.exp(-cum)
        dec_end = jnp.exp(tot - cum)
        r_f = r[rs] * jnp.exp(cum)
        a_t = (-kk[rs] * jnp.exp(cum - lw)).astype(BF16)
        b_t = (kka[rs] * einv).astype(BF16)
        k_t = (k[rs] * einv).astype(BF16)
        r_t = r_f.astype(BF16)
        b_e = (kka[rs] * dec_end).astype(BF16)
        k_e = (k[rs] * dec_end).astype(BF16)
        v_b = v[rs].astype(BF16)
        gam_ref[0, j] = jnp.exp(tot)
        for h in range(RWKV_HEADS):
            sl = slice(h * n, (h + 1) * n)
            chains.append(dict(j=j, h=h, rs=rs, sl=sl, a=a_t[:, sl], r=r_t[:, sl], rf=r_f[:, sl], v=v_b[:, sl],
                               rhs=jnp.concatenate([b_t[:, sl], k_t[:, sl]], axis=0),
                               bke=jnp.concatenate([b_e[:, sl], k_e[:, sl]], axis=0)))

    for ch in chains:
        lhs = jnp.concatenate([ch["a"], ch["r"]], axis=0)
        ch["amat"] = jnp.where(keep, _dot_nt(lhs, ch["rhs"]), 0.0)
        ch["pw"] = ch["amat"][:c, :c]
        ch["tinv"] = eye + ch["pw"]
    for _ in range(5):
        for ch in chains:
            pw_b = ch["pw"].astype(BF16)
            ch["pw"] = _dot(pw_b, pw_b)
        for ch in chains:
            ch["tinv"] = ch["tinv"] + _dot(ch["tinv"].astype(BF16), ch["pw"].astype(BF16))
    for ch in chains:
        ch["akv"] = _dot(ch["amat"][:c, c:].astype(BF16), ch["v"])
    for ch in chains:
        wu = _dot(ch["tinv"].astype(BF16), jnp.concatenate([ch["a"], ch["akv"].astype(BF16)], axis=1)).astype(BF16)
        ch["x"] = jnp.concatenate([wu, jnp.concatenate([zeros_cn, ch["v"]], axis=1)], axis=0)
    for ch in chains:
        ry = _dot(ch["amat"][c:, :].astype(BF16), ch["x"])
        rm_ref[0, ch["rs"], ch["sl"]] = (ch["rf"] + ry[:, :n]).astype(BF16)
        y0_ref[0, ch["rs"], ch["sl"]] = ry[:, n:]
    for ch in chains:
        md = _dot_tn(ch["x"], ch["bke"])
        m_ref[0, ch["j"], ch["h"]] = md[:n].astype(BF16)
        d0_ref[0, ch["j"], ch["h"]] = md[n:]


def _rwkv_chunks(proj3, mix, w0, wup, a0, aup, gup, k_k, k_a, r_k, rows=256):
    b, s, _ = proj3.shape
    c = RWKV_CHUNK
    nc = s // c
    cps = rows // c
    vec = lambda width: pl.BlockSpec((1, width), lambda i, t: (0, 0))
    mat = lambda nrows: pl.BlockSpec((nrows, RWKV_WIDTH), lambda i, t: (0, 0))
    tok = pl.BlockSpec((1, rows, RWKV_WIDTH), lambda i, t: (i, t, 0))
    sq = pl.BlockSpec((1, cps, RWKV_HEADS, RWKV_HEAD_DIM, RWKV_HEAD_DIM), lambda i, t: (i, t, 0, 0, 0))
    tok_shape = lambda dt: jax.ShapeDtypeStruct((b, s, RWKV_WIDTH), dt)
    sq_shape = lambda dt: jax.ShapeDtypeStruct((b, nc, RWKV_HEADS, RWKV_HEAD_DIM, RWKV_HEAD_DIM), dt)
    return pl.pallas_call(
        _rwkv_chunk_kernel,
        grid=(b, s // rows),
        in_specs=[
            pl.BlockSpec((1, rows, RWKV_PAD), lambda i, t: (i, t, COL_RWKV // RWKV_PAD)),
            pl.BlockSpec((1, 8, RWKV_PAD), lambda i, t: (i, jnp.maximum(t * (rows // 8) - 1, 0), COL_RWKV // RWKV_PAD)),
            vec(RWKV_PAD), vec(RWKV_WIDTH), mat(LANE), vec(RWKV_WIDTH), mat(LANE), mat(LANE),
            vec(RWKV_WIDTH), vec(RWKV_WIDTH), vec(RWKV_WIDTH),
        ],
        out_specs=[tok, tok, tok, tok,
                   pl.BlockSpec((1, cps, 1, RWKV_WIDTH), lambda i, t: (i, t, 0, 0)), sq, sq],
        out_shape=[tok_shape(BF16), tok_shape(F32), tok_shape(F32), tok_shape(F32),
                   jax.ShapeDtypeStruct((b, nc, 1, RWKV_WIDTH), F32), sq_shape(BF16), sq_shape(F32)],
        compiler_params=_params(("parallel", "parallel")),
        name="rwkv7_chunks",
    )(proj3, proj3, mix, w0, wup, a0, aup, gup, k_k, k_a, r_k)


def _rwkv_scan_kernel(rm_ref, y0_ref, bonus_ref, g_ref, gam_ref, m_ref, d0_ref, lnw_ref, lnb_ref,
                      o_ref, state_ref, y_ref):
    c = RWKV_CHUNK
    n = RWKV_HEAD_DIM

    @pl.when(pl.program_id(1) == 0)
    def _():
        state_ref[...] = jnp.zeros_like(state_ref)

    for j in range(gam_ref.shape[1]):
        rs = slice(j * c, (j + 1) * c)
        gam = gam_ref[0, j]
        for h in range(RWKV_HEADS):
            sl = slice(h * n, (h + 1) * n)
            s = state_ref[h]
            s_b = s.astype(BF16)
            y_ref[rs, sl] = _dot_nt(rm_ref[0, rs, sl], s_b) + y0_ref[0, rs, sl]
            state_ref[h] = s * gam[:, sl] + _dot(s_b, m_ref[0, j, h]) + d0_ref[0, j, h]

    seg = _seg_matrix(RWKV_WIDTH, n).astype(BF16)
    y = y_ref[...]
    mu = _dot01_right(y, seg) * (1.0 / n)
    d = y - mu
    var = _dot01_right(d * d, seg) * (1.0 / n)
    yn = d * lax.rsqrt(var + GN_EPS) * lnw_ref[...] + lnb_ref[...]
    o_ref[0] = (yn + bonus_ref[0]) * g_ref[0]


def _rwkv_scan(rm, y0, bonus, g, gam, m, d0, ln_w, ln_b, rows=256):
    b, s, _ = rm.shape
    cps = rows // RWKV_CHUNK
    vec = pl.BlockSpec((1, RWKV_WIDTH), lambda i, t: (0, 0))
    tok = pl.BlockSpec((1, rows, RWKV_WIDTH), lambda i, t: (i, t, 0))
    sq = pl.BlockSpec((1, cps, RWKV_HEADS, RWKV_HEAD_DIM, RWKV_HEAD_DIM), lambda i, t: (i, t, 0, 0, 0))
    return pl.pallas_call(
        _rwkv_scan_kernel,
        grid=(b, s // rows),
        in_specs=[tok, tok, tok, tok, pl.BlockSpec((1, cps, 1, RWKV_WIDTH), lambda i, t: (i, t, 0, 0)), sq, sq,
                  vec, vec],
        out_specs=tok,
        out_shape=jax.ShapeDtypeStruct((b, s, RWKV_WIDTH), F32),
        scratch_shapes=[
            pltpu.VMEM((RWKV_HEADS, RWKV_HEAD_DIM, RWKV_HEAD_DIM), F32),
            pltpu.VMEM((rows, RWKV_WIDTH), F32),
        ],
        compiler_params=_params(("parallel", "arbitrary")),
        name="rwkv7_scan",
    )(rm, y0, bonus, g, gam, m, d0, ln_w, ln_b)


def _rwkv(proj3, mix, w0, wup, a0, aup, gup, k_k, k_a, r_k, ln_w, ln_b):
    rm, y0, bonus, g, gam, m, d0 = _rwkv_chunks(proj3, mix, w0, wup, a0, aup, gup, k_k, k_a, r_k)
    return _rwkv_scan(rm, y0, bonus, g, gam, m, d0, ln_w, ln_b)


def _rope(x, cos, sin_signed):
    w = x.shape[-1]
    first_half = (lax.broadcasted_iota(jnp.int32, x.shape, 1) % HEAD_DIM) < (HEAD_DIM // 2)
    rot = jnp.where(first_half, pltpu.roll(x, w - HEAD_DIM // 2, axis=1), pltpu.roll(x, HEAD_DIM // 2, axis=1))
    return x * cos + rot * sin_signed


def _head_rmsnorm(x, gain, seg):
    ms = _dot01_right(x * x, seg) * (1.0 / HEAD_DIM)
    return x * lax.rsqrt(ms + RMS_EPS) * gain


def _split_groups(x):
    return [x[:, g * HEAD_DIM:(g + 1) * HEAD_DIM] for g in range(NSA_KV_HEADS)]


def _nsa_prep_kernel(q_ref, ksl_ref, vsl_ref, kwn_ref, vwn_ref, cos_ref, sin_ref, qn_ref, kn_ref,
                     qo_ref, ks_ref, vs_ref, kw_ref, vw_ref):
    seg_q = _seg_matrix(NSA_WIDTH, HEAD_DIM).astype(BF16)
    seg_k = _seg_matrix(NSA_KV_WIDTH, HEAD_DIM).astype(BF16)
    cos_k = cos_ref[:, :NSA_KV_WIDTH]
    sin_k = sin_ref[:, :NSA_KV_WIDTH]

    q = _rope(_head_rmsnorm(q_ref[0].astype(F32), qn_ref[...], seg_q), cos_ref[...], sin_ref[...])
    q_t = (q * (HEAD_DIM ** -0.5 * LOG2_E)).T
    for h in range(NSA_HEADS):
        qo_ref[0, h] = q_t[h * HEAD_DIM:(h + 1) * HEAD_DIM].astype(BF16)

    ks = _rope(_head_rmsnorm(ksl_ref[0].astype(F32), kn_ref[1:2, :], seg_k), cos_k, sin_k)
    kw = _rope(_head_rmsnorm(kwn_ref[0].astype(F32), kn_ref[2:3, :], seg_k), cos_k, sin_k)
    vs_t = vsl_ref[0].astype(F32).T
    vw_t = vwn_ref[0].astype(F32).T
    for g, (a, c_) in enumerate(zip(_split_groups(ks), _split_groups(kw))):
        ks_ref[0, g] = a.astype(BF16)
        kw_ref[0, g] = c_.astype(BF16)
        vs_ref[0, g] = vs_t[g * HEAD_DIM:(g + 1) * HEAD_DIM].astype(BF16)
        vw_ref[0, g] = vw_t[g * HEAD_DIM:(g + 1) * HEAD_DIM].astype(BF16)


def _nsa_prep(proj3, cos, sin_signed, q_norm, k_norm, tq=512):
    b, s, _ = proj3.shape
    kvw = NSA_KV_WIDTH
    kv_spec = lambda j: pl.BlockSpec((1, tq, kvw), lambda i, t: (i, t, COL_KV // kvw + j))
    out_k = pl.BlockSpec((1, NSA_KV_HEADS, tq, HEAD_DIM), lambda i, t: (i, 0, t, 0))
    out_vt = pl.BlockSpec((1, NSA_KV_HEADS, HEAD_DIM, tq), lambda i, t: (i, 0, 0, t))
    k_shape = jax.ShapeDtypeStruct((b, NSA_KV_HEADS, s, HEAD_DIM), BF16)
    vt_shape = jax.ShapeDtypeStruct((b, NSA_KV_HEADS, HEAD_DIM, s), BF16)
    return pl.pallas_call(
        _nsa_prep_kernel,
        grid=(b, s // tq),
        in_specs=[
            pl.BlockSpec((1, tq, NSA_WIDTH), lambda i, t: (i, t, COL_QNSA // NSA_WIDTH)),
            kv_spec(0), kv_spec(1), kv_spec(2), kv_spec(3),
            pl.BlockSpec((tq, NSA_WIDTH), lambda i, t: (t, 0)),
            pl.BlockSpec((tq, NSA_WIDTH), lambda i, t: (t, 0)),
            pl.BlockSpec((1, NSA_WIDTH), lambda i, t: (0, 0)),
            pl.BlockSpec((3, kvw), lambda i, t: (0, 0)),
        ],
        out_specs=[
            pl.BlockSpec((1, NSA_HEADS, HEAD_DIM, tq), lambda i, t: (i, 0, 0, t)),
            out_k, out_vt, out_k, out_vt,
        ],
        out_shape=[jax.ShapeDtypeStruct((b, NSA_HEADS, HEAD_DIM, s), BF16), k_shape, vt_shape, k_shape, vt_shape],
        compiler_params=_params(("parallel", "parallel")),
        name="nsa_prep",
    )(proj3, proj3, proj3, proj3, proj3, cos, sin_signed, q_norm, k_norm)


def _gelu_tanh(x):
    return 0.5 * x * (1.0 + jnp.tanh(0.7978845608028654 * (x + 0.044715 * x * x * x)))


def _nsa_cmp_kernel(kc_in_ref, vc_in_ref, cos_ref, sin_ref, kn_ref, pe_ref, w1_ref, b1_ref, w2_ref,
                    kc_ref, vc_ref):
    n_cmp = (kc_in_ref.shape[1] - CMP_LEN) // CMP_STRIDE + 1
    n_pad = n_cmp + 1
    zero_row = jnp.zeros((1, NSA_KV_WIDTH), F32)
    outs = []
    for j, src in enumerate((kc_in_ref, vc_in_ref)):
        acc = jnp.zeros((NSA_KV_HEADS * n_pad, CMP_HIDDEN), F32)
        for l in range(CMP_LEN):
            x = src[0, pl.ds(l, n_cmp, stride=CMP_STRIDE), :]
            x = jnp.concatenate([x, zero_row], axis=0)
            xg = jnp.concatenate(_split_groups(x), axis=0) + pe_ref[j, l:l + 1, :]
            acc = acc + _dot(xg.astype(BF16), w1_ref[j, l * HEAD_DIM:(l + 1) * HEAD_DIM, :])
        hid = _gelu_tanh(acc + b1_ref[j])
        out = _dot(hid.astype(BF16), w2_ref[j])
        outs.append(jnp.concatenate([out[g * n_pad:(g + 1) * n_pad] for g in range(NSA_KV_HEADS)], axis=1))
    kc, vc = outs
    seg_k = _seg_matrix(NSA_KV_WIDTH, HEAD_DIM).astype(BF16)
    kc = _rope(_head_rmsnorm(kc, kn_ref[0:1, :], seg_k), cos_ref[...], sin_ref[...])
    vc_t = vc.T
    for g, a in enumerate(_split_groups(kc)):
        kc_ref[0, g] = a.astype(BF16)
        vc_ref[0, g] = vc_t[g * HEAD_DIM:(g + 1) * HEAD_DIM].astype(BF16)


def _nsa_cmp(proj3, cos_c, sin_c, k_norm, pe, w1, b1, w2):
    b, s, _ = proj3.shape
    kvw = NSA_KV_WIDTH
    n_pad = (s - CMP_LEN) // CMP_STRIDE + 2
    full = lambda shape: pl.BlockSpec(shape, lambda i: (0,) * len(shape))
    k_spec = pl.BlockSpec((1, NSA_KV_HEADS, n_pad, HEAD_DIM), lambda i: (i, 0, 0, 0))
    vt_spec = pl.BlockSpec((1, NSA_KV_HEADS, HEAD_DIM, n_pad), lambda i: (i, 0, 0, 0))
    k_shape = jax.ShapeDtypeStruct((b, NSA_KV_HEADS, n_pad, HEAD_DIM), BF16)
    vt_shape = jax.ShapeDtypeStruct((b, NSA_KV_HEADS, HEAD_DIM, n_pad), BF16)
    return pl.pallas_call(
        _nsa_cmp_kernel,
        grid=(b,),
        in_specs=[
            pl.BlockSpec((1, s, kvw), lambda i: (i, 0, COL_CMP // kvw)),
            pl.BlockSpec((1, s, kvw), lambda i: (i, 0, COL_CMP // kvw + 1)),
            full((n_pad, kvw)), full((n_pad, kvw)), full((3, kvw)),
            full((2, CMP_LEN, HEAD_DIM)), full((2, CMP_LEN * HEAD_DIM, CMP_HIDDEN)),
            full((2, 1, CMP_HIDDEN)), full((2, CMP_HIDDEN, HEAD_DIM)),
        ],
        out_specs=[k_spec, vt_spec],
        out_shape=[k_shape, vt_shape],
        compiler_params=_params(("parallel",)),
        name="nsa_compress",
    )(proj3, proj3, cos_c, sin_c, k_norm, pe, w1, b1, w2)


def _nsa_attn_kernel(q_ref, gate_ref, kc_ref, vc_ref, ks_ref, vs_ref, kw_ref, vw_ref, o_ref):
    qb = Q_BLOCK
    hg = NSA_GROUP
    g = pl.program_id(1)
    c = pl.program_id(2)
    cols = hg * qb
    qt = jnp.concatenate([q_ref[0, h] for h in range(hg)], axis=1)
    tile_heads = lambda x: jnp.concatenate([x] * hg, axis=1)

    n_cp = kc_ref.shape[2]
    s_c = _dot(kc_ref[0, 0], qt)
    cend = lax.broadcasted_iota(jnp.int32, (n_cp, qb), 0) * CMP_STRIDE + CMP_LEN - 1
    cmask = tile_heads(jnp.where(cend <= c * qb + lax.broadcasted_iota(jnp.int32, (n_cp, qb), 1), 1, 0)) > 0
    s_m = jnp.where(cmask, s_c, -jnp.inf)
    m = jnp.max(s_m, axis=0, keepdims=True)
    m = jnp.where(m == -jnp.inf, 0.0, m)
    e = jnp.where(cmask, jnp.exp2(s_m - m), 0.0)
    p_c = e / jnp.maximum(jnp.sum(e, axis=0, keepdims=True), 1e-30)
    o_cmp = _dot(vc_ref[0, 0], p_c.astype(BF16))

    n_blk = ks_ref.shape[2] // SEL_BLOCK
    p_sum = p_c[:, 0:qb]
    for h in range(1, hg):
        p_sum = p_sum + p_c[:, h * qb:(h + 1) * qb]
    bi = lax.broadcasted_iota(jnp.int32, (n_blk, n_cp), 0) * SEL_BLOCK
    ci = lax.broadcasted_iota(jnp.int32, (n_blk, n_cp), 1) * CMP_STRIDE
    cover_t = ((ci < bi + SEL_BLOCK) & (ci + CMP_LEN > bi)).astype(F32)
    imp = _dot(cover_t, p_sum, HI)
    blk = lax.broadcasted_iota(jnp.int32, (n_blk, qb), 0)
    qpos = c * qb + lax.broadcasted_iota(jnp.int32, (n_blk, qb), 1)
    cur = qpos // SEL_BLOCK
    forced = (blk == 0) | (blk == cur) | (blk == cur - 1)
    imp = jnp.where(forced, imp + FORCE_BONUS, imp)
    imp = jnp.where(blk <= cur, imp, -jnp.inf)
    rank = jnp.zeros((n_blk, qb), F32)
    for j in range(n_blk):
        other = imp[j:j + 1, :]
        rank = rank + jnp.where(blk > j, jnp.where(other >= imp, 1.0, 0.0), jnp.where(other > imp, 1.0, 0.0))
    sel_t = jnp.where(rank < min(TOP_N, n_blk), 1.0, 0.0).astype(BF16)

    kt_w = 4 * LANE
    kq = c * qb + lax.broadcasted_iota(jnp.int32, (kt_w, qb), 1)
    krel = lax.broadcasted_iota(jnp.int32, (kt_w, qb), 0)
    er = lax.broadcasted_iota(jnp.int32, (kt_w, n_blk), 0) // SEL_BLOCK
    ec = lax.broadcasted_iota(jnp.int32, (kt_w, n_blk), 1)

    def sel_body(kt, carry):
        m_i, l_i, acc = carry
        off = pl.multiple_of(kt * kt_w, kt_w)
        expand = jnp.where(ec == er + kt * (kt_w // SEL_BLOCK), 1.0, 0.0).astype(BF16)
        in_sel = _dot(expand, sel_t)
        bias = jnp.where((in_sel > 0.5) & (krel + off <= kq), 0.0, NEG_BIG)
        s = _dot(ks_ref[0, 0, pl.ds(off, kt_w), :], qt) + tile_heads(bias)
        m_new = jnp.maximum(m_i, jnp.max(s, axis=0, keepdims=True))
        alpha = jnp.exp2(m_i - m_new)
        p = jnp.exp2(s - m_new)
        l_new = alpha * l_i + jnp.sum(p, axis=0, keepdims=True)
        acc = alpha * acc + _dot(vs_ref[0, 0, :, pl.ds(off, kt_w)], p.astype(BF16))
        return m_new, l_new, acc

    init = (jnp.full((1, cols), NEG_BIG, F32), jnp.zeros((1, cols), F32), jnp.zeros((HEAD_DIM, cols), F32))
    _, l_s, acc_s = lax.fori_loop(0, (c * qb + qb + kt_w - 1) // kt_w, sel_body, init)
    o_slc = acc_s / l_s

    span = WINDOW + 2 * qb
    w0 = pl.multiple_of(jnp.clip(c * qb - WINDOW, 0, kw_ref.shape[2] - span), qb)
    kpos = w0 + lax.broadcasted_iota(jnp.int32, (span, qb), 0)
    wq = c * qb + lax.broadcasted_iota(jnp.int32, (span, qb), 1)
    wbias = jnp.where((kpos <= wq) & (kpos > wq - WINDOW), 0.0, NEG_BIG)
    s_w = _dot(kw_ref[0, 0, pl.ds(w0, span), :], qt) + tile_heads(wbias)
    e_w = jnp.exp2(s_w - jnp.max(s_w, axis=0, keepdims=True))
    o_win = _dot(vw_ref[0, 0, :, pl.ds(w0, span)], e_w.astype(BF16)) / jnp.sum(e_w, axis=0, keepdims=True)

    gates_t = jax.nn.sigmoid(gate_ref[0].astype(F32)).T
    grow = lax.broadcasted_iota(jnp.int32, gates_t.shape, 0)
    outs = []
    for h in range(hg):
        hc = slice(h * qb, (h + 1) * qb)
        first = (g * hg + h) * 3
        pick = lambda j: jnp.sum(jnp.where(grow == first + j, gates_t, 0.0), axis=0, keepdims=True)
        outs.append(pick(0) * o_cmp[:, hc] + pick(1) * o_slc[:, hc] + pick(2) * o_win[:, hc])
    o_ref[0] = jnp.concatenate(outs, axis=0).T


def _nsa_attn(q, proj3, kc, vc, ks, vs, kw, vw):
    b, _, _, s = q.shape
    gw = NSA_GROUP * HEAD_DIM
    n_pad = kc.shape[2]
    per_group = lambda rows, width: pl.BlockSpec((1, 1, rows, width), lambda i, g, c: (i, g, 0, 0))
    return pl.pallas_call(
        _nsa_attn_kernel,
        grid=(b, NSA_KV_HEADS, s // Q_BLOCK),
        in_specs=[
            pl.BlockSpec((1, NSA_GROUP, HEAD_DIM, Q_BLOCK), lambda i, g, c: (i, g, 0, c)),
            pl.BlockSpec((1, Q_BLOCK, LANE), lambda i, g, c: (i, c, COL_GATE // LANE)),
            per_group(n_pad, HEAD_DIM), per_group(HEAD_DIM, n_pad),
            per_group(s, HEAD_DIM), per_group(HEAD_DIM, s), per_group(s, HEAD_DIM), per_group(HEAD_DIM, s),
        ],
        out_specs=pl.BlockSpec((1, Q_BLOCK, gw), lambda i, g, c: (i, c, g)),
        out_shape=jax.ShapeDtypeStruct((b, s, NSA_WIDTH), F32),
        compiler_params=_params(("parallel", "parallel", "arbitrary")),
        name="nsa_attn",
    )(q, proj3, kc, vc, ks, vs, kw, vw)


def _mem_kv_kernel(mem_ref, g_ref, w_ref, kn_ref, k_ref, v_ref):
    x = mem_ref[0]
    ms = jnp.mean(x * x, axis=-1, keepdims=True)
    xn = (x * lax.rsqrt(ms + RMS_EPS) * g_ref[...]).astype(BF16)
    kv = _dot(xn, w_ref[...])
    for h in range(MEM_HEADS):
        sl = slice(h * MEM_HEAD_DIM, (h + 1) * MEM_HEAD_DIM)
        kh = kv[:, sl]
        kms = jnp.mean(kh * kh, axis=-1, keepdims=True)
        k_ref[0, :, sl] = (kh * lax.rsqrt(kms + RMS_EPS) * kn_ref[...]).astype(BF16)
    v_ref[0] = kv[:, MEM_WIDTH:].astype(BF16)


def _mem_kv(mem, g_mem, w_kv, k_norm):
    b, m, d = mem.shape
    spec = pl.BlockSpec((1, m, MEM_WIDTH), lambda i: (i, 0, 0))
    shape = jax.ShapeDtypeStruct((b, m, MEM_WIDTH), BF16)
    return pl.pallas_call(
        _mem_kv_kernel,
        grid=(b,),
        in_specs=[
            pl.BlockSpec((1, m, d), lambda i: (i, 0, 0)),
            pl.BlockSpec((1, d), lambda i: (0, 0)),
            pl.BlockSpec((d, 2 * MEM_WIDTH), lambda i: (0, 0)),
            pl.BlockSpec((1, MEM_HEAD_DIM), lambda i: (0, 0)),
        ],
        out_specs=[spec, spec],
        out_shape=[shape, shape],
        compiler_params=_params(("parallel",)),
        name="mem_kv",
    )(mem, g_mem, w_kv, k_norm)


def _mem_attn_kernel(q_ref, qn_ref, k_ref, v_ref, o_ref):
    q = q_ref[0].astype(F32)
    for h in range(MEM_HEADS):
        sl = slice(h * MEM_HEAD_DIM, (h + 1) * MEM_HEAD_DIM)
        qh = q[:, sl]
        ms = jnp.mean(qh * qh, axis=-1, keepdims=True)
        qh = (qh * lax.rsqrt(ms + RMS_EPS) * qn_ref[...]).astype(BF16)
        s = _dot_nt(qh, k_ref[0, :, sl]) * (MEM_HEAD_DIM ** -0.5)
        m = jnp.max(s, axis=-1, keepdims=True)
        e = jnp.exp(s - m)
        p = e / jnp.sum(e, axis=-1, keepdims=True)
        o_ref[0, :, sl] = _dot(p.astype(BF16), v_ref[0, :, sl])


def _mem_attn(proj3, q_norm, k, v, tq=512):
    b, s, _ = proj3.shape
    m = k.shape[1]
    kv_spec = pl.BlockSpec((1, m, MEM_WIDTH), lambda i, t: (i, 0, 0))
    return pl.pallas_call(
        _mem_attn_kernel,
        grid=(b, s // tq),
        in_specs=[
            pl.BlockSpec((1, tq, MEM_WIDTH), lambda i, t: (i, t, COL_QMEM // MEM_WIDTH)),
            pl.BlockSpec((1, MEM_HEAD_DIM), lambda i, t: (0, 0)),
            kv_spec, kv_spec,
        ],
        out_specs=pl.BlockSpec((1, tq, MEM_WIDTH), lambda i, t: (i, t, 0)),
        out_shape=jax.ShapeDtypeStruct((b, s, MEM_WIDTH), F32),
        compiler_params=_params(("parallel", "parallel")),
        name="mem_attn",
    )(proj3, q_norm, k, v)


def _merge_kernel(oa_ref, ob_ref, om_ref, l0_ref, l1_ref, l2_ref, x_ref, bm_ref, wb_ref, wo_ref,
                  gf_ref, rw_ref, rb_ref, x1_ref, h_ref, route_ref):
    mixed = None
    for n, (o_ref, l_ref) in enumerate(((oa_ref, l0_ref), (ob_ref, l1_ref), (om_ref, l2_ref))):
        gate = jax.nn.sigmoid(l_ref[...].astype(F32) + bm_ref[:, n * D_MODEL:(n + 1) * D_MODEL])
        term = gate * _dot(o_ref[...].astype(BF16), wb_ref[n])
        mixed = term if mixed is None else mixed + term
    x1 = x_ref[...] + _dot(mixed.astype(BF16), wo_ref[...])
    x1_ref[...] = x1
    ms = jnp.mean(x1 * x1, axis=-1, keepdims=True)
    h = x1 * lax.rsqrt(ms + RMS_EPS) * gf_ref[...]
    _store_slabs(h_ref, h)

    logits = _dot(h, rw_ref[...], HI) + rb_ref[...]
    lane_f = lax.broadcasted_iota(jnp.int32, logits.shape, 1).astype(F32)
    work = logits
    picks = []
    for _ in range(TOP_K):
        mx = jnp.max(work, axis=-1, keepdims=True)
        idx = jnp.min(jnp.where(work == mx, lane_f, 2.0 * LANE), axis=-1, keepdims=True)
        picks.append((idx, mx))
        work = jnp.where(lane_f == idx, -jnp.inf, work)
    exps = [jnp.exp(mx - picks[0][1]) for _, mx in picks]
    denom = functools.reduce(lambda a, b: a + b, exps)
    route = jnp.zeros(logits.shape, F32)
    for kk, ((idx, _), ex) in enumerate(zip(picks, exps)):
        route = jnp.where(lane_f == kk, idx, route)
        route = jnp.where(lane_f == TOP_K + kk, ex / denom, route)
    route_ref[...] = route


def _merge(o_a, o_b, o_m, proj2, x2, b_merge, w_branch, w_out, g_ffn, router_w, router_b, tm=512):
    t, d = x2.shape
    row = lambda width: pl.BlockSpec((tm, width), lambda i: (i, 0))
    logit = lambda n: pl.BlockSpec((tm, d), lambda i: (i, COL_MERGE // d + n))
    full = lambda shape: pl.BlockSpec(shape, lambda i: (0,) * len(shape))
    return pl.pallas_call(
        _merge_kernel,
        grid=(t // tm,),
        in_specs=[
            row(BRANCH_WIDTH), row(BRANCH_WIDTH), row(BRANCH_WIDTH), logit(0), logit(1), logit(2), row(d),
            full((1, N_BRANCHES * d)), full((N_BRANCHES, BRANCH_WIDTH, d)), full((d, d)), full((1, d)),
            full((d, LANE)), full((1, LANE)),
        ],
        out_specs=[row(d), pl.BlockSpec((tm * SUBLANES, LANE), lambda i: (i, 0)), row(LANE)],
        out_shape=[jax.ShapeDtypeStruct((t, d), F32), jax.ShapeDtypeStruct((t * SUBLANES, LANE), F32),
                   jax.ShapeDtypeStruct((t, LANE), F32)],
        compiler_params=_params(("parallel",)),
        name="merge_router",
    )(o_a, o_b, o_m, proj2, proj2, proj2, x2, b_merge, w_branch, w_out, g_ffn, router_w, router_b)


def _deinterleave(x):
    rows, width = x.shape
    lane = lax.broadcasted_iota(jnp.int32, (rows, LANE), 1)
    half = LANE // 2
    low = lane < half
    idx = jnp.where(low, 2 * lane, 2 * (lane - half) + 1)
    evens, odds = [], []
    for j in range(0, width, 2 * LANE):
        a = jnp.take_along_axis(x[:, j:j + LANE], idx, axis=1)
        b = jnp.take_along_axis(x[:, j + LANE:j + 2 * LANE], idx, axis=1)
        evens.append(jnp.where(low, a, pltpu.roll(b, half, axis=1)))
        odds.append(jnp.where(low, pltpu.roll(a, half, axis=1), b))
    return jnp.concatenate(evens, axis=1), jnp.concatenate(odds, axis=1)


MOE_TILE = 512


def _route_kernel(route_ref, dest_ref, counts_ref, carry_ref, start_ref):
    phase = pl.program_id(0)
    i = pl.program_id(1)
    tr = route_ref.shape[0]
    route = route_ref[...]
    lane = lax.broadcasted_iota(jnp.int32, route.shape, 1)
    lane_f = lane.astype(F32)
    hits = [lane_f == route[:, kk:kk + 1] for kk in range(TOP_K)]
    sel = jnp.zeros(route.shape, F32)
    for hit in hits:
        sel = jnp.where(hit, 1.0, sel)

    @pl.when((phase == 0) & (i == 0))
    def _():
        carry_ref[...] = jnp.zeros_like(carry_ref)

    @pl.when((phase == 1) & (i == 0))
    def _():
        counts = carry_ref[...]
        counts_ref[...] = counts
        padded = jnp.ceil(counts * (1.0 / MOE_TILE)) * MOE_TILE
        ui = lax.broadcasted_iota(jnp.int32, (LANE, LANE), 0)
        uj = lax.broadcasted_iota(jnp.int32, (LANE, LANE), 1)
        start_ref[...] = _dot01_right(jnp.broadcast_to(padded, (8, LANE)), (ui < uj).astype(BF16))[0:1]
        carry_ref[...] = jnp.zeros_like(carry_ref)

    @pl.when(phase == 1)
    def _():
        ri = lax.broadcasted_iota(jnp.int32, (tr, tr), 0)
        rj = lax.broadcasted_iota(jnp.int32, (tr, tr), 1)
        rank = carry_ref[...] + _dot((rj < ri).astype(BF16), sel.astype(BF16))
        row = start_ref[...] + rank
        dest = jnp.zeros(route.shape, F32)
        for kk, hit in enumerate(hits):
            dest = jnp.where(lane == kk, jnp.sum(jnp.where(hit, row, 0.0), axis=-1, keepdims=True), dest)
        dest_ref[...] = dest.astype(jnp.int32)

    carry_ref[...] += jnp.sum(sel, axis=0, keepdims=True)


def _route(route, tr=512):
    t = route.shape[0]
    return pl.pallas_call(
        _route_kernel,
        grid=(2, t // tr),
        in_specs=[pl.BlockSpec((tr, LANE), lambda p, i: (i, 0))],
        out_specs=[pl.BlockSpec((tr, LANE), lambda p, i: (i * p, 0)), pl.BlockSpec((1, LANE), lambda p, i: (0, 0))],
        out_shape=[jax.ShapeDtypeStruct((t, LANE), jnp.int32), jax.ShapeDtypeStruct((1, LANE), F32)],
        scratch_shapes=[pltpu.VMEM((1, LANE), F32), pltpu.VMEM((1, LANE), F32)],
        compiler_params=_params(("arbitrary", "arbitrary")),
        name="moe_route",
    )(route)


def _row_copy(src, src_row, dst, dst_row, sem):
    src_at = pl.ds(pl.multiple_of(src_row * SUBLANES, SUBLANES), SUBLANES)
    dst_at = pl.ds(pl.multiple_of(dst_row * SUBLANES, SUBLANES), SUBLANES)
    return pltpu.make_async_copy(src.at[src_at], dst.at[dst_at], sem)


def _dispatch_kernel(dest_ref, h_ref, xs_in_ref, xs_ref, sem):
    del xs_in_ref
    td = h_ref.shape[0] // SUBLANES

    def issue(r, carry):
        for kk in range(TOP_K):
            _row_copy(h_ref, r, xs_ref, dest_ref[r * TOP_K + kk], sem).start(priority=kk % 2)
        return carry

    lax.fori_loop(0, td, issue, 0, unroll=8)

    def drain(r, carry):
        for kk in range(TOP_K):
            _row_copy(h_ref, 0, xs_ref, 0, sem).wait()
        return carry

    lax.fori_loop(0, td, drain, 0, unroll=8)


def _dispatch(dest_flat, h, xs_zero, td=512):
    t = h.shape[0] // SUBLANES
    return pl.pallas_call(
        _dispatch_kernel,
        grid=(t // td,),
        in_specs=[
            pl.BlockSpec((td * TOP_K,), lambda i: (i,), memory_space=pltpu.SMEM),
            pl.BlockSpec((td * SUBLANES, LANE), lambda i: (i, 0)),
            pl.BlockSpec(memory_space=pl.ANY),
        ],
        out_specs=pl.BlockSpec(memory_space=pl.ANY),
        out_shape=jax.ShapeDtypeStruct(xs_zero.shape, xs_zero.dtype),
        scratch_shapes=[pltpu.SemaphoreType.DMA],
        input_output_aliases={2: 0},
        compiler_params=_params(("arbitrary",)),
        name="moe_dispatch",
    )(dest_flat, h, xs_zero)


def _experts_kernel(te_ref, nu_ref, xs_ref, w1_ref, b1g_ref, b1l_ref, w2_ref, b2_ref, y_ref, w1b_ref, w2b_ref):
    j = pl.program_id(0)

    @pl.when(j < nu_ref[0])
    def _():
        @pl.when((j == 0) | (te_ref[j] != te_ref[jnp.maximum(j - 1, 0)]))
        def _():
            w1b_ref[...] = w1_ref[0].astype(BF16)
            w2b_ref[...] = w2_ref[0].astype(BF16)

        x = _load_slabs(xs_ref, MOE_TILE).astype(BF16)
        hid_g, hid_l = _deinterleave(_dot(x, w1b_ref[...]))
        x_glu = jnp.minimum(hid_g + b1g_ref[0], SWIGLU_LIMIT)
        x_lin = jnp.clip(hid_l + b1l_ref[0], -SWIGLU_LIMIT, SWIGLU_LIMIT)
        act = x_glu * jax.nn.sigmoid(SWIGLU_ALPHA * x_glu) * (x_lin + 1.0)
        _store_slabs(y_ref, _dot(act.astype(BF16), w2b_ref[...]) + b2_ref[0])

    @pl.when(j >= nu_ref[0])
    def _():
        y_ref[...] = jnp.zeros_like(y_ref)


def _experts(tile_expert, n_used, xs, w1, b1g, b1l, w2, b2):
    rows = xs.shape[0] // SUBLANES
    n_e, ff, d = w2.shape
    tg = MOE_TILE
    used = lambda j, te, nu: jnp.minimum(j, nu[0] - 1)
    exp_of = lambda j, te, nu: te[used(j, te, nu)]
    return pl.pallas_call(
        _experts_kernel,
        grid_spec=pltpu.PrefetchScalarGridSpec(
            num_scalar_prefetch=2,
            grid=(rows // tg,),
            in_specs=[
                pl.BlockSpec((tg * SUBLANES, LANE), lambda j, te, nu: (used(j, te, nu), 0)),
                pl.BlockSpec((1, d, 2 * ff), lambda j, te, nu: (exp_of(j, te, nu), 0, 0)),
                pl.BlockSpec((1, 1, ff), lambda j, te, nu: (exp_of(j, te, nu), 0, 0)),
                pl.BlockSpec((1, 1, ff), lambda j, te, nu: (exp_of(j, te, nu), 0, 0)),
                pl.BlockSpec((1, ff, d), lambda j, te, nu: (exp_of(j, te, nu), 0, 0)),
                pl.BlockSpec((1, 1, d), lambda j, te, nu: (exp_of(j, te, nu), 0, 0)),
            ],
            out_specs=pl.BlockSpec((tg * SUBLANES, LANE), lambda j, te, nu: (j, 0)),
            scratch_shapes=[pltpu.VMEM((d, 2 * ff), BF16), pltpu.VMEM((ff, d), BF16)],
        ),
        out_shape=jax.ShapeDtypeStruct((rows * SUBLANES, LANE), F32),
        compiler_params=_params(("arbitrary",)),
        name="moe_experts",
    )(tile_expert, n_used, xs, w1, b1g, b1l, w2, b2)


def _combine_kernel(dest_ref, dest_next_ref, route_ref, x1_ref, y_ref, o_ref, buf_ref, sems):
    i = pl.program_id(0)
    tc = x1_ref.shape[0]
    slot = i % 2

    def gather(dref, to_slot):
        def issue(r, carry):
            for kk in range(TOP_K):
                _row_copy(y_ref, dref[r * TOP_K + kk], buf_ref.at[to_slot, kk], r, sems.at[to_slot]).start(
                    priority=kk % 2)
            return carry

        lax.fori_loop(0, tc, issue, 0, unroll=8)

    @pl.when(i == 0)
    def _():
        gather(dest_ref, slot)

    @pl.when(i + 1 < pl.num_programs(0))
    def _():
        gather(dest_next_ref, 1 - slot)

    def drain(r, carry):
        for kk in range(TOP_K):
            _row_copy(y_ref, 0, buf_ref.at[slot, kk], 0, sems.at[slot]).wait()
        return carry

    lax.fori_loop(0, tc, drain, 0, unroll=8)

    route = route_ref[...]
    acc = x1_ref[...]
    for kk in range(TOP_K):
        acc = acc + route[:, TOP_K + kk:TOP_K + kk + 1] * _load_slabs(buf_ref, tc, lead=(slot, kk))
    o_ref[...] = acc


def _combine(dest_flat, route, x1, y, tc=256):
    t, d = x1.shape
    last = t // tc - 1
    return pl.pallas_call(
        _combine_kernel,
        grid=(t // tc,),
        in_specs=[
            pl.BlockSpec((tc * TOP_K,), lambda i: (i,), memory_space=pltpu.SMEM),
            pl.BlockSpec((tc * TOP_K,), lambda i: (jnp.minimum(i + 1, last),), memory_space=pltpu.SMEM),
            pl.BlockSpec((tc, LANE), lambda i: (i, 0)),
            pl.BlockSpec((tc, d), lambda i: (i, 0)),
            pl.BlockSpec(memory_space=pl.ANY),
        ],
        out_specs=pl.BlockSpec((tc, d), lambda i: (i, 0)),
        out_shape=jax.ShapeDtypeStruct((t, d), F32),
        scratch_shapes=[pltpu.VMEM((2, TOP_K, tc * SUBLANES, LANE), F32), pltpu.SemaphoreType.DMA((2,))],
        compiler_params=_params(("arbitrary",)),
        name="moe_combine",
    )(dest_flat, dest_flat, route, x1, y)


def _moe(h, route, x1, w1, b1g, b1l, w2, b2):
    t, d = x1.shape
    assert d == SUBLANES * LANE
    n_e = w2.shape[0]
    n_tiles = (t * TOP_K) // MOE_TILE + n_e
    dest, counts = _route(route)
    tiles_per = jnp.ceil(counts[0, :n_e] * (1.0 / MOE_TILE)).astype(jnp.int32)
    tile_end = jnp.cumsum(tiles_per)
    past = (tile_end[None, :] <= jnp.arange(n_tiles, dtype=jnp.int32)[:, None]).astype(jnp.int32)
    tile_expert = jnp.minimum(jnp.sum(past, axis=1), n_e - 1).astype(jnp.int32)
    n_used = tile_end[-1:].astype(jnp.int32)
    dest_flat = dest[:, :TOP_K].reshape(-1)
    xs = _dispatch(dest_flat, h, jnp.zeros((n_tiles * MOE_TILE * SUBLANES, LANE), F32))
    y = _experts(tile_expert, n_used, xs, w1, b1g, b1l, w2, b2)
    return _combine(dest_flat, route, x1, y)


def _pad_cols(w, width):
    return jnp.pad(w, ((0, 0), (0, width - w.shape[1])))


def _rope_tables(pos, reps):
    inv = ROPE_THETA ** (-jnp.arange(0, HEAD_DIM, 2, dtype=F32) / HEAD_DIM)
    ang = pos.astype(F32)[:, None] * inv[None, :]
    ang = jnp.concatenate([ang, ang], axis=-1)
    sign = jnp.concatenate([-jnp.ones((HEAD_DIM // 2,), F32), jnp.ones((HEAD_DIM // 2,), F32)])
    return jnp.tile(jnp.cos(ang), (1, reps)), jnp.tile(jnp.sin(ang) * sign, (1, reps))


def _layer(x, mem, g_mix, g_mem, w_in, b_merge, nsa_q_norm, nsa_k_norm, cmp_pe, cmp_w1, cmp_b1, cmp_w2,
           rwkv_shift_mix, rwkv_w0, rwkv_w_up, rwkv_a0, rwkv_a_up, rwkv_g_up, rwkv_k_k, rwkv_k_a, rwkv_r_k,
           rwkv_ln_w, rwkv_ln_b, mem_w_kv, mem_q_norm, mem_k_norm, w_branch, w_out, g_ffn,
           router_w, router_b, exp_w1, exp_b1, exp_w2, exp_b2, cos, sin, cos_c, sin_c):
    b, s, d = x.shape
    t = b * s
    x2 = x.reshape(t, d)

    o = 0
    parts = []
    for width in (NSA_WIDTH, 6 * NSA_KV_WIDTH, 3 * NSA_HEADS,
                  3 * RWKV_WIDTH + DECAY_RANK + AAA_RANK + GATE_RANK, MEM_WIDTH, N_BRANCHES * D_MODEL):
        parts.append(w_in[:, o:o + width])
        o += width
    w_q, w_kv, w_gate, w_rwkv, w_qm, w_merge = parts

    def rwkv_layout(m):
        r3 = m[:, :3 * RWKV_WIDTH]
        xw = m[:, 3 * RWKV_WIDTH:3 * RWKV_WIDTH + DECAY_RANK]
        xa = m[:, 3 * RWKV_WIDTH + DECAY_RANK:3 * RWKV_WIDTH + DECAY_RANK + AAA_RANK]
        xg = m[:, 3 * RWKV_WIDTH + DECAY_RANK + AAA_RANK:]
        return _pad_cols(jnp.concatenate([r3, _pad_cols(xw, LANE), _pad_cols(xa, LANE), xg], axis=1), RWKV_PAD)

    w_f32 = jnp.concatenate([rwkv_layout(w_rwkv), w_kv[:, :2 * NSA_KV_WIDTH]], axis=1).astype(BF16)
    w_b16 = jnp.concatenate([w_merge, w_q, w_qm, w_kv[:, 2 * NSA_KV_WIDTH:], w_gate], axis=1)
    w_b16 = _pad_cols(w_b16, PROJ_BF16_WIDTH).astype(BF16)
    g_row = g_mix.reshape(1, d)
    proj3 = _inproj(x2, g_row, w_f32, F32, PROJ_F32_WIDTH).reshape(b, s, PROJ_F32_WIDTH)
    proj2 = _inproj(x2, g_row, w_b16, BF16, PROJ_BF16_WIDTH // 2)
    proj3b = proj2.reshape(b, s, PROJ_BF16_WIDTH)

    rowv = lambda a: a.reshape(1, -1)
    pad_rows = lambda m: jnp.pad(m, ((0, LANE - m.shape[0]), (0, 0)))
    o_b = _rwkv(proj3, rwkv_layout(rowv(rwkv_shift_mix)), rowv(rwkv_w0), pad_rows(rwkv_w_up), rowv(rwkv_a0),
                pad_rows(rwkv_a_up), rwkv_g_up, rowv(rwkv_k_k), rowv(rwkv_k_a), rowv(rwkv_r_k),
                rowv(rwkv_ln_w), rowv(rwkv_ln_b))

    q_gain = jnp.tile(nsa_q_norm.reshape(1, HEAD_DIM), (1, NSA_HEADS))
    k_gain = jnp.tile(nsa_k_norm, (1, NSA_KV_HEADS))
    qn, ks, vs, kw, vw = _nsa_prep(proj3b, cos, sin, q_gain, k_gain)
    kc, vc = _nsa_cmp(proj3, cos_c, sin_c, k_gain, cmp_pe, cmp_w1.astype(BF16),
                      cmp_b1.reshape(2, 1, CMP_HIDDEN), cmp_w2.astype(BF16))
    o_a = _nsa_attn(qn, proj3b, kc, vc, ks, vs, kw, vw)

    mk, mv = _mem_kv(mem, g_mem.reshape(1, d), mem_w_kv.astype(BF16), mem_k_norm.reshape(1, MEM_HEAD_DIM))
    o_m = _mem_attn(proj3b, mem_q_norm.reshape(1, MEM_HEAD_DIM), mk, mv)

    rw = _pad_cols(router_w, LANE)
    rb = jnp.concatenate([router_b, jnp.full((LANE - N_EXPERTS,), NEG_BIG, F32)]).reshape(1, LANE)
    x1, h2, route = _merge(o_a.reshape(t, NSA_WIDTH), o_b.reshape(t, RWKV_WIDTH), o_m.reshape(t, MEM_WIDTH),
                          proj2, x2, b_merge.reshape(1, -1), w_branch.astype(BF16), w_out.astype(BF16),
                          g_ffn.reshape(1, d), rw, rb)

    b1g = exp_b1[:, None, 0::2]
    b1l = exp_b1[:, None, 1::2]
    out = _moe(h2, route, x1, exp_w1, b1g, b1l, exp_w2, exp_b2[:, None, :])
    return out.reshape(b, s, d)


def kernel(x, mem, g_mix, g_mem, w_in, b_merge, nsa_q_norm, nsa_k_norm, cmp_pe, cmp_w1, cmp_b1, cmp_w2,
           rwkv_shift_mix, rwkv_w0, rwkv_w_up, rwkv_a0, rwkv_a_up, rwkv_g_up, rwkv_k_k, rwkv_k_a, rwkv_r_k,
           rwkv_ln_w, rwkv_ln_b, mem_w_kv, mem_q_norm, mem_k_norm, w_branch, w_out, g_ffn,
           router_w, router_b, exp_w1, exp_b1, exp_w2, exp_b2):
    s = x.shape[1]
    cos, sin = _rope_tables(jnp.arange(s), NSA_HEADS)
    n_cmp = (s - CMP_LEN) // CMP_STRIDE + 1
    cos_c, sin_c = _rope_tables(jnp.arange(n_cmp + 1) * CMP_STRIDE + CMP_LEN - 1, NSA_KV_HEADS)
    depth = g_mix.shape[0]
    for l in range(depth):
        x = _layer(x, mem, g_mix[l], g_mem[l], w_in[l], b_merge[l], nsa_q_norm[l], nsa_k_norm[l], cmp_pe[l],
                   cmp_w1[l], cmp_b1[l], cmp_w2[l], rwkv_shift_mix[l], rwkv_w0[l], rwkv_w_up[l], rwkv_a0[l],
                   rwkv_a_up[l], rwkv_g_up[l], rwkv_k_k[l], rwkv_k_a[l], rwkv_r_k[l], rwkv_ln_w[l], rwkv_ln_b[l],
                   mem_w_kv[l], mem_q_norm[l], mem_k_norm[l], w_branch[l], w_out[l], g_ffn[l], router_w[l],
                   router_b[l], exp_w1[l], exp_b1[l], exp_w2[l], exp_b2[l], cos, sin, cos_c, sin_c)
    return x
```

```python
import functools

import jax
import jax.numpy as jnp
from jax import lax
from jax.experimental import pallas as pl
from jax.experimental.pallas import tpu as pltpu

F32 = jnp.float32
BF16 = jnp.bfloat16
HI = lax.Precision.HIGHEST

D_MODEL = 1024
HEAD_DIM = 64
NSA_HEADS = 8
NSA_KV_HEADS = 2
NSA_GROUP = NSA_HEADS // NSA_KV_HEADS
NSA_WIDTH = NSA_HEADS * HEAD_DIM
NSA_KV_WIDTH = NSA_KV_HEADS * HEAD_DIM
CMP_LEN = 32
CMP_STRIDE = 16
CMP_HIDDEN = 128
SEL_BLOCK = 64
TOP_N = 8
WINDOW = 512
Q_BLOCK = 128
FORCE_BONUS = 1000.0
RWKV_HEADS = 8
RWKV_HEAD_DIM = 64
RWKV_WIDTH = RWKV_HEADS * RWKV_HEAD_DIM
DECAY_RANK = 64
AAA_RANK = 64
GATE_RANK = 128
GN_EPS = 64e-5
MEM_HEADS = 4
MEM_HEAD_DIM = 128
MEM_WIDTH = MEM_HEADS * MEM_HEAD_DIM
N_BRANCHES = 3
BRANCH_WIDTH = 512
N_EXPERTS = 32
TOP_K = 4
EXPERT_FF = 1024
SWIGLU_ALPHA = 1.702
SWIGLU_LIMIT = 7.0
ROPE_THETA = 10000.0
RMS_EPS = 1e-6

LANE = 128
LOG2_E = 1.4426950408889634
NEG_BIG = -1e30

RWKV_PAD = 2048
COL_RWKV = 0
COL_CMP = COL_RWKV + RWKV_PAD
PROJ_F32_WIDTH = COL_CMP + 2 * NSA_KV_WIDTH
COL_MERGE = 0
COL_QNSA = COL_MERGE + N_BRANCHES * D_MODEL
COL_QMEM = COL_QNSA + NSA_WIDTH
COL_KV = COL_QMEM + MEM_WIDTH
COL_GATE = COL_KV + 4 * NSA_KV_WIDTH
PROJ_BF16_WIDTH = 5120
RW_R, RW_K, RW_V, RW_XW, RW_XA, RW_XG = 0, 512, 1024, 1536, 1664, 1792

RWKV_CHUNK = 64
VMEM_LIMIT = 56 * 1024 * 1024


def _dot(a, b, prec=None):
    return jnp.dot(a, b, preferred_element_type=F32, precision=prec)


def _dot_nt(a, b, prec=None):
    return lax.dot_general(a, b, (((1,), (1,)), ((), ())), preferred_element_type=F32, precision=prec)


def _dot_tn(a, b, prec=None):
    return lax.dot_general(a, b, (((0,), (0,)), ((), ())), preferred_element_type=F32, precision=prec)


def _split3(x):
    hi = x.astype(BF16)
    r1 = x - hi.astype(F32)
    mid = r1.astype(BF16)
    lo = (r1 - mid.astype(F32)).astype(BF16)
    return hi, mid, lo


def _dot01_left(m01, x):
    n = x.shape[1]
    out = _dot(m01, jnp.concatenate(_split3(x), axis=1))
    return out[:, :n] + out[:, n:2 * n] + out[:, 2 * n:]


def _dot01_right(x, m01):
    m = x.shape[0]
    out = _dot(jnp.concatenate(_split3(x), axis=0), m01)
    return out[:m] + out[m:2 * m] + out[2 * m:]


def _seg_matrix(width, seg):
    r = lax.broadcasted_iota(jnp.int32, (width, width), 0) // seg
    c = lax.broadcasted_iota(jnp.int32, (width, width), 1) // seg
    return (r == c).astype(F32)


SUBLANES = 8


def _store_slabs(ref, x, lead=()):
    rows = x.shape[0]
    for s in range(SUBLANES):
        ref[lead + (pl.ds(s, rows, stride=SUBLANES), slice(None))] = x[:, s * LANE:(s + 1) * LANE]


def _load_slabs(ref, rows, lead=()):
    return jnp.concatenate(
        [ref[lead + (pl.ds(s, rows, stride=SUBLANES), slice(None))] for s in range(SUBLANES)], axis=1)


def _params(sem):
    return pltpu.CompilerParams(dimension_semantics=sem, vmem_limit_bytes=VMEM_LIMIT)


def _inproj_kernel(x_ref, g_ref, w_ref, o_ref, hn_ref):
    @pl.when(pl.program_id(1) == 0)
    def _():
        x = x_ref[...]
        ms = jnp.mean(x * x, axis=-1, keepdims=True)
        hn_ref[...] = (x * lax.rsqrt(ms + RMS_EPS) * g_ref[...]).astype(BF16)

    o_ref[...] = _dot(hn_ref[...], w_ref[...]).astype(o_ref.dtype)


def _inproj(x2, g, w, out_dtype, tn, tm=1024):
    t, d = x2.shape
    n = w.shape[1]
    return pl.pallas_call(
        _inproj_kernel,
        grid=(t // tm, n // tn),
        in_specs=[
            pl.BlockSpec((tm, d), lambda i, j: (i, 0)),
            pl.BlockSpec((1, d), lambda i, j: (0, 0)),
            pl.BlockSpec((d, tn), lambda i, j: (0, j)),
        ],
        out_specs=pl.BlockSpec((tm, tn), lambda i, j: (i, j)),
        out_shape=jax.ShapeDtypeStruct((t, n), out_dtype),
        scratch_shapes=[pltpu.VMEM((tm, d), BF16)],
        compiler_params=_params(("parallel", "arbitrary")),
        name="inproj",
    )(x2, g, w)


def _rwkv_chunk_kernel(p_ref, pprev_ref, mix_ref, w0_ref, wup_ref, a0_ref, aup_ref, gup_ref, kk_ref, ka_ref,
                       rk_ref, rm_ref, y0_ref, bonus_ref, g_ref, gam_ref, m_ref, d0_ref):
    c = RWKV_CHUNK
    n = RWKV_HEAD_DIM
    rows = p_ref.shape[1]

    p = p_ref[0]
    row = lax.broadcasted_iota(jnp.int32, p.shape, 0)
    last_prev = jnp.where(pl.program_id(1) == 0, 0.0, 1.0) * pprev_ref[0, 7:8, :]
    prev = jnp.where(row == 0, last_prev, pltpu.roll(p, 1, axis=0))
    ps = p + (prev - p) * mix_ref[...]
    r = ps[:, RW_R:RW_R + RWKV_WIDTH]
    k = ps[:, RW_K:RW_K + RWKV_WIDTH]
    v = ps[:, RW_V:RW_V + RWKV_WIDTH]
    xw = ps[:, RW_XW:RW_XW + LANE]
    xa = ps[:, RW_XA:RW_XA + LANE]
    xg = ps[:, RW_XG:RW_XG + LANE]

    z = -(w0_ref[...] + _dot(jnp.tanh(xw), wup_ref[...], HI))
    softplus = jnp.maximum(z, 0.0) + jnp.log1p(jnp.exp(-jnp.abs(z)))
    w = -softplus - 0.5
    logw = -jnp.exp(w)
    a = jax.nn.sigmoid(a0_ref[...] + _dot(xa, aup_ref[...], HI))
    g_ref[0] = _dot(jax.nn.sigmoid(xg), gup_ref[...], HI)

    seg = _seg_matrix(RWKV_WIDTH, n).astype(BF16)
    kk = k * kk_ref[...]
    k = k * (1.0 + (a - 1.0) * ka_ref[...])
    sums = _dot01_right(jnp.concatenate([kk * kk, r * k * rk_ref[...]], axis=0), seg)
    kk = kk / jnp.maximum(jnp.sqrt(sums[:rows]), 1e-12)
    bonus_ref[0] = sums[rows:] * v
    kka = kk * a

    ti = lax.broadcasted_iota(jnp.int32, (2 * c, 2 * c), 0)
    tj = lax.broadcasted_iota(jnp.int32, (2 * c, 2 * c), 1)
    keep = (tj % c) < jnp.where(ti < c, ti, ti - c + 1)
    ci = lax.broadcasted_iota(jnp.int32, (c, c), 0)
    cj = lax.broadcasted_iota(jnp.int32, (c, c), 1)
    eye = (ci == cj).astype(F32)
    ltri = (cj <= ci).astype(BF16)
    zeros_cn = jnp.zeros((c, n), BF16)

    chains = []
    for j in range(rows // c):
        rs = slice(j * c, (j + 1) * c)
        lw = logw[rs]
        cum = _dot01_left(ltri, lw)
        tot = cum[c - 1:c, :]
        einv = jnp.exp(-cum)
        dec_end = jnp.exp(tot - cum)
        r_f = r[rs] * jnp.exp(cum)
        a_t = (-kk[rs] * jnp.exp(cum - lw)).astype(BF16)
        b_t = (kka[rs] * einv).astype(BF16)
        k_t = (k[rs] * einv).astype(BF16)
        r_t = r_f.astype(BF16)
        b_e = (kka[rs] * dec_end).astype(BF16)
        k_e = (k[rs] * dec_end).astype(BF16)
        v_b = v[rs].astype(BF16)
        gam_ref[0, j] = jnp.exp(tot)
        for h in range(RWKV_HEADS):
            sl = slice(h * n, (h + 1) * n)
            chains.append(dict(j=j, h=h, rs=rs, sl=sl, a=a_t[:, sl], r=r_t[:, sl], rf=r_f[:, sl], v=v_b[:, sl],
                               rhs=jnp.concatenate([b_t[:, sl], k_t[:, sl]], axis=0),
                               bke=jnp.concatenate([b_e[:, sl], k_e[:, sl]], axis=0)))

    for ch in chains:
        lhs = jnp.concatenate([ch["a"], ch["r"]], axis=0)
        ch["amat"] = jnp.where(keep, _dot_nt(lhs, ch["rhs"]), 0.0)
        ch["pw"] = ch["amat"][:c, :c]
        ch["tinv"] = eye + ch["pw"]
    for _ in range(5):
        for ch in chains:
            pw_b = ch["pw"].astype(BF16)
            ch["pw"] = _dot(pw_b, pw_b)
        for ch in chains:
            ch["tinv"] = ch["tinv"] + _dot(ch["tinv"].astype(BF16), ch["pw"].astype(BF16))
    for ch in chains:
        ch["akv"] = _dot(ch["amat"][:c, c:].astype(BF16), ch["v"])
    for ch in chains:
        wu = _dot(ch["tinv"].astype(BF16), jnp.concatenate([ch["a"], ch["akv"].astype(BF16)], axis=1)).astype(BF16)
        ch["x"] = jnp.concatenate([wu, jnp.concatenate([zeros_cn, ch["v"]], axis=1)], axis=0)
    for ch in chains:
        ry = _dot(ch["amat"][c:, :].astype(BF16), ch["x"])
        rm_ref[0, ch["rs"], ch["sl"]] = (ch["rf"] + ry[:, :n]).astype(BF16)
        y0_ref[0, ch["rs"], ch["sl"]] = ry[:, n:]
    for ch in chains:
        md = _dot_tn(ch["x"], ch["bke"])
        m_ref[0, ch["j"], ch["h"]] = md[:n].astype(BF16)
        d0_ref[0, ch["j"], ch["h"]] = md[n:]


def _rwkv_chunks(proj3, mix, w0, wup, a0, aup, gup, k_k, k_a, r_k, rows=256):
    b, s, _ = proj3.shape
    c = RWKV_CHUNK
    nc = s // c
    cps = rows // c
    vec = lambda width: pl.BlockSpec((1, width), lambda i, t: (0, 0))
    mat = lambda nrows: pl.BlockSpec((nrows, RWKV_WIDTH), lambda i, t: (0, 0))
    tok = pl.BlockSpec((1, rows, RWKV_WIDTH), lambda i, t: (i, t, 0))
    sq = pl.BlockSpec((1, cps, RWKV_HEADS, RWKV_HEAD_DIM, RWKV_HEAD_DIM), lambda i, t: (i, t, 0, 0, 0))
    tok_shape = lambda dt: jax.ShapeDtypeStruct((b, s, RWKV_WIDTH), dt)
    sq_shape = lambda dt: jax.ShapeDtypeStruct((b, nc, RWKV_HEADS, RWKV_HEAD_DIM, RWKV_HEAD_DIM), dt)
    return pl.pallas_call(
        _rwkv_chunk_kernel,
        grid=(b, s // rows),
        in_specs=[
            pl.BlockSpec((1, rows, RWKV_PAD), lambda i, t: (i, t, COL_RWKV // RWKV_PAD)),
            pl.BlockSpec((1, 8, RWKV_PAD), lambda i, t: (i, jnp.maximum(t * (rows // 8) - 1, 0), COL_RWKV // RWKV_PAD)),
            vec(RWKV_PAD), vec(RWKV_WIDTH), mat(LANE), vec(RWKV_WIDTH), mat(LANE), mat(LANE),
            vec(RWKV_WIDTH), vec(RWKV_WIDTH), vec(RWKV_WIDTH),
        ],
        out_specs=[tok, tok, tok, tok,
                   pl.BlockSpec((1, cps, 1, RWKV_WIDTH), lambda i, t: (i, t, 0, 0)), sq, sq],
        out_shape=[tok_shape(BF16), tok_shape(F32), tok_shape(F32), tok_shape(F32),
                   jax.ShapeDtypeStruct((b, nc, 1, RWKV_WIDTH), F32), sq_shape(BF16), sq_shape(F32)],
        compiler_params=_params(("parallel", "parallel")),
        name="rwkv7_chunks",
    )(proj3, proj3, mix, w0, wup, a0, aup, gup, k_k, k_a, r_k)


def _rwkv_scan_kernel(rm_ref, y0_ref, bonus_ref, g_ref, gam_ref, m_ref, d0_ref, lnw_ref, lnb_ref,
                      o_ref, state_ref, y_ref):
    c = RWKV_CHUNK
    n = RWKV_HEAD_DIM

    @pl.when(pl.program_id(1) == 0)
    def _():
        state_ref[...] = jnp.zeros_like(state_ref)

    for j in range(gam_ref.shape[1]):
        rs = slice(j * c, (j + 1) * c)
        gam = gam_ref[0, j]
        for h in range(RWKV_HEADS):
            sl = slice(h * n, (h + 1) * n)
            s = state_ref[h]
            s_b = s.astype(BF16)
            y_ref[rs, sl] = _dot_nt(rm_ref[0, rs, sl], s_b) + y0_ref[0, rs, sl]
            state_ref[h] = s * gam[:, sl] + _dot(s_b, m_ref[0, j, h]) + d0_ref[0, j, h]

    seg = _seg_matrix(RWKV_WIDTH, n).astype(BF16)
    y = y_ref[...]
    mu = _dot01_right(y, seg) * (1.0 / n)
    d = y - mu
    var = _dot01_right(d * d, seg) * (1.0 / n)
    yn = d * lax.rsqrt(var + GN_EPS) * lnw_ref[...] + lnb_ref[...]
    o_ref[0] = (yn + bonus_ref[0]) * g_ref[0]


def _rwkv_scan(rm, y0, bonus, g, gam, m, d0, ln_w, ln_b, rows=512):
    b, s, _ = rm.shape
    cps = rows // RWKV_CHUNK
    vec = pl.BlockSpec((1, RWKV_WIDTH), lambda i, t: (0, 0))
    tok = pl.BlockSpec((1, rows, RWKV_WIDTH), lambda i, t: (i, t, 0))
    sq = pl.BlockSpec((1, cps, RWKV_HEADS, RWKV_HEAD_DIM, RWKV_HEAD_DIM), lambda i, t: (i, t, 0, 0, 0))
    return pl.pallas_call(
        _rwkv_scan_kernel,
        grid=(b, s // rows),
        in_specs=[tok, tok, tok, tok, pl.BlockSpec((1, cps, 1, RWKV_WIDTH), lambda i, t: (i, t, 0, 0)), sq, sq,
                  vec, vec],
        out_specs=tok,
        out_shape=jax.ShapeDtypeStruct((b, s, RWKV_WIDTH), F32),
        scratch_shapes=[
            pltpu.VMEM((RWKV_HEADS, RWKV_HEAD_DIM, RWKV_HEAD_DIM), F32),
            pltpu.VMEM((rows, RWKV_WIDTH), F32),
        ],
        compiler_params=_params(("parallel", "arbitrary")),
        name="rwkv7_scan",
    )(rm, y0, bonus, g, gam, m, d0, ln_w, ln_b)


def _rwkv(proj3, mix, w0, wup, a0, aup, gup, k_k, k_a, r_k, ln_w, ln_b):
    rm, y0, bonus, g, gam, m, d0 = _rwkv_chunks(proj3, mix, w0, wup, a0, aup, gup, k_k, k_a, r_k)
    return _rwkv_scan(rm, y0, bonus, g, gam, m, d0, ln_w, ln_b)


def _rope(x, cos, sin_signed):
    w = x.shape[-1]
    first_half = (lax.broadcasted_iota(jnp.int32, x.shape, 1) % HEAD_DIM) < (HEAD_DIM // 2)
    rot = jnp.where(first_half, pltpu.roll(x, w - HEAD_DIM // 2, axis=1), pltpu.roll(x, HEAD_DIM // 2, axis=1))
    return x * cos + rot * sin_signed


def _head_rmsnorm(x, gain, seg):
    ms = _dot01_right(x * x, seg) * (1.0 / HEAD_DIM)
    return x * lax.rsqrt(ms + RMS_EPS) * gain


def _split_groups(x):
    return [x[:, g * HEAD_DIM:(g + 1) * HEAD_DIM] for g in range(NSA_KV_HEADS)]


def _nsa_prep_kernel(q_ref, ksl_ref, vsl_ref, kwn_ref, vwn_ref, cos_ref, sin_ref, qn_ref, kn_ref,
                     qo_ref, ks_ref, vs_ref, kw_ref, vw_ref):
    seg_q = _seg_matrix(NSA_WIDTH, HEAD_DIM).astype(BF16)
    seg_k = _seg_matrix(NSA_KV_WIDTH, HEAD_DIM).astype(BF16)
    cos_k = cos_ref[:, :NSA_KV_WIDTH]
    sin_k = sin_ref[:, :NSA_KV_WIDTH]

    q = _rope(_head_rmsnorm(q_ref[0].astype(F32), qn_ref[...], seg_q), cos_ref[...], sin_ref[...])
    q_t = (q * (HEAD_DIM ** -0.5 * LOG2_E)).T
    for h in range(NSA_HEADS):
        qo_ref[0, h] = q_t[h * HEAD_DIM:(h + 1) * HEAD_DIM].astype(BF16)

    ks = _rope(_head_rmsnorm(ksl_ref[0].astype(F32), kn_ref[1:2, :], seg_k), cos_k, sin_k)
    kw = _rope(_head_rmsnorm(kwn_ref[0].astype(F32), kn_ref[2:3, :], seg_k), cos_k, sin_k)
    vs_t = vsl_ref[0].astype(F32).T
    vw_t = vwn_ref[0].astype(F32).T
    for g, (a, c_) in enumerate(zip(_split_groups(ks), _split_groups(kw))):
        ks_ref[0, g] = a.astype(BF16)
        kw_ref[0, g] = c_.astype(BF16)
        vs_ref[0, g] = vs_t[g * HEAD_DIM:(g + 1) * HEAD_DIM].astype(BF16)
        vw_ref[0, g] = vw_t[g * HEAD_DIM:(g + 1) * HEAD_DIM].astype(BF16)


def _nsa_prep(proj3, cos, sin_signed, q_norm, k_norm, tq=512):
    b, s, _ = proj3.shape
    kvw = NSA_KV_WIDTH
    kv_spec = lambda j: pl.BlockSpec((1, tq, kvw), lambda i, t: (i, t, COL_KV // kvw + j))
    out_k = pl.BlockSpec((1, NSA_KV_HEADS, tq, HEAD_DIM), lambda i, t: (i, 0, t, 0))
    out_vt = pl.BlockSpec((1, NSA_KV_HEADS, HEAD_DIM, tq), lambda i, t: (i, 0, 0, t))
    k_shape = jax.ShapeDtypeStruct((b, NSA_KV_HEADS, s, HEAD_DIM), BF16)
    vt_shape = jax.ShapeDtypeStruct((b, NSA_KV_HEADS, HEAD_DIM, s), BF16)
    return pl.pallas_call(
        _nsa_prep_kernel,
        grid=(b, s // tq),
        in_specs=[
            pl.BlockSpec((1, tq, NSA_WIDTH), lambda i, t: (i, t, COL_QNSA // NSA_WIDTH)),
            kv_spec(0), kv_spec(1), kv_spec(2), kv_spec(3),
            pl.BlockSpec((tq, NSA_WIDTH), lambda i, t: (t, 0)),
            pl.BlockSpec((tq, NSA_WIDTH), lambda i, t: (t, 0)),
            pl.BlockSpec((1, NSA_WIDTH), lambda i, t: (0, 0)),
            pl.BlockSpec((3, kvw), lambda i, t: (0, 0)),
        ],
        out_specs=[
            pl.BlockSpec((1, NSA_HEADS, HEAD_DIM, tq), lambda i, t: (i, 0, 0, t)),
            out_k, out_vt, out_k, out_vt,
        ],
        out_shape=[jax.ShapeDtypeStruct((b, NSA_HEADS, HEAD_DIM, s), BF16), k_shape, vt_shape, k_shape, vt_shape],
        compiler_params=_params(("parallel", "parallel")),
        name="nsa_prep",
    )(proj3, proj3, proj3, proj3, proj3, cos, sin_signed, q_norm, k_norm)


def _gelu_tanh(x):
    return 0.5 * x * (1.0 + jnp.tanh(0.7978845608028654 * (x + 0.044715 * x * x * x)))


def _nsa_cmp_kernel(kc_in_ref, vc_in_ref, cos_ref, sin_ref, kn_ref, pe_ref, w1_ref, b1_ref, w2_ref,
                    kc_ref, vc_ref):
    n_cmp = (kc_in_ref.shape[1] - CMP_LEN) // CMP_STRIDE + 1
    n_pad = n_cmp + 1
    zero_row = jnp.zeros((1, NSA_KV_WIDTH), F32)
    outs = []
    for j, src in enumerate((kc_in_ref, vc_in_ref)):
        acc = jnp.zeros((NSA_KV_HEADS * n_pad, CMP_HIDDEN), F32)
        for l in range(CMP_LEN):
            x = src[0, pl.ds(l, n_cmp, stride=CMP_STRIDE), :]
            x = jnp.concatenate([x, zero_row], axis=0)
            xg = jnp.concatenate(_split_groups(x), axis=0) + pe_ref[j, l:l + 1, :]
            acc = acc + _dot(xg.astype(BF16), w1_ref[j, l * HEAD_DIM:(l + 1) * HEAD_DIM, :])
        hid = _gelu_tanh(acc + b1_ref[j])
        out = _dot(hid.astype(BF16), w2_ref[j])
        outs.append(jnp.concatenate([out[g * n_pad:(g + 1) * n_pad] for g in range(NSA_KV_HEADS)], axis=1))
    kc, vc = outs
    seg_k = _seg_matrix(NSA_KV_WIDTH, HEAD_DIM).astype(BF16)
    kc = _rope(_head_rmsnorm(kc, kn_ref[0:1, :], seg_k), cos_ref[...], sin_ref[...])
    vc_t = vc.T
    for g, a in enumerate(_split_groups(kc)):
        kc_ref[0, g] = a.astype(BF16)
        vc_ref[0, g] = vc_t[g * HEAD_DIM:(g + 1) * HEAD_DIM].astype(BF16)


def _nsa_cmp(proj3, cos_c, sin_c, k_norm, pe, w1, b1, w2):
    b, s, _ = proj3.shape
    kvw = NSA_KV_WIDTH
    n_pad = (s - CMP_LEN) // CMP_STRIDE + 2
    full = lambda shape: pl.BlockSpec(shape, lambda i: (0,) * len(shape))
    k_spec = pl.BlockSpec((1, NSA_KV_HEADS, n_pad, HEAD_DIM), lambda i: (i, 0, 0, 0))
    vt_spec = pl.BlockSpec((1, NSA_KV_HEADS, HEAD_DIM, n_pad), lambda i: (i, 0, 0, 0))
    k_shape = jax.ShapeDtypeStruct((b, NSA_KV_HEADS, n_pad, HEAD_DIM), BF16)
    vt_shape = jax.ShapeDtypeStruct((b, NSA_KV_HEADS, HEAD_DIM, n_pad), BF16)
    return pl.pallas_call(
        _nsa_cmp_kernel,
        grid=(b,),
        in_specs=[
            pl.BlockSpec((1, s, kvw), lambda i: (i, 0, COL_CMP // kvw)),
            pl.BlockSpec((1, s, kvw), lambda i: (i, 0, COL_CMP // kvw + 1)),
            full((n_pad, kvw)), full((n_pad, kvw)), full((3, kvw)),
            full((2, CMP_LEN, HEAD_DIM)), full((2, CMP_LEN * HEAD_DIM, CMP_HIDDEN)),
            full((2, 1, CMP_HIDDEN)), full((2, CMP_HIDDEN, HEAD_DIM)),
        ],
        out_specs=[k_spec, vt_spec],
        out_shape=[k_shape, vt_shape],
        compiler_params=_params(("parallel",)),
        name="nsa_compress",
    )(proj3, proj3, cos_c, sin_c, k_norm, pe, w1, b1, w2)


def _nsa_attn_kernel(q_ref, gate_ref, kc_ref, vc_ref, ks_ref, vs_ref, kw_ref, vw_ref, o_ref):
    qb = Q_BLOCK
    hg = NSA_GROUP
    g = pl.program_id(1)
    c = pl.program_id(2)
    cols = hg * qb
    qt = jnp.concatenate([q_ref[0, h] for h in range(hg)], axis=1)
    tile_heads = lambda x: jnp.concatenate([x] * hg, axis=1)

    n_cp = kc_ref.shape[2]
    s_c = _dot(kc_ref[0, 0], qt)
    cend = lax.broadcasted_iota(jnp.int32, (n_cp, qb), 0) * CMP_STRIDE + CMP_LEN - 1
    cmask = tile_heads(jnp.where(cend <= c * qb + lax.broadcasted_iota(jnp.int32, (n_cp, qb), 1), 1, 0)) > 0
    s_m = jnp.where(cmask, s_c, -jnp.inf)
    m = jnp.max(s_m, axis=0, keepdims=True)
    m = jnp.where(m == -jnp.inf, 0.0, m)
    e = jnp.where(cmask, jnp.exp2(s_m - m), 0.0)
    p_c = e / jnp.maximum(jnp.sum(e, axis=0, keepdims=True), 1e-30)
    o_cmp = _dot(vc_ref[0, 0], p_c.astype(BF16))

    n_blk = ks_ref.shape[2] // SEL_BLOCK
    p_sum = p_c[:, 0:qb]
    for h in range(1, hg):
        p_sum = p_sum + p_c[:, h * qb:(h + 1) * qb]
    bi = lax.broadcasted_iota(jnp.int32, (n_blk, n_cp), 0) * SEL_BLOCK
    ci = lax.broadcasted_iota(jnp.int32, (n_blk, n_cp), 1) * CMP_STRIDE
    cover_t = ((ci < bi + SEL_BLOCK) & (ci + CMP_LEN > bi)).astype(F32)
    imp = _dot(cover_t, p_sum, HI)
    blk = lax.broadcasted_iota(jnp.int32, (n_blk, qb), 0)
    qpos = c * qb + lax.broadcasted_iota(jnp.int32, (n_blk, qb), 1)
    cur = qpos // SEL_BLOCK
    forced = (blk == 0) | (blk == cur) | (blk == cur - 1)
    imp = jnp.where(forced, imp + FORCE_BONUS, imp)
    imp = jnp.where(blk <= cur, imp, -jnp.inf)
    rank = jnp.zeros((n_blk, qb), F32)
    for j in range(n_blk):
        other = imp[j:j + 1, :]
        rank = rank + jnp.where(blk > j, jnp.where(other >= imp, 1.0, 0.0), jnp.where(other > imp, 1.0, 0.0))
    sel_t = jnp.where(rank < min(TOP_N, n_blk), 1.0, 0.0).astype(BF16)

    kt_w = 4 * LANE
    kq = c * qb + lax.broadcasted_iota(jnp.int32, (kt_w, qb), 1)
    krel = lax.broadcasted_iota(jnp.int32, (kt_w, qb), 0)
    er = lax.broadcasted_iota(jnp.int32, (kt_w, n_blk), 0) // SEL_BLOCK
    ec = lax.broadcasted_iota(jnp.int32, (kt_w, n_blk), 1)

    def sel_body(kt, carry):
        m_i, l_i, acc = carry
        off = pl.multiple_of(kt * kt_w, kt_w)
        expand = jnp.where(ec == er + kt * (kt_w // SEL_BLOCK), 1.0, 0.0).astype(BF16)
        in_sel = _dot(expand, sel_t)
        bias = jnp.where((in_sel > 0.5) & (krel + off <= kq), 0.0, NEG_BIG)
        s = _dot(ks_ref[0, 0, pl.ds(off, kt_w), :], qt) + tile_heads(bias)
        m_new = jnp.maximum(m_i, jnp.max(s, axis=0, keepdims=True))
        alpha = jnp.exp2(m_i - m_new)
        p = jnp.exp2(s - m_new)
        l_new = alpha * l_i + jnp.sum(p, axis=0, keepdims=True)
        acc = alpha * acc + _dot(vs_ref[0, 0, :, pl.ds(off, kt_w)], p.astype(BF16))
        return m_new, l_new, acc

    init = (jnp.full((1, cols), NEG_BIG, F32), jnp.zeros((1, cols), F32), jnp.zeros((HEAD_DIM, cols), F32))
    _, l_s, acc_s = lax.fori_loop(0, (c * qb + qb + kt_w - 1) // kt_w, sel_body, init)
    o_slc = acc_s / l_s

    span = WINDOW + 2 * qb
    w0 = pl.multiple_of(jnp.clip(c * qb - WINDOW, 0, kw_ref.shape[2] - span), qb)
    kpos = w0 + lax.broadcasted_iota(jnp.int32, (span, qb), 0)
    wq = c * qb + lax.broadcasted_iota(jnp.int32, (span, qb), 1)
    wbias = jnp.where((kpos <= wq) & (kpos > wq - WINDOW), 0.0, NEG_BIG)
    s_w = _dot(kw_ref[0, 0, pl.ds(w0, span), :], qt) + tile_heads(wbias)
    e_w = jnp.exp2(s_w - jnp.max(s_w, axis=0, keepdims=True))
    o_win = _dot(vw_ref[0, 0, :, pl.ds(w0, span)], e_w.astype(BF16)) / jnp.sum(e_w, axis=0, keepdims=True)

    gates_t = jax.nn.sigmoid(gate_ref[0].astype(F32)).T
    grow = lax.broadcasted_iota(jnp.int32, gates_t.shape, 0)
    outs = []
    for h in range(hg):
        hc = slice(h * qb, (h + 1) * qb)
        first = (g * hg + h) * 3
        pick = lambda j: jnp.sum(jnp.where(grow == first + j, gates_t, 0.0), axis=0, keepdims=True)
        outs.append(pick(0) * o_cmp[:, hc] + pick(1) * o_slc[:, hc] + pick(2) * o_win[:, hc])
    o_ref[0] = jnp.concatenate(outs, axis=0).T


def _nsa_attn(q, proj3, kc, vc, ks, vs, kw, vw):
    b, _, _, s = q.shape
    gw = NSA_GROUP * HEAD_DIM
    n_pad = kc.shape[2]
    per_group = lambda rows, width: pl.BlockSpec((1, 1, rows, width), lambda i, g, c: (i, g, 0, 0))
    return pl.pallas_call(
        _nsa_attn_kernel,
        grid=(b, NSA_KV_HEADS, s // Q_BLOCK),
        in_specs=[
            pl.BlockSpec((1, NSA_GROUP, HEAD_DIM, Q_BLOCK), lambda i, g, c: (i, g, 0, c)),
            pl.BlockSpec((1, Q_BLOCK, LANE), lambda i, g, c: (i, c, COL_GATE // LANE)),
            per_group(n_pad, HEAD_DIM), per_group(HEAD_DIM, n_pad),
            per_group(s, HEAD_DIM), per_group(HEAD_DIM, s), per_group(s, HEAD_DIM), per_group(HEAD_DIM, s),
        ],
        out_specs=pl.BlockSpec((1, Q_BLOCK, gw), lambda i, g, c: (i, c, g)),
        out_shape=jax.ShapeDtypeStruct((b, s, NSA_WIDTH), F32),
        compiler_params=_params(("parallel", "parallel", "arbitrary")),
        name="nsa_attn",
    )(q, proj3, kc, vc, ks, vs, kw, vw)


def _mem_kv_kernel(mem_ref, g_ref, w_ref, kn_ref, k_ref, v_ref):
    x = mem_ref[0]
    ms = jnp.mean(x * x, axis=-1, keepdims=True)
    xn = (x * lax.rsqrt(ms + RMS_EPS) * g_ref[...]).astype(BF16)
    kv = _dot(xn, w_ref[...])
    for h in range(MEM_HEADS):
        sl = slice(h * MEM_HEAD_DIM, (h + 1) * MEM_HEAD_DIM)
        kh = kv[:, sl]
        kms = jnp.mean(kh * kh, axis=-1, keepdims=True)
        k_ref[0, :, sl] = (kh * lax.rsqrt(kms + RMS_EPS) * kn_ref[...]).astype(BF16)
    v_ref[0] = kv[:, MEM_WIDTH:].astype(BF16)


def _mem_kv(mem, g_mem, w_kv, k_norm):
    b, m, d = mem.shape
    spec = pl.BlockSpec((1, m, MEM_WIDTH), lambda i: (i, 0, 0))
    shape = jax.ShapeDtypeStruct((b, m, MEM_WIDTH), BF16)
    return pl.pallas_call(
        _mem_kv_kernel,
        grid=(b,),
        in_specs=[
            pl.BlockSpec((1, m, d), lambda i: (i, 0, 0)),
            pl.BlockSpec((1, d), lambda i: (0, 0)),
            pl.BlockSpec((d, 2 * MEM_WIDTH), lambda i: (0, 0)),
            pl.BlockSpec((1, MEM_HEAD_DIM), lambda i: (0, 0)),
        ],
        out_specs=[spec, spec],
        out_shape=[shape, shape],
        compiler_params=_params(("parallel",)),
        name="mem_kv",
    )(mem, g_mem, w_kv, k_norm)


def _mem_attn_kernel(q_ref, qn_ref, k_ref, v_ref, o_ref):
    q = q_ref[0].astype(F32)
    for h in range(MEM_HEADS):
        sl = slice(h * MEM_HEAD_DIM, (h + 1) * MEM_HEAD_DIM)
        qh = q[:, sl]
        ms = jnp.mean(qh * qh, axis=-1, keepdims=True)
        qh = (qh * lax.rsqrt(ms + RMS_EPS) * qn_ref[...]).astype(BF16)
        s = _dot_nt(qh, k_ref[0, :, sl]) * (MEM_HEAD_DIM ** -0.5)
        m = jnp.max(s, axis=-1, keepdims=True)
        e = jnp.exp(s - m)
        p = e / jnp.sum(e, axis=-1, keepdims=True)
        o_ref[0, :, sl] = _dot(p.astype(BF16), v_ref[0, :, sl])


def _mem_attn(proj3, q_norm, k, v, tq=512):
    b, s, _ = proj3.shape
    m = k.shape[1]
    kv_spec = pl.BlockSpec((1, m, MEM_WIDTH), lambda i, t: (i, 0, 0))
    return pl.pallas_call(
        _mem_attn_kernel,
        grid=(b, s // tq),
        in_specs=[
            pl.BlockSpec((1, tq, MEM_WIDTH), lambda i, t: (i, t, COL_QMEM // MEM_WIDTH)),
            pl.BlockSpec((1, MEM_HEAD_DIM), lambda i, t: (0, 0)),
            kv_spec, kv_spec,
        ],
        out_specs=pl.BlockSpec((1, tq, MEM_WIDTH), lambda i, t: (i, t, 0)),
        out_shape=jax.ShapeDtypeStruct((b, s, MEM_WIDTH), F32),
        compiler_params=_params(("parallel", "parallel")),
        name="mem_attn",
    )(proj3, q_norm, k, v)


def _merge_kernel(oa_ref, ob_ref, om_ref, l0_ref, l1_ref, l2_ref, x_ref, bm_ref, wb_ref, wo_ref,
                  gf_ref, rw_ref, rb_ref, x1_ref, h_ref, route_ref):
    mixed = None
    for n, (o_ref, l_ref) in enumerate(((oa_ref, l0_ref), (ob_ref, l1_ref), (om_ref, l2_ref))):
        gate = jax.nn.sigmoid(l_ref[...].astype(F32) + bm_ref[:, n * D_MODEL:(n + 1) * D_MODEL])
        term = gate * _dot(o_ref[...].astype(BF16), wb_ref[n])
        mixed = term if mixed is None else mixed + term
    x1 = x_ref[...] + _dot(mixed.astype(BF16), wo_ref[...])
    x1_ref[...] = x1
    ms = jnp.mean(x1 * x1, axis=-1, keepdims=True)
    h = x1 * lax.rsqrt(ms + RMS_EPS) * gf_ref[...]
    _store_slabs(h_ref, h)

    logits = _dot(h, rw_ref[...], HI) + rb_ref[...]
    lane_f = lax.broadcasted_iota(jnp.int32, logits.shape, 1).astype(F32)
    work = logits
    picks = []
    for _ in range(TOP_K):
        mx = jnp.max(work, axis=-1, keepdims=True)
        idx = jnp.min(jnp.where(work == mx, lane_f, 2.0 * LANE), axis=-1, keepdims=True)
        picks.append((idx, mx))
        work = jnp.where(lane_f == idx, -jnp.inf, work)
    exps = [jnp.exp(mx - picks[0][1]) for _, mx in picks]
    denom = functools.reduce(lambda a, b: a + b, exps)
    route = jnp.zeros(logits.shape, F32)
    for kk, ((idx, _), ex) in enumerate(zip(picks, exps)):
        route = jnp.where(lane_f == kk, idx, route)
        route = jnp.where(lane_f == TOP_K + kk, ex / denom, route)
    route_ref[...] = route


def _merge(o_a, o_b, o_m, proj2, x2, b_merge, w_branch, w_out, g_ffn, router_w, router_b, tm=512):
    t, d = x2.shape
    row = lambda width: pl.BlockSpec((tm, width), lambda i: (i, 0))
    logit = lambda n: pl.BlockSpec((tm, d), lambda i: (i, COL_MERGE // d + n))
    full = lambda shape: pl.BlockSpec(shape, lambda i: (0,) * len(shape))
    return pl.pallas_call(
        _merge_kernel,
        grid=(t // tm,),
        in_specs=[
            row(BRANCH_WIDTH), row(BRANCH_WIDTH), row(BRANCH_WIDTH), logit(0), logit(1), logit(2), row(d),
            full((1, N_BRANCHES * d)), full((N_BRANCHES, BRANCH_WIDTH, d)), full((d, d)), full((1, d)),
            full((d, LANE)), full((1, LANE)),
        ],
        out_specs=[row(d), pl.BlockSpec((tm * SUBLANES, LANE), lambda i: (i, 0)), row(LANE)],
        out_shape=[jax.ShapeDtypeStruct((t, d), F32), jax.ShapeDtypeStruct((t * SUBLANES, LANE), F32),
                   jax.ShapeDtypeStruct((t, LANE), F32)],
        compiler_params=_params(("parallel",)),
        name="merge_router",
    )(o_a, o_b, o_m, proj2, proj2, proj2, x2, b_merge, w_branch, w_out, g_ffn, router_w, router_b)


def _deinterleave(x):
    rows, width = x.shape
    lane = lax.broadcasted_iota(jnp.int32, (rows, LANE), 1)
    half = LANE // 2
    low = lane < half
    idx = jnp.where(low, 2 * lane, 2 * (lane - half) + 1)
    evens, odds = [], []
    for j in range(0, width, 2 * LANE):
        a = jnp.take_along_axis(x[:, j:j + LANE], idx, axis=1)
        b = jnp.take_along_axis(x[:, j + LANE:j + 2 * LANE], idx, axis=1)
        evens.append(jnp.where(low, a, pltpu.roll(b, half, axis=1)))
        odds.append(jnp.where(low, pltpu.roll(a, half, axis=1), b))
    return jnp.concatenate(evens, axis=1), jnp.concatenate(odds, axis=1)


MOE_TILE = 512


def _route_kernel(route_ref, dest_ref, counts_ref, carry_ref, start_ref):
    phase = pl.program_id(0)
    i = pl.program_id(1)
    tr = route_ref.shape[0]
    route = route_ref[...]
    lane = lax.broadcasted_iota(jnp.int32, route.shape, 1)
    lane_f = lane.astype(F32)
    hits = [lane_f == route[:, kk:kk + 1] for kk in range(TOP_K)]
    sel = jnp.zeros(route.shape, F32)
    for hit in hits:
        sel = jnp.where(hit, 1.0, sel)

    @pl.when((phase == 0) & (i == 0))
    def _():
        carry_ref[...] = jnp.zeros_like(carry_ref)

    @pl.when((phase == 1) & (i == 0))
    def _():
        counts = carry_ref[...]
        counts_ref[...] = counts
        padded = jnp.ceil(counts * (1.0 / MOE_TILE)) * MOE_TILE
        ui = lax.broadcasted_iota(jnp.int32, (LANE, LANE), 0)
        uj = lax.broadcasted_iota(jnp.int32, (LANE, LANE), 1)
        start_ref[...] = _dot01_right(jnp.broadcast_to(padded, (8, LANE)), (ui < uj).astype(BF16))[0:1]
        carry_ref[...] = jnp.zeros_like(carry_ref)

    @pl.when(phase == 1)
    def _():
        ri = lax.broadcasted_iota(jnp.int32, (tr, tr), 0)
        rj = lax.broadcasted_iota(jnp.int32, (tr, tr), 1)
        rank = carry_ref[...] + _dot((rj < ri).astype(BF16), sel.astype(BF16))
        row = start_ref[...] + rank
        dest = jnp.zeros(route.shape, F32)
        for kk, hit in enumerate(hits):
            dest = jnp.where(lane == kk, jnp.sum(jnp.where(hit, row, 0.0), axis=-1, keepdims=True), dest)
        dest_ref[...] = dest.astype(jnp.int32)

    carry_ref[...] += jnp.sum(sel, axis=0, keepdims=True)


def _route(route, tr=512):
    t = route.shape[0]
    return pl.pallas_call(
        _route_kernel,
        grid=(2, t // tr),
        in_specs=[pl.BlockSpec((tr, LANE), lambda p, i: (i, 0))],
        out_specs=[pl.BlockSpec((tr, LANE), lambda p, i: (i * p, 0)), pl.BlockSpec((1, LANE), lambda p, i: (0, 0))],
        out_shape=[jax.ShapeDtypeStruct((t, LANE), jnp.int32), jax.ShapeDtypeStruct((1, LANE), F32)],
        scratch_shapes=[pltpu.VMEM((1, LANE), F32), pltpu.VMEM((1, LANE), F32)],
        compiler_params=_params(("arbitrary", "arbitrary")),
        name="moe_route",
    )(route)


def _row_copy(src, src_row, dst, dst_row, sem):
    src_at = pl.ds(pl.multiple_of(src_row * SUBLANES, SUBLANES), SUBLANES)
    dst_at = pl.ds(pl.multiple_of(dst_row * SUBLANES, SUBLANES), SUBLANES)
    return pltpu.make_async_copy(src.at[src_at], dst.at[dst_at], sem)


def _zero_fill(pad_start_ref, pad_len_ref, nu_ref, xs_ref, zero_ref, zsem, wait):
    n_e = pad_start_ref.shape[0]
    n_tiles = xs_ref.shape[0] // (MOE_TILE * SUBLANES)

    def zero_rows(first_row, n_rows):
        src = zero_ref.at[pl.ds(0, n_rows * SUBLANES)]
        dst = xs_ref.at[pl.ds(pl.multiple_of(first_row * SUBLANES, SUBLANES), n_rows * SUBLANES)]
        copy = pltpu.make_async_copy(src, dst, zsem)
        copy.wait() if wait else copy.start()

    for e in range(n_e):
        row = pad_start_ref[e]
        for bit in [1 << b for b in reversed(range(MOE_TILE.bit_length() - 1))]:
            has = (pad_len_ref[e] & bit) != 0
            pl.when(has)(functools.partial(zero_rows, row, bit))
            row = row + jnp.where(has, bit, 0)
    for k in range(n_e):
        tile = n_tiles - n_e + k
        pl.when(tile >= nu_ref[0])(functools.partial(zero_rows, tile * MOE_TILE, MOE_TILE))


def _dispatch_kernel(pad_start_ref, pad_len_ref, nu_ref, dest_ref, h_ref, xs_ref, zero_ref, sem, zsem):
    td = h_ref.shape[0] // SUBLANES

    @pl.when(pl.program_id(0) == 0)
    def _():
        zero_ref[...] = jnp.zeros_like(zero_ref)
        _zero_fill(pad_start_ref, pad_len_ref, nu_ref, xs_ref, zero_ref, zsem, wait=False)

    def issue(r, carry):
        for kk in range(TOP_K):
            _row_copy(h_ref, r, xs_ref, dest_ref[r * TOP_K + kk], sem).start(priority=kk % 2)
        return carry

    lax.fori_loop(0, td, issue, 0, unroll=8)

    def drain(r, carry):
        for kk in range(TOP_K):
            _row_copy(h_ref, 0, xs_ref, 0, sem).wait()
        return carry

    lax.fori_loop(0, td, drain, 0, unroll=8)

    @pl.when(pl.program_id(0) == 0)
    def _():
        _zero_fill(pad_start_ref, pad_len_ref, nu_ref, xs_ref, zero_ref, zsem, wait=True)


def _dispatch(pad_start, pad_len, n_used, dest_flat, h, n_rows, td=512):
    t = h.shape[0] // SUBLANES
    return pl.pallas_call(
        _dispatch_kernel,
        grid_spec=pltpu.PrefetchScalarGridSpec(
            num_scalar_prefetch=3,
            grid=(t // td,),
            in_specs=[
                pl.BlockSpec((td * TOP_K,), lambda i, ps, pn, nu: (i,), memory_space=pltpu.SMEM),
                pl.BlockSpec((td * SUBLANES, LANE), lambda i, ps, pn, nu: (i, 0)),
            ],
            out_specs=pl.BlockSpec(memory_space=pl.ANY),
            scratch_shapes=[pltpu.VMEM((MOE_TILE * SUBLANES, LANE), F32), pltpu.SemaphoreType.DMA,
                            pltpu.SemaphoreType.DMA],
        ),
        out_shape=jax.ShapeDtypeStruct((n_rows * SUBLANES, LANE), F32),
        compiler_params=_params(("arbitrary",)),
        name="moe_dispatch",
    )(pad_start, pad_len, n_used, dest_flat, h)


def _experts_kernel(te_ref, nu_ref, xs_ref, w1_ref, b1g_ref, b1l_ref, w2_ref, b2_ref, y_ref, w1b_ref, w2b_ref):
    j = pl.program_id(0)

    @pl.when(j < nu_ref[0])
    def _():
        @pl.when((j == 0) | (te_ref[j] != te_ref[jnp.maximum(j - 1, 0)]))
        def _():
            w1b_ref[...] = w1_ref[0].astype(BF16)
            w2b_ref[...] = w2_ref[0].astype(BF16)

        x = _load_slabs(xs_ref, MOE_TILE).astype(BF16)
        hid_g, hid_l = _deinterleave(_dot(x, w1b_ref[...]))
        x_glu = jnp.minimum(hid_g + b1g_ref[0], SWIGLU_LIMIT)
        x_lin = jnp.clip(hid_l + b1l_ref[0], -SWIGLU_LIMIT, SWIGLU_LIMIT)
        act = x_glu * jax.nn.sigmoid(SWIGLU_ALPHA * x_glu) * (x_lin + 1.0)
        _store_slabs(y_ref, _dot(act.astype(BF16), w2b_ref[...]) + b2_ref[0])

    @pl.when(j >= nu_ref[0])
    def _():
        y_ref[...] = jnp.zeros_like(y_ref)


def _experts(tile_expert, n_used, xs, w1, b1g, b1l, w2, b2):
    rows = xs.shape[0] // SUBLANES
    n_e, ff, d = w2.shape
    tg = MOE_TILE
    used = lambda j, te, nu: jnp.minimum(j, nu[0] - 1)
    exp_of = lambda j, te, nu: te[used(j, te, nu)]
    return pl.pallas_call(
        _experts_kernel,
        grid_spec=pltpu.PrefetchScalarGridSpec(
            num_scalar_prefetch=2,
            grid=(rows // tg,),
            in_specs=[
                pl.BlockSpec((tg * SUBLANES, LANE), lambda j, te, nu: (used(j, te, nu), 0)),
                pl.BlockSpec((1, d, 2 * ff), lambda j, te, nu: (exp_of(j, te, nu), 0, 0)),
                pl.BlockSpec((1, 1, ff), lambda j, te, nu: (exp_of(j, te, nu), 0, 0)),
                pl.BlockSpec((1, 1, ff), lambda j, te, nu: (exp_of(j, te, nu), 0, 0)),
                pl.BlockSpec((1, ff, d), lambda j, te, nu: (exp_of(j, te, nu), 0, 0)),
                pl.BlockSpec((1, 1, d), lambda j, te, nu: (exp_of(j, te, nu), 0, 0)),
            ],
            out_specs=pl.BlockSpec((tg * SUBLANES, LANE), lambda j, te, nu: (j, 0)),
            scratch_shapes=[pltpu.VMEM((d, 2 * ff), BF16), pltpu.VMEM((ff, d), BF16)],
        ),
        out_shape=jax.ShapeDtypeStruct((rows * SUBLANES, LANE), F32),
        compiler_params=_params(("arbitrary",)),
        name="moe_experts",
    )(tile_expert, n_used, xs, w1, b1g, b1l, w2, b2)


def _combine_kernel(dest_ref, dest_next_ref, route_ref, x1_ref, y_ref, o_ref, buf_ref, sems):
    i = pl.program_id(0)
    tc = x1_ref.shape[0]
    slot = i % 2

    def gather(dref, to_slot):
        def issue(r, carry):
            for kk in range(TOP_K):
                _row_copy(y_ref, dref[r * TOP_K + kk], buf_ref.at[to_slot, kk], r, sems.at[to_slot]).start(
                    priority=kk % 2)
            return carry

        lax.fori_loop(0, tc, issue, 0, unroll=8)

    @pl.when(i == 0)
    def _():
        gather(dest_ref, slot)

    @pl.when(i + 1 < pl.num_programs(0))
    def _():
        gather(dest_next_ref, 1 - slot)

    def drain(r, carry):
        for kk in range(TOP_K):
            _row_copy(y_ref, 0, buf_ref.at[slot, kk], 0, sems.at[slot]).wait()
        return carry

    lax.fori_loop(0, tc, drain, 0, unroll=8)

    route = route_ref[...]
    acc = x1_ref[...]
    for kk in range(TOP_K):
        acc = acc + route[:, TOP_K + kk:TOP_K + kk + 1] * _load_slabs(buf_ref, tc, lead=(slot, kk))
    o_ref[...] = acc


def _combine(dest_flat, route, x1, y, tc=256):
    t, d = x1.shape
    last = t // tc - 1
    return pl.pallas_call(
        _combine_kernel,
        grid=(t // tc,),
        in_specs=[
            pl.BlockSpec((tc * TOP_K,), lambda i: (i,), memory_space=pltpu.SMEM),
            pl.BlockSpec((tc * TOP_K,), lambda i: (jnp.minimum(i + 1, last),), memory_space=pltpu.SMEM),
            pl.BlockSpec((tc, LANE), lambda i: (i, 0)),
            pl.BlockSpec((tc, d), lambda i: (i, 0)),
            pl.BlockSpec(memory_space=pl.ANY),
        ],
        out_specs=pl.BlockSpec((tc, d), lambda i: (i, 0)),
        out_shape=jax.ShapeDtypeStruct((t, d), F32),
        scratch_shapes=[pltpu.VMEM((2, TOP_K, tc * SUBLANES, LANE), F32), pltpu.SemaphoreType.DMA((2,))],
        compiler_params=_params(("arbitrary",)),
        name="moe_combine",
    )(dest_flat, dest_flat, route, x1, y)


def _moe(h, route, x1, w1, b1g, b1l, w2, b2):
    t, d = x1.shape
    assert d == SUBLANES * LANE
    n_e = w2.shape[0]
    n_tiles = (t * TOP_K) // MOE_TILE + n_e
    dest, counts = _route(route)
    tiles_per = jnp.ceil(counts[0, :n_e] * (1.0 / MOE_TILE)).astype(jnp.int32)
    tile_end = jnp.cumsum(tiles_per)
    past = (tile_end[None, :] <= jnp.arange(n_tiles, dtype=jnp.int32)[:, None]).astype(jnp.int32)
    tile_expert = jnp.minimum(jnp.sum(past, axis=1), n_e - 1).astype(jnp.int32)
    n_used = tile_end[-1:].astype(jnp.int32)
    dest_flat = dest[:, :TOP_K].reshape(-1)
    n_tok = counts[0, :n_e].astype(jnp.int32)
    pad_start = (tile_end - tiles_per) * MOE_TILE + n_tok
    pad_len = tiles_per * MOE_TILE - n_tok
    xs = _dispatch(pad_start, pad_len, n_used, dest_flat, h, n_tiles * MOE_TILE)
    y = _experts(tile_expert, n_used, xs, w1, b1g, b1l, w2, b2)
    return _combine(dest_flat, route, x1, y)


def _pad_cols(w, width):
    return jnp.pad(w, ((0, 0), (0, width - w.shape[1])))


def _rope_tables(pos, reps):
    inv = ROPE_THETA ** (-jnp.arange(0, HEAD_DIM, 2, dtype=F32) / HEAD_DIM)
    ang = pos.astype(F32)[:, None] * inv[None, :]
    ang = jnp.concatenate([ang, ang], axis=-1)
    sign = jnp.concatenate([-jnp.ones((HEAD_DIM // 2,), F32), jnp.ones((HEAD_DIM // 2,), F32)])
    return jnp.tile(jnp.cos(ang), (1, reps)), jnp.tile(jnp.sin(ang) * sign, (1, reps))


def _layer(x, mem, g_mix, g_mem, w_in, b_merge, nsa_q_norm, nsa_k_norm, cmp_pe, cmp_w1, cmp_b1, cmp_w2,
           rwkv_shift_mix, rwkv_w0, rwkv_w_up, rwkv_a0, rwkv_a_up, rwkv_g_up, rwkv_k_k, rwkv_k_a, rwkv_r_k,
           rwkv_ln_w, rwkv_ln_b, mem_w_kv, mem_q_norm, mem_k_norm, w_branch, w_out, g_ffn,
           router_w, router_b, exp_w1, exp_b1, exp_w2, exp_b2, cos, sin, cos_c, sin_c):
    b, s, d = x.shape
    t = b * s
    x2 = x.reshape(t, d)

    o = 0
    parts = []
    for width in (NSA_WIDTH, 6 * NSA_KV_WIDTH, 3 * NSA_HEADS,
                  3 * RWKV_WIDTH + DECAY_RANK + AAA_RANK + GATE_RANK, MEM_WIDTH, N_BRANCHES * D_MODEL):
        parts.append(w_in[:, o:o + width])
        o += width
    w_q, w_kv, w_gate, w_rwkv, w_qm, w_merge = parts

    def rwkv_layout(m):
        r3 = m[:, :3 * RWKV_WIDTH]
        xw = m[:, 3 * RWKV_WIDTH:3 * RWKV_WIDTH + DECAY_RANK]
        xa = m[:, 3 * RWKV_WIDTH + DECAY_RANK:3 * RWKV_WIDTH + DECAY_RANK + AAA_RANK]
        xg = m[:, 3 * RWKV_WIDTH + DECAY_RANK + AAA_RANK:]
        return _pad_cols(jnp.concatenate([r3, _pad_cols(xw, LANE), _pad_cols(xa, LANE), xg], axis=1), RWKV_PAD)

    w_f32 = jnp.concatenate([rwkv_layout(w_rwkv), w_kv[:, :2 * NSA_KV_WIDTH]], axis=1).astype(BF16)
    w_b16 = jnp.concatenate([w_merge, w_q, w_qm, w_kv[:, 2 * NSA_KV_WIDTH:], w_gate], axis=1)
    w_b16 = _pad_cols(w_b16, PROJ_BF16_WIDTH).astype(BF16)
    g_row = g_mix.reshape(1, d)
    proj3 = _inproj(x2, g_row, w_f32, F32, PROJ_F32_WIDTH).reshape(b, s, PROJ_F32_WIDTH)
    proj2 = _inproj(x2, g_row, w_b16, BF16, PROJ_BF16_WIDTH // 2)
    proj3b = proj2.reshape(b, s, PROJ_BF16_WIDTH)

    rowv = lambda a: a.reshape(1, -1)
    pad_rows = lambda m: jnp.pad(m, ((0, LANE - m.shape[0]), (0, 0)))
    o_b = _rwkv(proj3, rwkv_layout(rowv(rwkv_shift_mix)), rowv(rwkv_w0), pad_rows(rwkv_w_up), rowv(rwkv_a0),
                pad_rows(rwkv_a_up), rwkv_g_up, rowv(rwkv_k_k), rowv(rwkv_k_a), rowv(rwkv_r_k),
                rowv(rwkv_ln_w), rowv(rwkv_ln_b))

    q_gain = jnp.tile(nsa_q_norm.reshape(1, HEAD_DIM), (1, NSA_HEADS))
    k_gain = jnp.tile(nsa_k_norm, (1, NSA_KV_HEADS))
    qn, ks, vs, kw, vw = _nsa_prep(proj3b, cos, sin, q_gain, k_gain)
    kc, vc = _nsa_cmp(proj3, cos_c, sin_c, k_gain, cmp_pe, cmp_w1.astype(BF16),
                      cmp_b1.reshape(2, 1, CMP_HIDDEN), cmp_w2.astype(BF16))
    o_a = _nsa_attn(qn, proj3b, kc, vc, ks, vs, kw, vw)

    mk, mv = _mem_kv(mem, g_mem.reshape(1, d), mem_w_kv.astype(BF16), mem_k_norm.reshape(1, MEM_HEAD_DIM))
    o_m = _mem_attn(proj3b, mem_q_norm.reshape(1, MEM_HEAD_DIM), mk, mv)

    rw = _pad_cols(router_w, LANE)
    rb = jnp.concatenate([router_b, jnp.full((LANE - N_EXPERTS,), NEG_BIG, F32)]).reshape(1, LANE)
    x1, h2, route = _merge(o_a.reshape(t, NSA_WIDTH), o_b.reshape(t, RWKV_WIDTH), o_m.reshape(t, MEM_WIDTH),
                          proj2, x2, b_merge.reshape(1, -1), w_branch.astype(BF16), w_out.astype(BF16),
                          g_ffn.reshape(1, d), rw, rb)

    b1g = exp_b1[:, None, 0::2]
    b1l = exp_b1[:, None, 1::2]
    out = _moe(h2, route, x1, exp_w1, b1g, b1l, exp_w2, exp_b2[:, None, :])
    return out.reshape(b, s, d)


def kernel(x, mem, g_mix, g_mem, w_in, b_merge, nsa_q_norm, nsa_k_norm, cmp_pe, cmp_w1, cmp_b1, cmp_w2,
           rwkv_shift_mix, rwkv_w0, rwkv_w_up, rwkv_a0, rwkv_a_up, rwkv_g_up, rwkv_k_k, rwkv_k_a, rwkv_r_k,
           rwkv_ln_w, rwkv_ln_b, mem_w_kv, mem_q_norm, mem_k_norm, w_branch, w_out, g_ffn,
           router_w, router_b, exp_w1, exp_b1, exp_w2, exp_b2):
    s = x.shape[1]
    cos, sin = _rope_tables(jnp.arange(s), NSA_HEADS)
    n_cmp = (s - CMP_LEN) // CMP_STRIDE + 1
    cos_c, sin_c = _rope_tables(jnp.arange(n_cmp + 1) * CMP_STRIDE + CMP_LEN - 1, NSA_KV_HEADS)
    depth = g_mix.shape[0]
    for l in range(depth):
        x = _layer(x, mem, g_mix[l], g_mem[l], w_in[l], b_merge[l], nsa_q_norm[l], nsa_k_norm[l], cmp_pe[l],
                   cmp_w1[l], cmp_b1[l], cmp_w2[l], rwkv_shift_mix[l], rwkv_w0[l], rwkv_w_up[l], rwkv_a0[l],
                   rwkv_a_up[l], rwkv_g_up[l], rwkv_k_k[l], rwkv_k_a[l], rwkv_r_k[l], rwkv_ln_w[l], rwkv_ln_b[l],
                   mem_w_kv[l], mem_q_norm[l], mem_k_norm[l], w_branch[l], w_out[l], g_ffn[l], router_w[l],
                   router_b[l], exp_w1[l], exp_b1[l], exp_w2[l], exp_b2[l], cos, sin, cos_c, sin_c)
    return x
```

```python
import functools

import jax
import jax.numpy as jnp
from jax import lax
from jax.experimental import pallas as pl
from jax.experimental.pallas import tpu as pltpu

F32 = jnp.float32
BF16 = jnp.bfloat16
HI = lax.Precision.HIGHEST

D_MODEL = 1024
HEAD_DIM = 64
NSA_HEADS = 8
NSA_KV_HEADS = 2
NSA_GROUP = NSA_HEADS // NSA_KV_HEADS
NSA_WIDTH = NSA_HEADS * HEAD_DIM
NSA_KV_WIDTH = NSA_KV_HEADS * HEAD_DIM
CMP_LEN = 32
CMP_STRIDE = 16
CMP_HIDDEN = 128
SEL_BLOCK = 64
TOP_N = 8
WINDOW = 512
Q_BLOCK = 256
FORCE_BONUS = 1000.0
RWKV_HEADS = 8
RWKV_HEAD_DIM = 64
RWKV_WIDTH = RWKV_HEADS * RWKV_HEAD_DIM
DECAY_RANK = 64
AAA_RANK = 64
GATE_RANK = 128
GN_EPS = 64e-5
MEM_HEADS = 4
MEM_HEAD_DIM = 128
MEM_WIDTH = MEM_HEADS * MEM_HEAD_DIM
N_BRANCHES = 3
BRANCH_WIDTH = 512
N_EXPERTS = 32
TOP_K = 4
EXPERT_FF = 1024
SWIGLU_ALPHA = 1.702
SWIGLU_LIMIT = 7.0
ROPE_THETA = 10000.0
RMS_EPS = 1e-6

LANE = 128
LOG2_E = 1.4426950408889634
NEG_BIG = -1e30

RWKV_PAD = 2048
COL_RWKV = 0
COL_CMP = COL_RWKV + RWKV_PAD
PROJ_F32_WIDTH = COL_CMP + 2 * NSA_KV_WIDTH
COL_MERGE = 0
COL_QNSA = COL_MERGE + N_BRANCHES * D_MODEL
COL_QMEM = COL_QNSA + NSA_WIDTH
COL_KV = COL_QMEM + MEM_WIDTH
COL_GATE = COL_KV + 4 * NSA_KV_WIDTH
PROJ_BF16_WIDTH = 5120
RW_R, RW_K, RW_V, RW_XW, RW_XA, RW_XG = 0, 512, 1024, 1536, 1664, 1792

RWKV_CHUNK = 64
VMEM_LIMIT = 56 * 1024 * 1024


def _dot(a, b, prec=None):
    return jnp.dot(a, b, preferred_element_type=F32, precision=prec)


def _dot_nt(a, b, prec=None):
    return lax.dot_general(a, b, (((1,), (1,)), ((), ())), preferred_element_type=F32, precision=prec)


def _dot_tn(a, b, prec=None):
    return lax.dot_general(a, b, (((0,), (0,)), ((), ())), preferred_element_type=F32, precision=prec)


def _split3(x):
    hi = x.astype(BF16)
    r1 = x - hi.astype(F32)
    mid = r1.astype(BF16)
    lo = (r1 - mid.astype(F32)).astype(BF16)
    return hi, mid, lo


def _dot01_left(m01, x):
    n = x.shape[1]
    out = _dot(m01, jnp.concatenate(_split3(x), axis=1))
    return out[:, :n] + out[:, n:2 * n] + out[:, 2 * n:]


def _dot01_right(x, m01):
    m = x.shape[0]
    out = _dot(jnp.concatenate(_split3(x), axis=0), m01)
    return out[:m] + out[m:2 * m] + out[2 * m:]


def _seg_matrix(width, seg):
    r = lax.broadcasted_iota(jnp.int32, (width, width), 0) // seg
    c = lax.broadcasted_iota(jnp.int32, (width, width), 1) // seg
    return (r == c).astype(F32)


SUBLANES = 8


def _store_slabs(ref, x, lead=()):
    rows = x.shape[0]
    for s in range(SUBLANES):
        ref[lead + (pl.ds(s, rows, stride=SUBLANES), slice(None))] = x[:, s * LANE:(s + 1) * LANE]


def _load_slabs(ref, rows, lead=()):
    return jnp.concatenate(
        [ref[lead + (pl.ds(s, rows, stride=SUBLANES), slice(None))] for s in range(SUBLANES)], axis=1)


def _params(sem):
    return pltpu.CompilerParams(dimension_semantics=sem, vmem_limit_bytes=VMEM_LIMIT)


def _inproj_kernel(x_ref, g_ref, w_ref, o_ref, hn_ref):
    @pl.when(pl.program_id(1) == 0)
    def _():
        x = x_ref[...]
        ms = jnp.mean(x * x, axis=-1, keepdims=True)
        hn_ref[...] = (x * lax.rsqrt(ms + RMS_EPS) * g_ref[...]).astype(BF16)

    o_ref[...] = _dot(hn_ref[...], w_ref[...]).astype(o_ref.dtype)


def _inproj(x2, g, w, out_dtype, tn, tm=1024):
    t, d = x2.shape
    n = w.shape[1]
    return pl.pallas_call(
        _inproj_kernel,
        grid=(t // tm, n // tn),
        in_specs=[
            pl.BlockSpec((tm, d), lambda i, j: (i, 0)),
            pl.BlockSpec((1, d), lambda i, j: (0, 0)),
            pl.BlockSpec((d, tn), lambda i, j: (0, j)),
        ],
        out_specs=pl.BlockSpec((tm, tn), lambda i, j: (i, j)),
        out_shape=jax.ShapeDtypeStruct((t, n), out_dtype),
        scratch_shapes=[pltpu.VMEM((tm, d), BF16)],
        compiler_params=_params(("parallel", "arbitrary")),
        name="inproj",
    )(x2, g, w)


def _rwkv_chunk_kernel(p_ref, pprev_ref, mix_ref, w0_ref, wup_ref, a0_ref, aup_ref, gup_ref, kk_ref, ka_ref,
                       rk_ref, rm_ref, y0_ref, bonus_ref, g_ref, gam_ref, m_ref, d0_ref):
    c = RWKV_CHUNK
    n = RWKV_HEAD_DIM
    rows = p_ref.shape[1]

    p = p_ref[0]
    row = lax.broadcasted_iota(jnp.int32, p.shape, 0)
    last_prev = jnp.where(pl.program_id(1) == 0, 0.0, 1.0) * pprev_ref[0, 7:8, :]
    prev = jnp.where(row == 0, last_prev, pltpu.roll(p, 1, axis=0))
    ps = p + (prev - p) * mix_ref[...]
    r = ps[:, RW_R:RW_R + RWKV_WIDTH]
    k = ps[:, RW_K:RW_K + RWKV_WIDTH]
    v = ps[:, RW_V:RW_V + RWKV_WIDTH]
    xw = ps[:, RW_XW:RW_XW + LANE]
    xa = ps[:, RW_XA:RW_XA + LANE]
    xg = ps[:, RW_XG:RW_XG + LANE]

    z = -(w0_ref[...] + _dot(jnp.tanh(xw), wup_ref[...], HI))
    softplus = jnp.maximum(z, 0.0) + jnp.log1p(jnp.exp(-jnp.abs(z)))
    w = -softplus - 0.5
    logw = -jnp.exp(w)
    a = jax.nn.sigmoid(a0_ref[...] + _dot(xa, aup_ref[...], HI))
    g_ref[0] = _dot(jax.nn.sigmoid(xg), gup_ref[...], HI)

    seg = _seg_matrix(RWKV_WIDTH, n).astype(BF16)
    kk = k * kk_ref[...]
    k = k * (1.0 + (a - 1.0) * ka_ref[...])
    sums = _dot01_right(jnp.concatenate([kk * kk, r * k * rk_ref[...]], axis=0), seg)
    kk = kk / jnp.maximum(jnp.sqrt(sums[:rows]), 1e-12)
    bonus_ref[0] = sums[rows:] * v
    kka = kk * a

    ti = lax.broadcasted_iota(jnp.int32, (2 * c, 2 * c), 0)
    tj = lax.broadcasted_iota(jnp.int32, (2 * c, 2 * c), 1)
    keep = (tj % c) < jnp.where(ti < c, ti, ti - c + 1)
    ci = lax.broadcasted_iota(jnp.int32, (c, c), 0)
    cj = lax.broadcasted_iota(jnp.int32, (c, c), 1)
    eye = (ci == cj).astype(F32)
    ltri = (cj <= ci).astype(BF16)
    zeros_cn = jnp.zeros((c, n), BF16)

    chains = []
    for j in range(rows // c):
        rs = slice(j * c, (j + 1) * c)
        lw = logw[rs]
        cum = _dot01_left(ltri, lw)
        tot = cum[c - 1:c, :]
        einv = jnp.exp(-cum)
        dec_end = jnp.exp(tot - cum)
        r_f = r[rs] * jnp.exp(cum)
        a_t = (-kk[rs] * jnp.exp(cum - lw)).astype(BF16)
        b_t = (kka[rs] * einv).astype(BF16)
        k_t = (k[rs] * einv).astype(BF16)
        r_t = r_f.astype(BF16)
        b_e = (kka[rs] * dec_end).astype(BF16)
        k_e = (k[rs] * dec_end).astype(BF16)
        v_b = v[rs].astype(BF16)
        gam_ref[0, j] = jnp.exp(tot)
        for h in range(RWKV_HEADS):
            sl = slice(h * n, (h + 1) * n)
            chains.append(dict(j=j, h=h, rs=rs, sl=sl, a=a_t[:, sl], r=r_t[:, sl], rf=r_f[:, sl], v=v_b[:, sl],
                               rhs=jnp.concatenate([b_t[:, sl], k_t[:, sl]], axis=0),
                               bke=jnp.concatenate([b_e[:, sl], k_e[:, sl]], axis=0)))

    for ch in chains:
        lhs = jnp.concatenate([ch["a"], ch["r"]], axis=0)
        ch["amat"] = jnp.where(keep, _dot_nt(lhs, ch["rhs"]), 0.0)
        ch["pw"] = ch["amat"][:c, :c]
        ch["tinv"] = eye + ch["pw"]
    for _ in range(5):
        for ch in chains:
            pw_b = ch["pw"].astype(BF16)
            ch["pw"] = _dot(pw_b, pw_b)
        for ch in chains:
            ch["tinv"] = ch["tinv"] + _dot(ch["tinv"].astype(BF16), ch["pw"].astype(BF16))
    for ch in chains:
        ch["akv"] = _dot(ch["amat"][:c, c:].astype(BF16), ch["v"])
    for ch in chains:
        wu = _dot(ch["tinv"].astype(BF16), jnp.concatenate([ch["a"], ch["akv"].astype(BF16)], axis=1)).astype(BF16)
        ch["x"] = jnp.concatenate([wu, jnp.concatenate([zeros_cn, ch["v"]], axis=1)], axis=0)
    for ch in chains:
        ry = _dot(ch["amat"][c:, :].astype(BF16), ch["x"])
        rm_ref[0, ch["rs"], ch["sl"]] = (ch["rf"] + ry[:, :n]).astype(BF16)
        y0_ref[0, ch["rs"], ch["sl"]] = ry[:, n:]
    for ch in chains:
        md = _dot_tn(ch["x"], ch["bke"])
        m_ref[0, ch["j"], ch["h"]] = md[:n].astype(BF16)
        d0_ref[0, ch["j"], ch["h"]] = md[n:]


def _rwkv_chunks(proj3, mix, w0, wup, a0, aup, gup, k_k, k_a, r_k, rows=256):
    b, s, _ = proj3.shape
    c = RWKV_CHUNK
    nc = s // c
    cps = rows // c
    vec = lambda width: pl.BlockSpec((1, width), lambda i, t: (0, 0))
    mat = lambda nrows: pl.BlockSpec((nrows, RWKV_WIDTH), lambda i, t: (0, 0))
    tok = pl.BlockSpec((1, rows, RWKV_WIDTH), lambda i, t: (i, t, 0))
    sq = pl.BlockSpec((1, cps, RWKV_HEADS, RWKV_HEAD_DIM, RWKV_HEAD_DIM), lambda i, t: (i, t, 0, 0, 0))
    tok_shape = lambda dt: jax.ShapeDtypeStruct((b, s, RWKV_WIDTH), dt)
    sq_shape = lambda dt: jax.ShapeDtypeStruct((b, nc, RWKV_HEADS, RWKV_HEAD_DIM, RWKV_HEAD_DIM), dt)
    return pl.pallas_call(
        _rwkv_chunk_kernel,
        grid=(b, s // rows),
        in_specs=[
            pl.BlockSpec((1, rows, RWKV_PAD), lambda i, t: (i, t, COL_RWKV // RWKV_PAD)),
            pl.BlockSpec((1, 8, RWKV_PAD), lambda i, t: (i, jnp.maximum(t * (rows // 8) - 1, 0), COL_RWKV // RWKV_PAD)),
            vec(RWKV_PAD), vec(RWKV_WIDTH), mat(LANE), vec(RWKV_WIDTH), mat(LANE), mat(LANE),
            vec(RWKV_WIDTH), vec(RWKV_WIDTH), vec(RWKV_WIDTH),
        ],
        out_specs=[tok, tok, tok, tok,
                   pl.BlockSpec((1, cps, 1, RWKV_WIDTH), lambda i, t: (i, t, 0, 0)), sq, sq],
        out_shape=[tok_shape(BF16), tok_shape(F32), tok_shape(F32), tok_shape(F32),
                   jax.ShapeDtypeStruct((b, nc, 1, RWKV_WIDTH), F32), sq_shape(BF16), sq_shape(F32)],
        compiler_params=_params(("parallel", "parallel")),
        name="rwkv7_chunks",
    )(proj3, proj3, mix, w0, wup, a0, aup, gup, k_k, k_a, r_k)


def _rwkv_scan_kernel(rm_ref, y0_ref, bonus_ref, g_ref, gam_ref, m_ref, d0_ref, lnw_ref, lnb_ref,
                      o_ref, state_ref, y_ref):
    c = RWKV_CHUNK
    n = RWKV_HEAD_DIM

    @pl.when(pl.program_id(1) == 0)
    def _():
        state_ref[...] = jnp.zeros_like(state_ref)

    for j in range(gam_ref.shape[1]):
        rs = slice(j * c, (j + 1) * c)
        gam = gam_ref[0, j]
        for h in range(RWKV_HEADS):
            sl = slice(h * n, (h + 1) * n)
            s = state_ref[h]
            s_b = s.astype(BF16)
            y_ref[rs, sl] = _dot_nt(rm_ref[0, rs, sl], s_b) + y0_ref[0, rs, sl]
            state_ref[h] = s * gam[:, sl] + _dot(s_b, m_ref[0, j, h]) + d0_ref[0, j, h]

    seg = _seg_matrix(RWKV_WIDTH, n).astype(BF16)
    y = y_ref[...]
    mu = _dot01_right(y, seg) * (1.0 / n)
    d = y - mu
    var = _dot01_right(d * d, seg) * (1.0 / n)
    yn = d * lax.rsqrt(var + GN_EPS) * lnw_ref[...] + lnb_ref[...]
    o_ref[0] = (yn + bonus_ref[0]) * g_ref[0]


def _rwkv_scan(rm, y0, bonus, g, gam, m, d0, ln_w, ln_b, rows=512):
    b, s, _ = rm.shape
    cps = rows // RWKV_CHUNK
    vec = pl.BlockSpec((1, RWKV_WIDTH), lambda i, t: (0, 0))
    tok = pl.BlockSpec((1, rows, RWKV_WIDTH), lambda i, t: (i, t, 0))
    sq = pl.BlockSpec((1, cps, RWKV_HEADS, RWKV_HEAD_DIM, RWKV_HEAD_DIM), lambda i, t: (i, t, 0, 0, 0))
    return pl.pallas_call(
        _rwkv_scan_kernel,
        grid=(b, s // rows),
        in_specs=[tok, tok, tok, tok, pl.BlockSpec((1, cps, 1, RWKV_WIDTH), lambda i, t: (i, t, 0, 0)), sq, sq,
                  vec, vec],
        out_specs=tok,
        out_shape=jax.ShapeDtypeStruct((b, s, RWKV_WIDTH), F32),
        scratch_shapes=[
            pltpu.VMEM((RWKV_HEADS, RWKV_HEAD_DIM, RWKV_HEAD_DIM), F32),
            pltpu.VMEM((rows, RWKV_WIDTH), F32),
        ],
        compiler_params=_params(("parallel", "arbitrary")),
        name="rwkv7_scan",
    )(rm, y0, bonus, g, gam, m, d0, ln_w, ln_b)


def _rwkv(proj3, mix, w0, wup, a0, aup, gup, k_k, k_a, r_k, ln_w, ln_b):
    rm, y0, bonus, g, gam, m, d0 = _rwkv_chunks(proj3, mix, w0, wup, a0, aup, gup, k_k, k_a, r_k)
    return _rwkv_scan(rm, y0, bonus, g, gam, m, d0, ln_w, ln_b)


def _rope(x, cos, sin_signed):
    w = x.shape[-1]
    first_half = (lax.broadcasted_iota(jnp.int32, x.shape, 1) % HEAD_DIM) < (HEAD_DIM // 2)
    rot = jnp.where(first_half, pltpu.roll(x, w - HEAD_DIM // 2, axis=1), pltpu.roll(x, HEAD_DIM // 2, axis=1))
    return x * cos + rot * sin_signed


def _head_rmsnorm(x, gain, seg):
    ms = _dot01_right(x * x, seg) * (1.0 / HEAD_DIM)
    return x * lax.rsqrt(ms + RMS_EPS) * gain


def _split_groups(x):
    return [x[:, g * HEAD_DIM:(g + 1) * HEAD_DIM] for g in range(NSA_KV_HEADS)]


def _nsa_prep_kernel(q_ref, ksl_ref, vsl_ref, kwn_ref, vwn_ref, cos_ref, sin_ref, qn_ref, kn_ref,
                     qo_ref, ks_ref, vs_ref, kw_ref, vw_ref):
    seg_q = _seg_matrix(NSA_WIDTH, HEAD_DIM).astype(BF16)
    seg_k = _seg_matrix(NSA_KV_WIDTH, HEAD_DIM).astype(BF16)
    cos_k = cos_ref[:, :NSA_KV_WIDTH]
    sin_k = sin_ref[:, :NSA_KV_WIDTH]

    q = _rope(_head_rmsnorm(q_ref[0].astype(F32), qn_ref[...], seg_q), cos_ref[...], sin_ref[...])
    q_t = (q * (HEAD_DIM ** -0.5 * LOG2_E)).T
    for h in range(NSA_HEADS):
        qo_ref[0, h] = q_t[h * HEAD_DIM:(h + 1) * HEAD_DIM].astype(BF16)

    ks = _rope(_head_rmsnorm(ksl_ref[0].astype(F32), kn_ref[1:2, :], seg_k), cos_k, sin_k)
    kw = _rope(_head_rmsnorm(kwn_ref[0].astype(F32), kn_ref[2:3, :], seg_k), cos_k, sin_k)
    vs_t = vsl_ref[0].astype(F32).T
    vw_t = vwn_ref[0].astype(F32).T
    for g, (a, c_) in enumerate(zip(_split_groups(ks), _split_groups(kw))):
        ks_ref[0, g] = a.astype(BF16)
        kw_ref[0, g] = c_.astype(BF16)
        vs_ref[0, g] = vs_t[g * HEAD_DIM:(g + 1) * HEAD_DIM].astype(BF16)
        vw_ref[0, g] = vw_t[g * HEAD_DIM:(g + 1) * HEAD_DIM].astype(BF16)


def _nsa_prep(proj3, cos, sin_signed, q_norm, k_norm, tq=512):
    b, s, _ = proj3.shape
    kvw = NSA_KV_WIDTH
    kv_spec = lambda j: pl.BlockSpec((1, tq, kvw), lambda i, t: (i, t, COL_KV // kvw + j))
    out_k = pl.BlockSpec((1, NSA_KV_HEADS, tq, HEAD_DIM), lambda i, t: (i, 0, t, 0))
    out_vt = pl.BlockSpec((1, NSA_KV_HEADS, HEAD_DIM, tq), lambda i, t: (i, 0, 0, t))
    k_shape = jax.ShapeDtypeStruct((b, NSA_KV_HEADS, s, HEAD_DIM), BF16)
    vt_shape = jax.ShapeDtypeStruct((b, NSA_KV_HEADS, HEAD_DIM, s), BF16)
    return pl.pallas_call(
        _nsa_prep_kernel,
        grid=(b, s // tq),
        in_specs=[
            pl.BlockSpec((1, tq, NSA_WIDTH), lambda i, t: (i, t, COL_QNSA // NSA_WIDTH)),
            kv_spec(0), kv_spec(1), kv_spec(2), kv_spec(3),
            pl.BlockSpec((tq, NSA_WIDTH), lambda i, t: (t, 0)),
            pl.BlockSpec((tq, NSA_WIDTH), lambda i, t: (t, 0)),
            pl.BlockSpec((1, NSA_WIDTH), lambda i, t: (0, 0)),
            pl.BlockSpec((3, kvw), lambda i, t: (0, 0)),
        ],
        out_specs=[
            pl.BlockSpec((1, NSA_HEADS, HEAD_DIM, tq), lambda i, t: (i, 0, 0, t)),
            out_k, out_vt, out_k, out_vt,
        ],
        out_shape=[jax.ShapeDtypeStruct((b, NSA_HEADS, HEAD_DIM, s), BF16), k_shape, vt_shape, k_shape, vt_shape],
        compiler_params=_params(("parallel", "parallel")),
        name="nsa_prep",
    )(proj3, proj3, proj3, proj3, proj3, cos, sin_signed, q_norm, k_norm)


def _gelu_tanh(x):
    return 0.5 * x * (1.0 + jnp.tanh(0.7978845608028654 * (x + 0.044715 * x * x * x)))


def _nsa_cmp_kernel(kc_in_ref, vc_in_ref, cos_ref, sin_ref, kn_ref, pe_ref, w1_ref, b1_ref, w2_ref,
                    kc_ref, vc_ref):
    n_cmp = (kc_in_ref.shape[1] - CMP_LEN) // CMP_STRIDE + 1
    n_pad = n_cmp + 1
    zero_row = jnp.zeros((1, NSA_KV_WIDTH), F32)
    outs = []
    for j, src in enumerate((kc_in_ref, vc_in_ref)):
        acc = jnp.zeros((NSA_KV_HEADS * n_pad, CMP_HIDDEN), F32)
        for l in range(CMP_LEN):
            x = src[0, pl.ds(l, n_cmp, stride=CMP_STRIDE), :]
            x = jnp.concatenate([x, zero_row], axis=0)
            xg = jnp.concatenate(_split_groups(x), axis=0) + pe_ref[j, l:l + 1, :]
            acc = acc + _dot(xg.astype(BF16), w1_ref[j, l * HEAD_DIM:(l + 1) * HEAD_DIM, :])
        hid = _gelu_tanh(acc + b1_ref[j])
        out = _dot(hid.astype(BF16), w2_ref[j])
        outs.append(jnp.concatenate([out[g * n_pad:(g + 1) * n_pad] for g in range(NSA_KV_HEADS)], axis=1))
    kc, vc = outs
    seg_k = _seg_matrix(NSA_KV_WIDTH, HEAD_DIM).astype(BF16)
    kc = _rope(_head_rmsnorm(kc, kn_ref[0:1, :], seg_k), cos_ref[...], sin_ref[...])
    vc_t = vc.T
    for g, a in enumerate(_split_groups(kc)):
        kc_ref[0, g] = a.astype(BF16)
        vc_ref[0, g] = vc_t[g * HEAD_DIM:(g + 1) * HEAD_DIM].astype(BF16)


def _nsa_cmp(proj3, cos_c, sin_c, k_norm, pe, w1, b1, w2):
    b, s, _ = proj3.shape
    kvw = NSA_KV_WIDTH
    n_pad = (s - CMP_LEN) // CMP_STRIDE + 2
    full = lambda shape: pl.BlockSpec(shape, lambda i: (0,) * len(shape))
    k_spec = pl.BlockSpec((1, NSA_KV_HEADS, n_pad, HEAD_DIM), lambda i: (i, 0, 0, 0))
    vt_spec = pl.BlockSpec((1, NSA_KV_HEADS, HEAD_DIM, n_pad), lambda i: (i, 0, 0, 0))
    k_shape = jax.ShapeDtypeStruct((b, NSA_KV_HEADS, n_pad, HEAD_DIM), BF16)
    vt_shape = jax.ShapeDtypeStruct((b, NSA_KV_HEADS, HEAD_DIM, n_pad), BF16)
    return pl.pallas_call(
        _nsa_cmp_kernel,
        grid=(b,),
        in_specs=[
            pl.BlockSpec((1, s, kvw), lambda i: (i, 0, COL_CMP // kvw)),
            pl.BlockSpec((1, s, kvw), lambda i: (i, 0, COL_CMP // kvw + 1)),
            full((n_pad, kvw)), full((n_pad, kvw)), full((3, kvw)),
            full((2, CMP_LEN, HEAD_DIM)), full((2, CMP_LEN * HEAD_DIM, CMP_HIDDEN)),
            full((2, 1, CMP_HIDDEN)), full((2, CMP_HIDDEN, HEAD_DIM)),
        ],
        out_specs=[k_spec, vt_spec],
        out_shape=[k_shape, vt_shape],
        compiler_params=_params(("parallel",)),
        name="nsa_compress",
    )(proj3, proj3, cos_c, sin_c, k_norm, pe, w1, b1, w2)


def _nsa_attn_kernel(q_ref, gate_ref, kc_ref, vc_ref, ks_ref, vs_ref, kw_ref, vw_ref, o_ref):
    qb = Q_BLOCK
    hg = NSA_GROUP
    g = pl.program_id(1)
    c = pl.program_id(2)
    cols = hg * qb
    qt = jnp.concatenate([q_ref[0, h] for h in range(hg)], axis=1)
    tile_heads = lambda x: jnp.concatenate([x] * hg, axis=1)

    n_cp = kc_ref.shape[2]
    s_c = _dot(kc_ref[0, 0], qt)
    cend = lax.broadcasted_iota(jnp.int32, (n_cp, qb), 0) * CMP_STRIDE + CMP_LEN - 1
    cmask = tile_heads(jnp.where(cend <= c * qb + lax.broadcasted_iota(jnp.int32, (n_cp, qb), 1), 1, 0)) > 0
    s_m = jnp.where(cmask, s_c, -jnp.inf)
    m = jnp.max(s_m, axis=0, keepdims=True)
    m = jnp.where(m == -jnp.inf, 0.0, m)
    e = jnp.where(cmask, jnp.exp2(s_m - m), 0.0)
    p_c = e / jnp.maximum(jnp.sum(e, axis=0, keepdims=True), 1e-30)
    o_cmp = _dot(vc_ref[0, 0], p_c.astype(BF16))

    n_blk = ks_ref.shape[2] // SEL_BLOCK
    p_sum = p_c[:, 0:qb]
    for h in range(1, hg):
        p_sum = p_sum + p_c[:, h * qb:(h + 1) * qb]
    bi = lax.broadcasted_iota(jnp.int32, (n_blk, n_cp), 0) * SEL_BLOCK
    ci = lax.broadcasted_iota(jnp.int32, (n_blk, n_cp), 1) * CMP_STRIDE
    cover_t = ((ci < bi + SEL_BLOCK) & (ci + CMP_LEN > bi)).astype(F32)
    imp = _dot(cover_t, p_sum, HI)
    blk = lax.broadcasted_iota(jnp.int32, (n_blk, qb), 0)
    qpos = c * qb + lax.broadcasted_iota(jnp.int32, (n_blk, qb), 1)
    cur = qpos // SEL_BLOCK
    forced = (blk == 0) | (blk == cur) | (blk == cur - 1)
    imp = jnp.where(forced, imp + FORCE_BONUS, imp)
    imp = jnp.where(blk <= cur, imp, -jnp.inf)
    rank = jnp.zeros((n_blk, qb), F32)
    for j in range(n_blk):
        other = imp[j:j + 1, :]
        rank = rank + jnp.where(blk > j, jnp.where(other >= imp, 1.0, 0.0), jnp.where(other > imp, 1.0, 0.0))
    sel_t = jnp.where(rank < min(TOP_N, n_blk), 1.0, 0.0).astype(BF16)

    kt_w = 4 * LANE
    kq = c * qb + lax.broadcasted_iota(jnp.int32, (kt_w, qb), 1)
    krel = lax.broadcasted_iota(jnp.int32, (kt_w, qb), 0)
    er = lax.broadcasted_iota(jnp.int32, (kt_w, n_blk), 0) // SEL_BLOCK
    ec = lax.broadcasted_iota(jnp.int32, (kt_w, n_blk), 1)

    def sel_body(kt, carry):
        m_i, l_i, acc = carry
        off = pl.multiple_of(kt * kt_w, kt_w)
        expand = jnp.where(ec == er + kt * (kt_w // SEL_BLOCK), 1.0, 0.0).astype(BF16)
        in_sel = _dot(expand, sel_t)
        bias = jnp.where((in_sel > 0.5) & (krel + off <= kq), 0.0, NEG_BIG)
        s = _dot(ks_ref[0, 0, pl.ds(off, kt_w), :], qt) + tile_heads(bias)
        m_new = jnp.maximum(m_i, jnp.max(s, axis=0, keepdims=True))
        alpha = jnp.exp2(m_i - m_new)
        p = jnp.exp2(s - m_new)
        l_new = alpha * l_i + jnp.sum(p, axis=0, keepdims=True)
        acc = alpha * acc + _dot(vs_ref[0, 0, :, pl.ds(off, kt_w)], p.astype(BF16))
        return m_new, l_new, acc

    init = (jnp.full((1, cols), NEG_BIG, F32), jnp.zeros((1, cols), F32), jnp.zeros((HEAD_DIM, cols), F32))
    _, l_s, acc_s = lax.fori_loop(0, (c * qb + qb + kt_w - 1) // kt_w, sel_body, init)
    o_slc = acc_s / l_s

    span = WINDOW + 2 * qb
    w0 = pl.multiple_of(jnp.clip(c * qb - WINDOW, 0, kw_ref.shape[2] - span), qb)
    kpos = w0 + lax.broadcasted_iota(jnp.int32, (span, qb), 0)
    wq = c * qb + lax.broadcasted_iota(jnp.int32, (span, qb), 1)
    wbias = jnp.where((kpos <= wq) & (kpos > wq - WINDOW), 0.0, NEG_BIG)
    s_w = _dot(kw_ref[0, 0, pl.ds(w0, span), :], qt) + tile_heads(wbias)
    e_w = jnp.exp2(s_w - jnp.max(s_w, axis=0, keepdims=True))
    o_win = _dot(vw_ref[0, 0, :, pl.ds(w0, span)], e_w.astype(BF16)) / jnp.sum(e_w, axis=0, keepdims=True)

    gates_t = jax.nn.sigmoid(gate_ref[0].astype(F32)).T
    grow = lax.broadcasted_iota(jnp.int32, gates_t.shape, 0)
    outs = []
    for h in range(hg):
        hc = slice(h * qb, (h + 1) * qb)
        first = (g * hg + h) * 3
        pick = lambda j: jnp.sum(jnp.where(grow == first + j, gates_t, 0.0), axis=0, keepdims=True)
        outs.append(pick(0) * o_cmp[:, hc] + pick(1) * o_slc[:, hc] + pick(2) * o_win[:, hc])
    o_ref[0] = jnp.concatenate(outs, axis=0).T


def _nsa_attn(q, proj3, kc, vc, ks, vs, kw, vw):
    b, _, _, s = q.shape
    gw = NSA_GROUP * HEAD_DIM
    n_pad = kc.shape[2]
    per_group = lambda rows, width: pl.BlockSpec((1, 1, rows, width), lambda i, g, c: (i, g, 0, 0))
    return pl.pallas_call(
        _nsa_attn_kernel,
        grid=(b, NSA_KV_HEADS, s // Q_BLOCK),
        in_specs=[
            pl.BlockSpec((1, NSA_GROUP, HEAD_DIM, Q_BLOCK), lambda i, g, c: (i, g, 0, c)),
            pl.BlockSpec((1, Q_BLOCK, LANE), lambda i, g, c: (i, c, COL_GATE // LANE)),
            per_group(n_pad, HEAD_DIM), per_group(HEAD_DIM, n_pad),
            per_group(s, HEAD_DIM), per_group(HEAD_DIM, s), per_group(s, HEAD_DIM), per_group(HEAD_DIM, s),
        ],
        out_specs=pl.BlockSpec((1, Q_BLOCK, gw), lambda i, g, c: (i, c, g)),
        out_shape=jax.ShapeDtypeStruct((b, s, NSA_WIDTH), F32),
        compiler_params=_params(("parallel", "parallel", "arbitrary")),
        name="nsa_attn",
    )(q, proj3, kc, vc, ks, vs, kw, vw)


def _mem_kv_kernel(mem_ref, g_ref, w_ref, kn_ref, k_ref, v_ref):
    x = mem_ref[0]
    ms = jnp.mean(x * x, axis=-1, keepdims=True)
    xn = (x * lax.rsqrt(ms + RMS_EPS) * g_ref[...]).astype(BF16)
    kv = _dot(xn, w_ref[...])
    for h in range(MEM_HEADS):
        sl = slice(h * MEM_HEAD_DIM, (h + 1) * MEM_HEAD_DIM)
        kh = kv[:, sl]
        kms = jnp.mean(kh * kh, axis=-1, keepdims=True)
        k_ref[0, :, sl] = (kh * lax.rsqrt(kms + RMS_EPS) * kn_ref[...]).astype(BF16)
    v_ref[0] = kv[:, MEM_WIDTH:].astype(BF16)


def _mem_kv(mem, g_mem, w_kv, k_norm):
    b, m, d = mem.shape
    spec = pl.BlockSpec((1, m, MEM_WIDTH), lambda i: (i, 0, 0))
    shape = jax.ShapeDtypeStruct((b, m, MEM_WIDTH), BF16)
    return pl.pallas_call(
        _mem_kv_kernel,
        grid=(b,),
        in_specs=[
            pl.BlockSpec((1, m, d), lambda i: (i, 0, 0)),
            pl.BlockSpec((1, d), lambda i: (0, 0)),
            pl.BlockSpec((d, 2 * MEM_WIDTH), lambda i: (0, 0)),
            pl.BlockSpec((1, MEM_HEAD_DIM), lambda i: (0, 0)),
        ],
        out_specs=[spec, spec],
        out_shape=[shape, shape],
        compiler_params=_params(("parallel",)),
        name="mem_kv",
    )(mem, g_mem, w_kv, k_norm)


def _mem_attn_kernel(q_ref, qn_ref, k_ref, v_ref, o_ref):
    q = q_ref[0].astype(F32)
    for h in range(MEM_HEADS):
        sl = slice(h * MEM_HEAD_DIM, (h + 1) * MEM_HEAD_DIM)
        qh = q[:, sl]
        ms = jnp.mean(qh * qh, axis=-1, keepdims=True)
        qh = (qh * lax.rsqrt(ms + RMS_EPS) * qn_ref[...]).astype(BF16)
        s = _dot_nt(qh, k_ref[0, :, sl]) * (MEM_HEAD_DIM ** -0.5)
        m = jnp.max(s, axis=-1, keepdims=True)
        e = jnp.exp(s - m)
        p = e / jnp.sum(e, axis=-1, keepdims=True)
        o_ref[0, :, sl] = _dot(p.astype(BF16), v_ref[0, :, sl])


def _mem_attn(proj3, q_norm, k, v, tq=512):
    b, s, _ = proj3.shape
    m = k.shape[1]
    kv_spec = pl.BlockSpec((1, m, MEM_WIDTH), lambda i, t: (i, 0, 0))
    return pl.pallas_call(
        _mem_attn_kernel,
        grid=(b, s // tq),
        in_specs=[
            pl.BlockSpec((1, tq, MEM_WIDTH), lambda i, t: (i, t, COL_QMEM // MEM_WIDTH)),
            pl.BlockSpec((1, MEM_HEAD_DIM), lambda i, t: (0, 0)),
            kv_spec, kv_spec,
        ],
        out_specs=pl.BlockSpec((1, tq, MEM_WIDTH), lambda i, t: (i, t, 0)),
        out_shape=jax.ShapeDtypeStruct((b, s, MEM_WIDTH), F32),
        compiler_params=_params(("parallel", "parallel")),
        name="mem_attn",
    )(proj3, q_norm, k, v)


def _merge_kernel(oa_ref, ob_ref, om_ref, l0_ref, l1_ref, l2_ref, x_ref, bm_ref, wb_ref, wo_ref,
                  gf_ref, rw_ref, rb_ref, x1_ref, h_ref, route_ref):
    mixed = None
    for n, (o_ref, l_ref) in enumerate(((oa_ref, l0_ref), (ob_ref, l1_ref), (om_ref, l2_ref))):
        gate = jax.nn.sigmoid(l_ref[...].astype(F32) + bm_ref[:, n * D_MODEL:(n + 1) * D_MODEL])
        term = gate * _dot(o_ref[...].astype(BF16), wb_ref[n])
        mixed = term if mixed is None else mixed + term
    x1 = x_ref[...] + _dot(mixed.astype(BF16), wo_ref[...])
    x1_ref[...] = x1
    ms = jnp.mean(x1 * x1, axis=-1, keepdims=True)
    h = x1 * lax.rsqrt(ms + RMS_EPS) * gf_ref[...]
    _store_slabs(h_ref, h)

    logits = _dot(h, rw_ref[...], HI) + rb_ref[...]
    lane_f = lax.broadcasted_iota(jnp.int32, logits.shape, 1).astype(F32)
    work = logits
    picks = []
    for _ in range(TOP_K):
        mx = jnp.max(work, axis=-1, keepdims=True)
        idx = jnp.min(jnp.where(work == mx, lane_f, 2.0 * LANE), axis=-1, keepdims=True)
        picks.append((idx, mx))
        work = jnp.where(lane_f == idx, -jnp.inf, work)
    exps = [jnp.exp(mx - picks[0][1]) for _, mx in picks]
    denom = functools.reduce(lambda a, b: a + b, exps)
    route = jnp.zeros(logits.shape, F32)
    for kk, ((idx, _), ex) in enumerate(zip(picks, exps)):
        route = jnp.where(lane_f == kk, idx, route)
        route = jnp.where(lane_f == TOP_K + kk, ex / denom, route)
    route_ref[...] = route


def _merge(o_a, o_b, o_m, proj2, x2, b_merge, w_branch, w_out, g_ffn, router_w, router_b, tm=512):
    t, d = x2.shape
    row = lambda width: pl.BlockSpec((tm, width), lambda i: (i, 0))
    logit = lambda n: pl.BlockSpec((tm, d), lambda i: (i, COL_MERGE // d + n))
    full = lambda shape: pl.BlockSpec(shape, lambda i: (0,) * len(shape))
    return pl.pallas_call(
        _merge_kernel,
        grid=(t // tm,),
        in_specs=[
            row(BRANCH_WIDTH), row(BRANCH_WIDTH), row(BRANCH_WIDTH), logit(0), logit(1), logit(2), row(d),
            full((1, N_BRANCHES * d)), full((N_BRANCHES, BRANCH_WIDTH, d)), full((d, d)), full((1, d)),
            full((d, LANE)), full((1, LANE)),
        ],
        out_specs=[row(d), pl.BlockSpec((tm * SUBLANES, LANE), lambda i: (i, 0)), row(LANE)],
        out_shape=[jax.ShapeDtypeStruct((t, d), F32), jax.ShapeDtypeStruct((t * SUBLANES, LANE), F32),
                   jax.ShapeDtypeStruct((t, LANE), F32)],
        compiler_params=_params(("parallel",)),
        name="merge_router",
    )(o_a, o_b, o_m, proj2, proj2, proj2, x2, b_merge, w_branch, w_out, g_ffn, router_w, router_b)


def _deinterleave(x):
    rows, width = x.shape
    lane = lax.broadcasted_iota(jnp.int32, (rows, LANE), 1)
    half = LANE // 2
    low = lane < half
    idx = jnp.where(low, 2 * lane, 2 * (lane - half) + 1)
    evens, odds = [], []
    for j in range(0, width, 2 * LANE):
        a = jnp.take_along_axis(x[:, j:j + LANE], idx, axis=1)
        b = jnp.take_along_axis(x[:, j + LANE:j + 2 * LANE], idx, axis=1)
        evens.append(jnp.where(low, a, pltpu.roll(b, half, axis=1)))
        odds.append(jnp.where(low, pltpu.roll(a, half, axis=1), b))
    return jnp.concatenate(evens, axis=1), jnp.concatenate(odds, axis=1)


MOE_TILE = 512


def _route_kernel(route_ref, dest_ref, counts_ref, carry_ref, start_ref):
    phase = pl.program_id(0)
    i = pl.program_id(1)
    tr = route_ref.shape[0]
    route = route_ref[...]
    lane = lax.broadcasted_iota(jnp.int32, route.shape, 1)
    lane_f = lane.astype(F32)
    hits = [lane_f == route[:, kk:kk + 1] for kk in range(TOP_K)]
    sel = jnp.zeros(route.shape, F32)
    for hit in hits:
        sel = jnp.where(hit, 1.0, sel)

    @pl.when((phase == 0) & (i == 0))
    def _():
        carry_ref[...] = jnp.zeros_like(carry_ref)

    @pl.when((phase == 1) & (i == 0))
    def _():
        counts = carry_ref[...]
        counts_ref[...] = counts
        padded = jnp.ceil(counts * (1.0 / MOE_TILE)) * MOE_TILE
        ui = lax.broadcasted_iota(jnp.int32, (LANE, LANE), 0)
        uj = lax.broadcasted_iota(jnp.int32, (LANE, LANE), 1)
        start_ref[...] = _dot01_right(jnp.broadcast_to(padded, (8, LANE)), (ui < uj).astype(BF16))[0:1]
        carry_ref[...] = jnp.zeros_like(carry_ref)

    @pl.when(phase == 1)
    def _():
        ri = lax.broadcasted_iota(jnp.int32, (tr, tr), 0)
        rj = lax.broadcasted_iota(jnp.int32, (tr, tr), 1)
        rank = carry_ref[...] + _dot((rj < ri).astype(BF16), sel.astype(BF16))
        row = start_ref[...] + rank
        dest = jnp.zeros(route.shape, F32)
        for kk, hit in enumerate(hits):
            dest = jnp.where(lane == kk, jnp.sum(jnp.where(hit, row, 0.0), axis=-1, keepdims=True), dest)
        dest_ref[...] = dest.astype(jnp.int32)

    carry_ref[...] += jnp.sum(sel, axis=0, keepdims=True)


def _route(route, tr=512):
    t = route.shape[0]
    return pl.pallas_call(
        _route_kernel,
        grid=(2, t // tr),
        in_specs=[pl.BlockSpec((tr, LANE), lambda p, i: (i, 0))],
        out_specs=[pl.BlockSpec((tr, LANE), lambda p, i: (i * p, 0)), pl.BlockSpec((1, LANE), lambda p, i: (0, 0))],
        out_shape=[jax.ShapeDtypeStruct((t, LANE), jnp.int32), jax.ShapeDtypeStruct((1, LANE), F32)],
        scratch_shapes=[pltpu.VMEM((1, LANE), F32), pltpu.VMEM((1, LANE), F32)],
        compiler_params=_params(("arbitrary", "arbitrary")),
        name="moe_route",
    )(route)


def _row_copy(src, src_row, dst, dst_row, sem):
    src_at = pl.ds(pl.multiple_of(src_row * SUBLANES, SUBLANES), SUBLANES)
    dst_at = pl.ds(pl.multiple_of(dst_row * SUBLANES, SUBLANES), SUBLANES)
    return pltpu.make_async_copy(src.at[src_at], dst.at[dst_at], sem)


def _zero_fill(pad_start_ref, pad_len_ref, nu_ref, xs_ref, zero_ref, zsem, wait):
    n_e = pad_start_ref.shape[0]
    n_tiles = xs_ref.shape[0] // (MOE_TILE * SUBLANES)

    def zero_rows(first_row, n_rows):
        src = zero_ref.at[pl.ds(0, n_rows * SUBLANES)]
        dst = xs_ref.at[pl.ds(pl.multiple_of(first_row * SUBLANES, SUBLANES), n_rows * SUBLANES)]
        copy = pltpu.make_async_copy(src, dst, zsem)
        copy.wait() if wait else copy.start()

    for e in range(n_e):
        row = pad_start_ref[e]
        for bit in [1 << b for b in reversed(range(MOE_TILE.bit_length() - 1))]:
            has = (pad_len_ref[e] & bit) != 0
            pl.when(has)(functools.partial(zero_rows, row, bit))
            row = row + jnp.where(has, bit, 0)
    for k in range(n_e):
        tile = n_tiles - n_e + k
        pl.when(tile >= nu_ref[0])(functools.partial(zero_rows, tile * MOE_TILE, MOE_TILE))


def _dispatch_kernel(pad_start_ref, pad_len_ref, nu_ref, dest_ref, h_ref, xs_ref, zero_ref, sem, zsem):
    td = h_ref.shape[0] // SUBLANES

    @pl.when(pl.program_id(0) == 0)
    def _():
        zero_ref[...] = jnp.zeros_like(zero_ref)
        _zero_fill(pad_start_ref, pad_len_ref, nu_ref, xs_ref, zero_ref, zsem, wait=False)

    def issue(r, carry):
        for kk in range(TOP_K):
            _row_copy(h_ref, r, xs_ref, dest_ref[r * TOP_K + kk], sem).start(priority=kk % 2)
        return carry

    lax.fori_loop(0, td, issue, 0, unroll=8)

    def drain(r, carry):
        for kk in range(TOP_K):
            _row_copy(h_ref, 0, xs_ref, 0, sem).wait()
        return carry

    lax.fori_loop(0, td, drain, 0, unroll=8)

    @pl.when(pl.program_id(0) == 0)
    def _():
        _zero_fill(pad_start_ref, pad_len_ref, nu_ref, xs_ref, zero_ref, zsem, wait=True)


def _dispatch(pad_start, pad_len, n_used, dest_flat, h, n_rows, td=512):
    t = h.shape[0] // SUBLANES
    return pl.pallas_call(
        _dispatch_kernel,
        grid_spec=pltpu.PrefetchScalarGridSpec(
            num_scalar_prefetch=3,
            grid=(t // td,),
            in_specs=[
                pl.BlockSpec((td * TOP_K,), lambda i, ps, pn, nu: (i,), memory_space=pltpu.SMEM),
                pl.BlockSpec((td * SUBLANES, LANE), lambda i, ps, pn, nu: (i, 0)),
            ],
            out_specs=pl.BlockSpec(memory_space=pl.ANY),
            scratch_shapes=[pltpu.VMEM((MOE_TILE * SUBLANES, LANE), F32), pltpu.SemaphoreType.DMA,
                            pltpu.SemaphoreType.DMA],
        ),
        out_shape=jax.ShapeDtypeStruct((n_rows * SUBLANES, LANE), F32),
        compiler_params=_params(("arbitrary",)),
        name="moe_dispatch",
    )(pad_start, pad_len, n_used, dest_flat, h)


def _experts_kernel(te_ref, nu_ref, xs_ref, w1_ref, b1g_ref, b1l_ref, w2_ref, b2_ref, y_ref, w1b_ref, w2b_ref):
    j = pl.program_id(0)

    @pl.when(j < nu_ref[0])
    def _():
        @pl.when((j == 0) | (te_ref[j] != te_ref[jnp.maximum(j - 1, 0)]))
        def _():
            w1b_ref[...] = w1_ref[0].astype(BF16)
            w2b_ref[...] = w2_ref[0].astype(BF16)

        x = _load_slabs(xs_ref, MOE_TILE).astype(BF16)
        hid_g, hid_l = _deinterleave(_dot(x, w1b_ref[...]))
        x_glu = jnp.minimum(hid_g + b1g_ref[0], SWIGLU_LIMIT)
        x_lin = jnp.clip(hid_l + b1l_ref[0], -SWIGLU_LIMIT, SWIGLU_LIMIT)
        act = x_glu * jax.nn.sigmoid(SWIGLU_ALPHA * x_glu) * (x_lin + 1.0)
        _store_slabs(y_ref, _dot(act.astype(BF16), w2b_ref[...]) + b2_ref[0])

    @pl.when(j >= nu_ref[0])
    def _():
        y_ref[...] = jnp.zeros_like(y_ref)


def _experts(tile_expert, n_used, xs, w1, b1g, b1l, w2, b2):
    rows = xs.shape[0] // SUBLANES
    n_e, ff, d = w2.shape
    tg = MOE_TILE
    used = lambda j, te, nu: jnp.minimum(j, nu[0] - 1)
    exp_of = lambda j, te, nu: te[used(j, te, nu)]
    return pl.pallas_call(
        _experts_kernel,
        grid_spec=pltpu.PrefetchScalarGridSpec(
            num_scalar_prefetch=2,
            grid=(rows // tg,),
            in_specs=[
                pl.BlockSpec((tg * SUBLANES, LANE), lambda j, te, nu: (used(j, te, nu), 0)),
                pl.BlockSpec((1, d, 2 * ff), lambda j, te, nu: (exp_of(j, te, nu), 0, 0)),
                pl.BlockSpec((1, 1, ff), lambda j, te, nu: (exp_of(j, te, nu), 0, 0)),
                pl.BlockSpec((1, 1, ff), lambda j, te, nu: (exp_of(j, te, nu), 0, 0)),
                pl.BlockSpec((1, ff, d), lambda j, te, nu: (exp_of(j, te, nu), 0, 0)),
                pl.BlockSpec((1, 1, d), lambda j, te, nu: (exp_of(j, te, nu), 0, 0)),
            ],
            out_specs=pl.BlockSpec((tg * SUBLANES, LANE), lambda j, te, nu: (j, 0)),
            scratch_shapes=[pltpu.VMEM((d, 2 * ff), BF16), pltpu.VMEM((ff, d), BF16)],
        ),
        out_shape=jax.ShapeDtypeStruct((rows * SUBLANES, LANE), F32),
        compiler_params=_params(("arbitrary",)),
        name="moe_experts",
    )(tile_expert, n_used, xs, w1, b1g, b1l, w2, b2)


def _combine_kernel(dest_ref, dest_next_ref, route_ref, x1_ref, y_ref, o_ref, buf_ref, sems):
    i = pl.program_id(0)
    tc = x1_ref.shape[0]
    slot = i % 2

    def gather(dref, to_slot):
        def issue(r, carry):
            for kk in range(TOP_K):
                _row_copy(y_ref, dref[r * TOP_K + kk], buf_ref.at[to_slot, kk], r, sems.at[to_slot]).start(
                    priority=kk % 2)
            return carry

        lax.fori_loop(0, tc, issue, 0, unroll=8)

    @pl.when(i == 0)
    def _():
        gather(dest_ref, slot)

    @pl.when(i + 1 < pl.num_programs(0))
    def _():
        gather(dest_next_ref, 1 - slot)

    def drain(r, carry):
        for kk in range(TOP_K):
            _row_copy(y_ref, 0, buf_ref.at[slot, kk], 0, sems.at[slot]).wait()
        return carry

    lax.fori_loop(0, tc, drain, 0, unroll=8)

    route = route_ref[...]
    acc = x1_ref[...]
    for kk in range(TOP_K):
        acc = acc + route[:, TOP_K + kk:TOP_K + kk + 1] * _load_slabs(buf_ref, tc, lead=(slot, kk))
    o_ref[...] = acc


def _combine(dest_flat, route, x1, y, tc=256):
    t, d = x1.shape
    last = t // tc - 1
    return pl.pallas_call(
        _combine_kernel,
        grid=(t // tc,),
        in_specs=[
            pl.BlockSpec((tc * TOP_K,), lambda i: (i,), memory_space=pltpu.SMEM),
            pl.BlockSpec((tc * TOP_K,), lambda i: (jnp.minimum(i + 1, last),), memory_space=pltpu.SMEM),
            pl.BlockSpec((tc, LANE), lambda i: (i, 0)),
            pl.BlockSpec((tc, d), lambda i: (i, 0)),
            pl.BlockSpec(memory_space=pl.ANY),
        ],
        out_specs=pl.BlockSpec((tc, d), lambda i: (i, 0)),
        out_shape=jax.ShapeDtypeStruct((t, d), F32),
        scratch_shapes=[pltpu.VMEM((2, TOP_K, tc * SUBLANES, LANE), F32), pltpu.SemaphoreType.DMA((2,))],
        compiler_params=_params(("arbitrary",)),
        name="moe_combine",
    )(dest_flat, dest_flat, route, x1, y)


def _moe(h, route, x1, w1, b1g, b1l, w2, b2):
    t, d = x1.shape
    assert d == SUBLANES * LANE
    n_e = w2.shape[0]
    n_tiles = (t * TOP_K) // MOE_TILE + n_e
    dest, counts = _route(route)
    tiles_per = jnp.ceil(counts[0, :n_e] * (1.0 / MOE_TILE)).astype(jnp.int32)
    tile_end = jnp.cumsum(tiles_per)
    past = (tile_end[None, :] <= jnp.arange(n_tiles, dtype=jnp.int32)[:, None]).astype(jnp.int32)
    tile_expert = jnp.minimum(jnp.sum(past, axis=1), n_e - 1).astype(jnp.int32)
    n_used = tile_end[-1:].astype(jnp.int32)
    dest_flat = dest[:, :TOP_K].reshape(-1)
    n_tok = counts[0, :n_e].astype(jnp.int32)
    pad_start = (tile_end - tiles_per) * MOE_TILE + n_tok
    pad_len = tiles_per * MOE_TILE - n_tok
    xs = _dispatch(pad_start, pad_len, n_used, dest_flat, h, n_tiles * MOE_TILE)
    y = _experts(tile_expert, n_used, xs, w1, b1g, b1l, w2, b2)
    return _combine(dest_flat, route, x1, y)


def _pad_cols(w, width):
    return jnp.pad(w, ((0, 0), (0, width - w.shape[1])))


def _rope_tables(pos, reps):
    inv = ROPE_THETA ** (-jnp.arange(0, HEAD_DIM, 2, dtype=F32) / HEAD_DIM)
    ang = pos.astype(F32)[:, None] * inv[None, :]
    ang = jnp.concatenate([ang, ang], axis=-1)
    sign = jnp.concatenate([-jnp.ones((HEAD_DIM // 2,), F32), jnp.ones((HEAD_DIM // 2,), F32)])
    return jnp.tile(jnp.cos(ang), (1, reps)), jnp.tile(jnp.sin(ang) * sign, (1, reps))


def _layer(x, mem, g_mix, g_mem, w_in, b_merge, nsa_q_norm, nsa_k_norm, cmp_pe, cmp_w1, cmp_b1, cmp_w2,
           rwkv_shift_mix, rwkv_w0, rwkv_w_up, rwkv_a0, rwkv_a_up, rwkv_g_up, rwkv_k_k, rwkv_k_a, rwkv_r_k,
           rwkv_ln_w, rwkv_ln_b, mem_w_kv, mem_q_norm, mem_k_norm, w_branch, w_out, g_ffn,
           router_w, router_b, exp_w1, exp_b1, exp_w2, exp_b2, cos, sin, cos_c, sin_c):
    b, s, d = x.shape
    t = b * s
    x2 = x.reshape(t, d)

    o = 0
    parts = []
    for width in (NSA_WIDTH, 6 * NSA_KV_WIDTH, 3 * NSA_HEADS,
                  3 * RWKV_WIDTH + DECAY_RANK + AAA_RANK + GATE_RANK, MEM_WIDTH, N_BRANCHES * D_MODEL):
        parts.append(w_in[:, o:o + width])
        o += width
    w_q, w_kv, w_gate, w_rwkv, w_qm, w_merge = parts

    def rwkv_layout(m):
        r3 = m[:, :3 * RWKV_WIDTH]
        xw = m[:, 3 * RWKV_WIDTH:3 * RWKV_WIDTH + DECAY_RANK]
        xa = m[:, 3 * RWKV_WIDTH + DECAY_RANK:3 * RWKV_WIDTH + DECAY_RANK + AAA_RANK]
        xg = m[:, 3 * RWKV_WIDTH + DECAY_RANK + AAA_RANK:]
        return _pad_cols(jnp.concatenate([r3, _pad_cols(xw, LANE), _pad_cols(xa, LANE), xg], axis=1), RWKV_PAD)

    w_f32 = jnp.concatenate([rwkv_layout(w_rwkv), w_kv[:, :2 * NSA_KV_WIDTH]], axis=1).astype(BF16)
    w_b16 = jnp.concatenate([w_merge, w_q, w_qm, w_kv[:, 2 * NSA_KV_WIDTH:], w_gate], axis=1)
    w_b16 = _pad_cols(w_b16, PROJ_BF16_WIDTH).astype(BF16)
    g_row = g_mix.reshape(1, d)
    proj3 = _inproj(x2, g_row, w_f32, F32, PROJ_F32_WIDTH).reshape(b, s, PROJ_F32_WIDTH)
    proj2 = _inproj(x2, g_row, w_b16, BF16, PROJ_BF16_WIDTH // 2)
    proj3b = proj2.reshape(b, s, PROJ_BF16_WIDTH)

    rowv = lambda a: a.reshape(1, -1)
    pad_rows = lambda m: jnp.pad(m, ((0, LANE - m.shape[0]), (0, 0)))
    o_b = _rwkv(proj3, rwkv_layout(rowv(rwkv_shift_mix)), rowv(rwkv_w0), pad_rows(rwkv_w_up), rowv(rwkv_a0),
                pad_rows(rwkv_a_up), rwkv_g_up, rowv(rwkv_k_k), rowv(rwkv_k_a), rowv(rwkv_r_k),
                rowv(rwkv_ln_w), rowv(rwkv_ln_b))

    q_gain = jnp.tile(nsa_q_norm.reshape(1, HEAD_DIM), (1, NSA_HEADS))
    k_gain = jnp.tile(nsa_k_norm, (1, NSA_KV_HEADS))
    qn, ks, vs, kw, vw = _nsa_prep(proj3b, cos, sin, q_gain, k_gain)
    kc, vc = _nsa_cmp(proj3, cos_c, sin_c, k_gain, cmp_pe, cmp_w1.astype(BF16),
                      cmp_b1.reshape(2, 1, CMP_HIDDEN), cmp_w2.astype(BF16))
    o_a = _nsa_attn(qn, proj3b, kc, vc, ks, vs, kw, vw)

    mk, mv = _mem_kv(mem, g_mem.reshape(1, d), mem_w_kv.astype(BF16), mem_k_norm.reshape(1, MEM_HEAD_DIM))
    o_m = _mem_attn(proj3b, mem_q_norm.reshape(1, MEM_HEAD_DIM), mk, mv)

    rw = _pad_cols(router_w, LANE)
    rb = jnp.concatenate([router_b, jnp.full((LANE - N_EXPERTS,), NEG_BIG, F32)]).reshape(1, LANE)
    x1, h2, route = _merge(o_a.reshape(t, NSA_WIDTH), o_b.reshape(t, RWKV_WIDTH), o_m.reshape(t, MEM_WIDTH),
                          proj2, x2, b_merge.reshape(1, -1), w_branch.astype(BF16), w_out.astype(BF16),
                          g_ffn.reshape(1, d), rw, rb)

    b1g = exp_b1[:, None, 0::2]
    b1l = exp_b1[:, None, 1::2]
    out = _moe(h2, route, x1, exp_w1, b1g, b1l, exp_w2, exp_b2[:, None, :])
    return out.reshape(b, s, d)


def kernel(x, mem, g_mix, g_mem, w_in, b_merge, nsa_q_norm, nsa_k_norm, cmp_pe, cmp_w1, cmp_b1, cmp_w2,
           rwkv_shift_mix, rwkv_w0, rwkv_w_up, rwkv_a0, rwkv_a_up, rwkv_g_up, rwkv_k_k, rwkv_k_a, rwkv_r_k,
           rwkv_ln_w, rwkv_ln_b, mem_w_kv, mem_q_norm, mem_k_norm, w_branch, w_out, g_ffn,
           router_w, router_b, exp_w1, exp_b1, exp_w2, exp_b2):
    s = x.shape[1]
    cos, sin = _rope_tables(jnp.arange(s), NSA_HEADS)
    n_cmp = (s - CMP_LEN) // CMP_STRIDE + 1
    cos_c, sin_c = _rope_tables(jnp.arange(n_cmp + 1) * CMP_STRIDE + CMP_LEN - 1, NSA_KV_HEADS)
    depth = g_mix.shape[0]
    for l in range(depth):
        x = _layer(x, mem, g_mix[l], g_mem[l], w_in[l], b_merge[l], nsa_q_norm[l], nsa_k_norm[l], cmp_pe[l],
                   cmp_w1[l], cmp_b1[l], cmp_w2[l], rwkv_shift_mix[l], rwkv_w0[l], rwkv_w_up[l], rwkv_a0[l],
                   rwkv_a_up[l], rwkv_g_up[l], rwkv_k_k[l], rwkv_k_a[l], rwkv_r_k[l], rwkv_ln_w[l], rwkv_ln_b[l],
                   mem_w_kv[l], mem_q_norm[l], mem_k_norm[l], w_branch[l], w_out[l], g_ffn[l], router_w[l],
                   router_b[l], exp_w1[l], exp_b1[l], exp_w2[l], exp_b2[l], cos, sin, cos_c, sin_c)
    return x
```
